```python
import jax, jax.numpy as jnp
from jax import lax
import numpy as np

D_MODEL = 4096
BATCH = 4
SEQ = 2048
DEPTH = 2

GRID_W = 64
CTX_LEN = 256
N_MIXERS = 2
N_MLSTM_LAYERS = (DEPTH + 1) // 2
N_ATTN_LAYERS = DEPTH // 2

ML_HEADS = 8
ML_DQK = D_MODEL // ML_HEADS // 2
ML_DV = D_MODEL // ML_HEADS
ML_CHUNK = 128
ML_IN = ML_HEADS * (2 * ML_DQK + 2 * ML_DV) + 4 * ML_HEADS
AT_HEAD_DIM = 128
AT_HEADS = D_MODEL // AT_HEAD_DIM
AT_KV_HEADS = 8
AT_GROUP = AT_HEADS // AT_KV_HEADS
AT_WINDOW = 128
AT_BLOCK = 128
AT_IN = (AT_HEADS + 2 * AT_KV_HEADS) * AT_HEAD_DIM
ROPE_BASE = 10000.0
N_EXPERTS = 16
N_GROUPS = 4
EXPERTS_PER_GROUP = N_EXPERTS // N_GROUPS
TOP_K = 2
D_FF_EXPERT = D_MODEL // 4
DN_ALPHA = (2 * DEPTH) ** 0.25
DN_BETA = (8 * DEPTH) ** -0.25
LN_EPS = 1e-5
RMS_EPS = 1e-6

kernel_name = "hybrid_mlstm_swa_moe_diffusion_block"


def layer_norm(x, g, b):
    xf = x.astype(jnp.float32)
    mu = jnp.mean(xf, -1, keepdims=True)
    var = jnp.mean(jnp.square(xf - mu), -1, keepdims=True)
    return ((xf - mu) * lax.rsqrt(var + LN_EPS) * g + b).astype(x.dtype)


def modulate(x, shift, scale):
    return x * (1 + scale) + shift


def mlstm_chunk_step(carry, xs):
    C, n, m = carry
    q, k, v, ig, lf = xs
    L = q.shape[2]
    b = jnp.cumsum(lf, axis=-1)
    prior = jnp.tril(jnp.ones((L, L), dtype=bool))
    logw = jnp.where(prior, b[..., :, None] - b[..., None, :] + ig[..., None, :], -jnp.inf)
    inter = b + m[..., None]
    m_t = jnp.maximum(inter, jnp.max(logw, -1))
    w_inter = jnp.exp(inter - m_t)
    s = jnp.einsum('bhtd,bhsd->bhts', q, k) * jnp.exp(logw - m_t[..., None])
    num = w_inter[..., None] * jnp.einsum('bhvd,bhtd->bhtv', C, q) + jnp.einsum('bhts,bhsv->bhtv', s, v)
    den = w_inter * jnp.einsum('bhd,bhtd->bht', n, q) + jnp.sum(s, -1)
    h = num / jnp.maximum(jnp.abs(den), jnp.exp(-m_t))[..., None]
    b_end = b[..., -1]
    log_src = b_end[..., None] - b + ig
    m_new = jnp.maximum(b_end + m, jnp.max(log_src, -1))
    decay = jnp.exp(b_end + m - m_new)
    w_src = jnp.exp(log_src - m_new[..., None])
    C_new = decay[..., None, None] * C + jnp.einsum('bhsv,bhsd->bhvd', v * w_src[..., None], k)
    n_new = decay[..., None] * n + jnp.einsum('bhs,bhsd->bhd', w_src, k)
    return (C_new, n_new, m_new), h


def mlstm_scan(q, k, v, ig, lf, state):
    Bsz, H, N, _ = q.shape
    nc = N // ML_CHUNK

    def chunks(a):
        return jnp.moveaxis(a.reshape(Bsz, H, nc, ML_CHUNK, *a.shape[3:]), 2, 0)

    state, h = lax.scan(mlstm_chunk_step, state, (chunks(q), chunks(k), chunks(v), chunks(ig), chunks(lf)))
    h = jnp.moveaxis(h, 0, 2).reshape(Bsz, H, N, ML_DV)
    return h, state


def mlstm_mixer(hx, hc, w_in, b_gates, norm_w, w_out, need_ctx):
    split_at = [ML_HEADS * ML_DQK, 2 * ML_HEADS * ML_DQK,
                2 * ML_HEADS * ML_DQK + ML_HEADS * ML_DV, 2 * ML_HEADS * (ML_DQK + ML_DV)]

    def project(h):
        q, k, v, o, g = jnp.split(h @ w_in, split_at, axis=-1)
        Bsz, N = h.shape[:2]
        heads = lambda a, d: jnp.moveaxis(a.reshape(Bsz, N, ML_HEADS, d), 2, 1).astype(jnp.float32)
        g = jnp.moveaxis((g + b_gates).reshape(Bsz, N, 4, ML_HEADS), 1, -1).astype(jnp.float32)
        return heads(q, ML_DQK) * (ML_DQK ** -0.5), heads(k, ML_DQK), heads(v, ML_DV), o, g

    qc, kc, vc, oc, gc = project(hc)
    qx, kx, vx, ox, gx = project(hx)
    Bsz = hx.shape[0]
    zero = (jnp.zeros((Bsz, ML_HEADS, ML_DV, ML_DQK), jnp.float32),
            jnp.zeros((Bsz, ML_HEADS, ML_DQK), jnp.float32),
            jnp.zeros((Bsz, ML_HEADS), jnp.float32))
    flip = lambda a: jnp.flip(a, axis=2)
    lsig = jax.nn.log_sigmoid
    hc_f, st_f = mlstm_scan(qc, kc, vc, gc[:, 0], lsig(gc[:, 1]), zero)
    hx_f, _ = mlstm_scan(qx, kx, vx, gx[:, 0], lsig(gx[:, 1]), st_f)
    hc_b, st_b = mlstm_scan(flip(qc), flip(kc), flip(vc), flip(gc[:, 2]), flip(lsig(gc[:, 3])), zero)
    hx_b, _ = mlstm_scan(flip(qx), flip(kx), flip(vx), flip(gx[:, 2]), flip(lsig(gx[:, 3])), st_b)

    def out(h, o):
        h = h * lax.rsqrt(jnp.mean(h * h, -1, keepdims=True) + RMS_EPS)
        h = jnp.moveaxis(h, 1, 2).reshape(h.shape[0], h.shape[2], ML_HEADS * ML_DV)
        return (h * norm_w * jax.nn.sigmoid(o.astype(jnp.float32))).astype(o.dtype) @ w_out

    y_lat = out(hx_f + flip(hx_b), ox)
    y_ctx = out(hc_f + flip(hc_b), oc) if need_ctx else None
    return y_ctx, y_lat


def axial_rope(n):
    t = jnp.arange(n)
    half = AT_HEAD_DIM // 2
    inv = jnp.power(ROPE_BASE, -jnp.arange(0, half, 2, dtype=jnp.float32) / half)
    ang_r = (t // GRID_W).astype(jnp.float32)[:, None] * inv
    ang_c = (t % GRID_W).astype(jnp.float32)[:, None] * inv
    return jnp.cos(ang_r), jnp.sin(ang_r), jnp.cos(ang_c), jnp.sin(ang_c)


def rotate(x, cos, sin):
    x1, x2 = jnp.split(x, 2, axis=-1)
    cos, sin = cos[None, :, None, :], sin[None, :, None, :]
    return jnp.concatenate([x1 * cos - x2 * sin, x2 * cos + x1 * sin], -1)


def apply_axial_rope(x, rope):
    cr, sr, cc, sc = rope
    xr, xcol = jnp.split(x.astype(jnp.float32), 2, axis=-1)
    return jnp.concatenate([rotate(xr, cr, sr), rotate(xcol, cc, sc)], -1)


def window_attention_mixer(hx, hc, w_in, b_in, sink, w_out, need_ctx):
    Bsz, N, _ = hx.shape
    Lc = hc.shape[1]
    f32 = jnp.float32

    def project(h):
        q, k, v = jnp.split(h @ w_in + b_in, [AT_HEADS * AT_HEAD_DIM, (AT_HEADS + AT_KV_HEADS) * AT_HEAD_DIM], axis=-1)
        sh = h.shape[:2]
        return (q.reshape(*sh, AT_HEADS, AT_HEAD_DIM), k.reshape(*sh, AT_KV_HEADS, AT_HEAD_DIM),
                v.reshape(*sh, AT_KV_HEADS, AT_HEAD_DIM))

    qx, kx, vx = project(hx)
    qc, kc, vc = project(hc)
    rope = axial_rope(N)
    scale = AT_HEAD_DIM ** -0.5
    qx = (apply_axial_rope(qx, rope) * scale).reshape(Bsz, N, AT_KV_HEADS, AT_GROUP, AT_HEAD_DIM)
    kx = apply_axial_rope(kx, rope)
    vx = vx.astype(f32)
    kc, vc = kc.astype(f32), vc.astype(f32)
    sink_logit = sink.astype(f32).reshape(AT_KV_HEADS, AT_GROUP)[None, :, :, None, None]

    nb = N // AT_BLOCK
    pad = ((0, 0), (AT_BLOCK, AT_BLOCK), (0, 0), (0, 0))
    kp, vp = jnp.pad(kx, pad), jnp.pad(vx, pad)
    rel = (jnp.arange(3 * AT_BLOCK)[None, :] - AT_BLOCK) - jnp.arange(AT_BLOCK)[:, None]
    in_window = jnp.abs(rel) <= AT_WINDOW

    def block(j):
        qb = lax.dynamic_slice_in_dim(qx, j * AT_BLOCK, AT_BLOCK, axis=1)
        kb = lax.dynamic_slice_in_dim(kp, j * AT_BLOCK, 3 * AT_BLOCK, axis=1)
        vb = lax.dynamic_slice_in_dim(vp, j * AT_BLOCK, 3 * AT_BLOCK, axis=1)
        key_pos = (j - 1) * AT_BLOCK + jnp.arange(3 * AT_BLOCK)
        valid = in_window & ((key_pos >= 0) & (key_pos < N))[None, :]
        s_loc = jnp.where(valid, jnp.einsum('bukgd,brkd->bkgur', qb, kb), -jnp.inf)
        s_ctx = jnp.einsum('bukgd,bckd->bkguc', qb, kc)
        logits = jnp.concatenate([s_loc, s_ctx, jnp.broadcast_to(sink_logit, s_loc.shape[:-1] + (1,))], -1)
        p = jax.nn.softmax(logits, axis=-1)
        return (jnp.einsum('bkgur,brkd->bukgd', p[..., :3 * AT_BLOCK], vb)
                + jnp.einsum('bkguc,bckd->bukgd', p[..., 3 * AT_BLOCK:3 * AT_BLOCK + Lc], vc))

    o_lat = lax.map(block, jnp.arange(nb))
    o_lat = jnp.moveaxis(o_lat, 0, 1).reshape(Bsz, N, AT_HEADS * AT_HEAD_DIM)
    y_lat = o_lat.astype(hx.dtype) @ w_out

    y_ctx = None
    if need_ctx:
        qcs = (qc.astype(f32) * scale).reshape(Bsz, Lc, AT_KV_HEADS, AT_GROUP, AT_HEAD_DIM)
        s = jnp.einsum('bukgd,bckd->bkguc', qcs, kc)
        p = jax.nn.softmax(jnp.concatenate([s, jnp.broadcast_to(sink_logit, s.shape[:-1] + (1,))], -1), axis=-1)
        o = jnp.einsum('bkguc,bckd->bukgd', p[..., :Lc], vc).reshape(Bsz, Lc, AT_HEADS * AT_HEAD_DIM)
        y_ctx = o.astype(hc.dtype) @ w_out
    return y_ctx, y_lat


def route(t, router_w, router_b):
    s = jax.nn.sigmoid((t @ router_w).astype(jnp.float32))
    sel = (s + router_b.astype(jnp.float32)).reshape(-1, N_GROUPS, EXPERTS_PER_GROUP)
    group_score = jnp.sum(lax.top_k(sel, TOP_K)[0], -1)
    g_idx = jnp.argmax(group_score, -1)
    sel_in = jnp.take_along_axis(sel, g_idx[:, None, None], axis=1)[:, 0]
    _, local = lax.top_k(sel_in, TOP_K)
    expert = g_idx[:, None] * EXPERTS_PER_GROUP + local
    w = jnp.take_along_axis(s, expert, axis=1)
    w = w / jnp.sum(w, -1, keepdims=True)
    return jnp.sum(jax.nn.one_hot(expert, N_EXPERTS, dtype=jnp.float32) * w[..., None], axis=1)


def moe_ffn(t, router_w, router_b, w1, w3, w2):
    gates = route(t, router_w, router_b).astype(t.dtype)
    y = jnp.zeros_like(t)
    for e in range(N_EXPERTS):
        a = jax.nn.silu(t @ w1[e]) * (t @ w3[e])
        y = y + gates[:, e:e + 1] * (a @ w2[e])
    return y


def setup_inputs(seed: int = 0) -> dict:
    key = jax.random.key(seed)
    ks = jax.random.split(key, 24)
    f32 = jnp.float32
    D = D_MODEL
    nrm = lambda k, shape, s: jax.random.normal(k, shape, f32) * s
    ig_b = nrm(ks[9], (N_MLSTM_LAYERS, 2, ML_HEADS), 0.1)
    fg_b = 3.0 + 3.0 * jax.random.uniform(ks[10], (N_MLSTM_LAYERS, 2, ML_HEADS), f32)
    mlstm_b_gates = jnp.stack([ig_b, fg_b], axis=2).reshape(N_MLSTM_LAYERS, 4 * ML_HEADS)
    return {
        "x": nrm(ks[0], (BATCH, SEQ, D), 1.0),
        "c": nrm(ks[1], (BATCH, D), 1.0),
        "ctx": nrm(ks[2], (BATCH, CTX_LEN, D), 1.0),
        "c_ctx": nrm(ks[3], (D,), 1.0),
        "ada_w": nrm(ks[4], (DEPTH, D, 6 * D), 0.5 * D ** -0.5),
        "ada_b": nrm(ks[5], (DEPTH, 6 * D), 0.02),
        "ln_g": 1.0 + nrm(ks[6], (DEPTH, 2, D), 0.02),
        "ln_b": nrm(ks[7], (DEPTH, 2, D), 0.02),
        "mlstm_w_in": nrm(ks[8], (N_MLSTM_LAYERS, D, ML_IN), D ** -0.5),
        "mlstm_b_gates": mlstm_b_gates,
        "mlstm_norm_w": 1.0 + nrm(ks[11], (N_MLSTM_LAYERS, ML_HEADS * ML_DV), 0.02),
        "mlstm_w_out": nrm(ks[12], (N_MLSTM_LAYERS, ML_HEADS * ML_DV, D), DN_BETA * (ML_HEADS * ML_DV) ** -0.5),
        "attn_w_in": nrm(ks[13], (N_ATTN_LAYERS, D, AT_IN), D ** -0.5),
        "attn_b_in": nrm(ks[14], (N_ATTN_LAYERS, AT_IN), 0.02),
        "attn_sink": nrm(ks[15], (N_ATTN_LAYERS, AT_HEADS), 1.0),
        "attn_w_out": nrm(ks[16], (N_ATTN_LAYERS, AT_HEADS * AT_HEAD_DIM, D), DN_BETA * (AT_HEADS * AT_HEAD_DIM) ** -0.5),
        "router_w": nrm(ks[17], (D, N_EXPERTS), D ** -0.5),
        "router_b": nrm(ks[18], (N_EXPERTS,), 0.01),
        "moe_w1": nrm(ks[19], (DEPTH, N_EXPERTS, D, D_FF_EXPERT), D ** -0.5),
        "moe_w3": nrm(ks[20], (DEPTH, N_EXPERTS, D, D_FF_EXPERT), D ** -0.5),
        "moe_w2": nrm(ks[21], (DEPTH, N_EXPERTS, D_FF_EXPERT, D), DN_BETA * D_FF_EXPERT ** -0.5),
    }


def reference(x, c, ctx, c_ctx, ada_w, ada_b, ln_g, ln_b, mlstm_w_in, mlstm_b_gates, mlstm_norm_w, mlstm_w_out,
              attn_w_in, attn_b_in, attn_sink, attn_w_out, router_w, router_b, moe_w1, moe_w3, moe_w2):
    Bsz, N, D = x.shape
    Lc = ctx.shape[1]
    for i in range(DEPTH):
        need_ctx = i < DEPTH - 1
        sh_m, sc_m, g_m, sh_f, sc_f, g_f = [a[:, None, :] for a in jnp.split(jax.nn.silu(c) @ ada_w[i] + ada_b[i], 6, axis=-1)]
        csh_m, csc_m, cg_m, csh_f, csc_f, cg_f = jnp.split(jax.nn.silu(c_ctx) @ ada_w[i] + ada_b[i], 6, axis=-1)
        hx = modulate(x, sh_m, sc_m)
        hc = modulate(ctx, csh_m, csc_m)
        j = i // N_MIXERS
        if i % N_MIXERS == 0:
            y_ctx, y_lat = mlstm_mixer(hx, hc, mlstm_w_in[j], mlstm_b_gates[j], mlstm_norm_w[j], mlstm_w_out[j], need_ctx)
        else:
            y_ctx, y_lat = window_attention_mixer(hx, hc, attn_w_in[j], attn_b_in[j], attn_sink[j], attn_w_out[j], need_ctx)
        x = layer_norm(DN_ALPHA * x + g_m * y_lat, ln_g[i, 0], ln_b[i, 0])
        if need_ctx:
            ctx = layer_norm(DN_ALPHA * ctx + cg_m * y_ctx, ln_g[i, 0], ln_b[i, 0])
        hx = modulate(x, sh_f, sc_f).reshape(-1, D)
        if need_ctx:
            hc = modulate(ctx, csh_f, csc_f).reshape(-1, D)
            y = moe_ffn(jnp.concatenate([hc, hx], axis=0), router_w, router_b, moe_w1[i], moe_w3[i], moe_w2[i])
            y_ctx, y_lat = y[:Bsz * Lc].reshape(ctx.shape), y[Bsz * Lc:].reshape(x.shape)
            ctx = layer_norm(DN_ALPHA * ctx + cg_f * y_ctx, ln_g[i, 1], ln_b[i, 1])
        else:
            y_lat = moe_ffn(hx, router_w, router_b, moe_w1[i], moe_w3[i], moe_w2[i]).reshape(x.shape)
        x = layer_norm(DN_ALPHA * x + g_f * y_lat, ln_g[i, 1], ln_b[i, 1])
    return x
```

```python
import functools

import jax
import jax.numpy as jnp
from jax import lax
from jax.experimental import pallas as pl
from jax.experimental.pallas import tpu as pltpu

F32 = jnp.float32
BF16 = jnp.bfloat16
NEG_INF = float("-inf")

GRID_W = 64
ML_CHUNK = 128
AT_WINDOW = 128
AT_BLOCK = 128
ROPE_BASE = 10000.0
N_GROUPS = 4
TOP_K = 2
LN_EPS = 1e-5
RMS_EPS = 1e-6

LANES = 128
VMEM_LIMIT = 56 * 1024 * 1024
MOE_TILE = 256
HIGHEST = lax.Precision.HIGHEST
NT_DIMS = (((1,), (1,)), ((), ()))


def _cparams(*sem):
    return pltpu.CompilerParams(dimension_semantics=sem, vmem_limit_bytes=VMEM_LIMIT)


def _pick(n, candidates):
    for c in candidates:
        if n % c == 0:
            return c
    return n


def _sigmoid(x):
    return 1.0 / (1.0 + jnp.exp(-x))


def _log_sigmoid(x):
    return jnp.minimum(x, 0.0) - jnp.log1p(jnp.exp(-jnp.abs(x)))


def _adaln_kernel(c_ref, w_ref, b_ref, o_ref):
    c = c_ref[...]
    a = (c * _sigmoid(c)).astype(BF16)
    o_ref[0] = jnp.dot(a, w_ref[0].astype(BF16), preferred_element_type=F32) + b_ref[0]


def adaln(cc, ada_w, ada_b):
    depth, d, n6 = ada_w.shape
    tn = _pick(n6, (512, 256, 128))
    return pl.pallas_call(
        _adaln_kernel,
        grid=(depth, n6 // tn),
        in_specs=[
            pl.BlockSpec((8, d), lambda l, j: (0, 0)),
            pl.BlockSpec((1, d, tn), lambda l, j: (l, 0, j)),
            pl.BlockSpec((1, 1, tn), lambda l, j: (l, 0, j)),
        ],
        out_specs=pl.BlockSpec((1, 8, tn), lambda l, j: (l, 0, j)),
        out_shape=jax.ShapeDtypeStruct((depth, 8, n6), F32),
        compiler_params=_cparams("parallel", "parallel"),
        name="adaln",
    )(cc, ada_w, ada_b.reshape(depth, 1, n6))


def _mod0_kernel(x_ref, c_ref, sh_ref, sc_ref, wg_ref, bg_ref, h_ref, g_ref):
    j = pl.program_id(1)
    t = jnp.where(j == 0, c_ref[0], x_ref[0])
    h = t * (1.0 + sc_ref[0]) + sh_ref[0]
    h_ref[...] = h.astype(BF16)
    g_ref[...] = jnp.dot(h, wg_ref[...], preferred_element_type=F32, precision=HIGHEST) + bg_ref[...]


def mod0(x, ctx, ada3, wg, bg):
    bsz, n, d = x.shape
    lc = ctx.shape[1]
    per = (n + lc) // lc
    ng = wg.shape[1]
    prow = lambda b, j: jnp.where(j == 0, bsz, b)
    return pl.pallas_call(
        _mod0_kernel,
        grid=(bsz, per),
        in_specs=[
            pl.BlockSpec((1, lc, d), lambda b, j: (b, jnp.maximum(j - 1, 0), 0)),
            pl.BlockSpec((1, lc, d), lambda b, j: (b, 0, 0)),
            pl.BlockSpec((1, 1, d), lambda b, j: (prow(b, j), 0, 0)),
            pl.BlockSpec((1, 1, d), lambda b, j: (prow(b, j), 0, 1)),
            pl.BlockSpec((d, ng), lambda b, j: (0, 0)),
            pl.BlockSpec((1, ng), lambda b, j: (0, 0)),
        ],
        out_specs=[
            pl.BlockSpec((lc, d), lambda b, j: (b * per + j, 0)),
            pl.BlockSpec((lc, ng), lambda b, j: (b * per + j, 0)),
        ],
        out_shape=[
            jax.ShapeDtypeStruct((bsz * (n + lc), d), BF16),
            jax.ShapeDtypeStruct((bsz * (n + lc), ng), F32),
        ],
        compiler_params=_cparams("parallel", "arbitrary"),
        name="mod0",
    )(x, ctx, ada3, ada3, wg, bg)


def _mm_kernel(*refs, has_bias, has_rope, tn):
    x_ref, w_ref = refs[0], refs[1]
    pos = 2
    acc = jnp.dot(x_ref[...], w_ref[...].astype(BF16), preferred_element_type=F32)
    if has_bias:
        acc = acc + refs[pos][...]
        pos += 1
    o_ref = refs[-1]
    if has_rope:
        ca = refs[pos][0]
        sb = refs[pos + 1][0]
        lane = lax.broadcasted_iota(jnp.int32, ca.shape, 1)
        first = (lane % 64) < 32
        for h in range(tn // LANES):
            y = acc[:, h * LANES:(h + 1) * LANES]
            sw = jnp.where(first, pltpu.roll(y, 96, 1), pltpu.roll(y, 32, 1))
            o_ref[:, h * LANES:(h + 1) * LANES] = (y * ca + sw * sb).astype(o_ref.dtype)
    else:
        o_ref[...] = acc.astype(o_ref.dtype)


def matmul(x, w, bias=None, rope=None, out_dtype=BF16, tm=None, tn=None, n_cols=None):
    m, k = x.shape
    n = n_cols or w.shape[1]
    tm = tm or _pick(m, (1152, 1024, 768, 512, 256, 128))
    tn = tn or _pick(n, (512, 256, 128))
    in_specs = [
        pl.BlockSpec((tm, k), lambda i, j: (i, 0)),
        pl.BlockSpec((k, tn), lambda i, j: (0, j)),
    ]
    args = [x, w]
    if bias is not None:
        in_specs.append(pl.BlockSpec((1, tn), lambda i, j: (0, j)))
        args.append(bias.reshape(1, n))
    if rope is not None:
        ca, sb, period, type_of = rope
        nper = period // tm
        spec = pl.BlockSpec((1, tm, LANES), lambda i, j: (type_of(j * tn), i % nper, 0))
        in_specs += [spec, spec]
        args += [ca, sb]
    return pl.pallas_call(
        functools.partial(_mm_kernel, has_bias=bias is not None, has_rope=rope is not None, tn=tn),
        grid=(m // tm, n // tn),
        in_specs=in_specs,
        out_specs=pl.BlockSpec((tm, tn), lambda i, j: (i, j)),
        out_shape=jax.ShapeDtypeStruct((m, n), out_dtype),
        compiler_params=_cparams("parallel", "arbitrary"),
        name="matmul",
    )(*args)


def _mlstm_kernel(*refs, reverse, scale):
    if reverse:
        q_ref, k_ref, v_ref, gr_ref, gc_ref, hf_ref, o_ref, nw_ref, out_ref, st_ref, n_ref, m_ref = refs
    else:
        q_ref, k_ref, v_ref, gr_ref, gc_ref, out_ref, st_ref, n_ref, m_ref = refs
    L = ML_CHUNK

    @pl.when(pl.program_id(2) == 0)
    def _():
        st_ref[...] = jnp.zeros_like(st_ref)
        n_ref[...] = jnp.zeros_like(n_ref)
        m_ref[...] = jnp.zeros_like(m_ref)

    gi = 2 if reverse else 0
    ig_row = gr_ref[0, gi, 0]
    lf_row = _log_sigmoid(gr_ref[0, gi + 1, 0])
    gcol = gc_ref[0, 0]
    ig_col = gcol[:, gi:gi + 1]
    lf_col = _log_sigmoid(gcol[:, gi + 1:gi + 2])

    r = lax.broadcasted_iota(jnp.int32, (L, L), 0)
    c = lax.broadcasted_iota(jnp.int32, (L, L), 1)
    keep = (c >= r) if reverse else (c <= r)
    keep_t = (r >= c) if reverse else (r <= c)
    b_col = jnp.sum(jnp.where(keep, lf_row, 0.0), axis=1, keepdims=True)
    b_row = jnp.sum(jnp.where(keep_t, lf_col, 0.0), axis=0, keepdims=True)

    m_prev = m_ref[...]
    logw = jnp.where(keep, b_col - b_row + ig_row, NEG_INF)
    inter = b_col + m_prev
    m_t = jnp.maximum(inter, jnp.max(logw, axis=1, keepdims=True))
    w_inter = jnp.exp(inter - m_t)
    p = jnp.exp(logw - m_t)

    q = (q_ref[...].astype(F32) * scale).astype(BF16)
    k = k_ref[...]
    v = v_ref[...]
    st = st_ref[...]
    n_row = n_ref[...]
    s = lax.dot_general(q, k, NT_DIMS, preferred_element_type=F32) * p
    num = (w_inter * jnp.dot(q, st.astype(BF16), preferred_element_type=F32)
           + jnp.dot(s.astype(BF16), v, preferred_element_type=F32))
    qn = jnp.sum(q.astype(F32) * n_row, axis=1, keepdims=True)
    den = w_inter * qn + jnp.sum(s, axis=1, keepdims=True)
    h = num / jnp.maximum(jnp.abs(den), jnp.exp(-m_t))

    b_end = b_col[0:1, :] if reverse else b_col[L - 1:L, :]
    m_new = jnp.maximum(b_end + m_prev, jnp.max(b_end - b_row + ig_row, axis=1, keepdims=True))
    decay = jnp.exp(b_end + m_prev - m_new)
    w_src = jnp.exp(b_end - b_col + ig_col - m_new)
    kf = k.astype(F32)
    vw = (v.astype(F32) * w_src).astype(BF16)
    st_ref[...] = decay * st + jnp.dot(kf.T.astype(BF16), vw, preferred_element_type=F32)
    n_ref[...] = decay * n_row + jnp.sum(kf * w_src, axis=0, keepdims=True)
    m_ref[...] = m_new

    if reverse:
        hs = hf_ref[...] + h
        hn = hs * lax.rsqrt(jnp.mean(hs * hs, axis=1, keepdims=True) + RMS_EPS)
        out_ref[...] = (hn * nw_ref[...] * _sigmoid(o_ref[...].astype(F32))).astype(out_ref.dtype)
    else:
        out_ref[...] = h


def mlstm_scan(qkvo, g_rows, g_cols, bsz, heads, dqk, dv, ncc, nct, hf=None, norm_w=None):
    reverse = hf is not None
    t = qkvo.shape[0]
    L = ML_CHUNK
    if reverse:
        cidx = lambda j: jnp.where(j < ncc, ncc - 1 - j, nct - 1 - (j - ncc))
    else:
        cidx = lambda j: j
    row = lambda b, j: b * nct + cidx(j)
    koff = heads
    voff = 2 * heads * dqk // dv
    ooff = voff + heads
    in_specs = [
        pl.BlockSpec((L, dqk), lambda b, h, j: (row(b, j), h)),
        pl.BlockSpec((L, dqk), lambda b, h, j: (row(b, j), koff + h)),
        pl.BlockSpec((L, dv), lambda b, h, j: (row(b, j), voff + h)),
        pl.BlockSpec((1, 4, 1, 1, L), lambda b, h, j: (b, 0, h, 0, cidx(j))),
        pl.BlockSpec((1, 1, L, 4), lambda b, h, j: (b, h, cidx(j), 0)),
    ]
    args = [qkvo, qkvo, qkvo, g_rows, g_cols]
    if reverse:
        in_specs += [
            pl.BlockSpec((L, dv), lambda b, h, j: (row(b, j), h)),
            pl.BlockSpec((L, dv), lambda b, h, j: (row(b, j), ooff + h)),
            pl.BlockSpec((1, dv), lambda b, h, j: (0, h)),
        ]
        args += [hf, qkvo, norm_w.reshape(1, heads * dv)]
    return pl.pallas_call(
        functools.partial(_mlstm_kernel, reverse=reverse, scale=dqk ** -0.5),
        grid=(bsz, heads, nct),
        in_specs=in_specs,
        out_specs=pl.BlockSpec((L, dv), lambda b, h, j: (row(b, j), h)),
        out_shape=jax.ShapeDtypeStruct((t, heads * dv), BF16 if reverse else F32),
        scratch_shapes=[pltpu.VMEM((dqk, dv), F32), pltpu.VMEM((1, dqk), F32), pltpu.VMEM((1, 1), F32)],
        compiler_params=_cparams("parallel", "parallel", "arbitrary"),
        name="mlstm_bwd" if reverse else "mlstm_fwd",
    )(*args)


def _attn_kernel(sink_ref, q_ref, k_ref, v_ref, o_ref, *, lc, nb, group, hd):
    kv = pl.program_id(1)
    blk = AT_BLOCK
    rows = group * blk
    kc = k_ref[0, 0:lc, :]
    vc = v_ref[0, 0:lc, :]
    rid = lax.broadcasted_iota(jnp.int32, (rows, 1), 0)
    sink_col = jnp.zeros((rows, 1), F32)
    for g in range(group):
        sink_col = jnp.where(rid // blk == g, sink_ref[kv * group + g], sink_col)
    qrow = lax.broadcasted_iota(jnp.int32, (rows, 3 * blk), 0) % blk
    kcol = lax.broadcasted_iota(jnp.int32, (rows, 3 * blk), 1)

    def body(j, carry):
        ws = jnp.clip(j - 1, 0, nb - 3)
        qs = pl.multiple_of(lc + j * blk, blk)
        ks = pl.multiple_of(lc + ws * blk, blk)
        q4 = q_ref[0, pl.ds(qs, blk), :]
        qq = jnp.concatenate([q4[:, g * hd:(g + 1) * hd] for g in range(group)], axis=0)
        kw = k_ref[0, pl.ds(ks, 3 * blk), :]
        vw = v_ref[0, pl.ds(ks, 3 * blk), :]
        s_loc = lax.dot_general(qq, kw, NT_DIMS, preferred_element_type=F32)
        s_ctx = lax.dot_general(qq, kc, NT_DIMS, preferred_element_type=F32)
        rel = (ws - j) * blk + kcol - qrow
        s_loc = jnp.where(jnp.abs(rel) <= AT_WINDOW, s_loc, NEG_INF)
        m = jnp.maximum(jnp.maximum(jnp.max(s_loc, axis=1, keepdims=True),
                                    jnp.max(s_ctx, axis=1, keepdims=True)), sink_col)
        p_loc = jnp.exp(s_loc - m)
        p_ctx = jnp.exp(s_ctx - m)
        den = (jnp.sum(p_loc, axis=1, keepdims=True) + jnp.sum(p_ctx, axis=1, keepdims=True)
               + jnp.exp(sink_col - m))
        o = (jnp.dot(p_loc.astype(BF16), vw, preferred_element_type=F32)
             + jnp.dot(p_ctx.astype(BF16), vc, preferred_element_type=F32)) / den
        os = pl.multiple_of(j * blk, blk)
        for g in range(group):
            o_ref[0, pl.ds(os, blk), g * hd:(g + 1) * hd] = o[g * blk:(g + 1) * blk].astype(o_ref.dtype)
        return carry

    lax.fori_loop(0, nb, body, 0)


def window_attention(qkv3, sink, lc, n, heads, kv_heads, hd):
    bsz = qkv3.shape[0]
    group = heads // kv_heads
    nb = n // AT_BLOCK
    assert nb >= 3
    koff = heads
    voff = heads + kv_heads
    return pl.pallas_call(
        functools.partial(_attn_kernel, lc=lc, nb=nb, group=group, hd=hd),
        grid=(bsz, kv_heads),
        in_specs=[
            pl.BlockSpec(memory_space=pltpu.SMEM),
            pl.BlockSpec((1, lc + n, group * hd), lambda b, g: (b, 0, g)),
            pl.BlockSpec((1, lc + n, hd), lambda b, g: (b, 0, koff + g)),
            pl.BlockSpec((1, lc + n, hd), lambda b, g: (b, 0, voff + g)),
        ],
        out_specs=pl.BlockSpec((1, n, group * hd), lambda b, g: (b, 0, g)),
        out_shape=jax.ShapeDtypeStruct((bsz, n, heads * hd), BF16),
        compiler_params=_cparams("parallel", "parallel"),
        name="window_attention",
    )(sink, qkv3, qkv3, qkv3)


def _ln_kernel(*refs, dual, gather, tile, has_h, has_router, alpha):
    refs = list(refs)
    if gather:
        pos_ref = refs.pop(0)
    x_ref = refs.pop(0)
    c_ref = refs.pop(0) if dual else None
    y_ref = refs.pop(0)
    gate_ref, lng_ref, lnb_ref = refs.pop(0), refs.pop(0), refs.pop(0)
    if has_h:
        sh_ref, sc_ref = refs.pop(0), refs.pop(0)
    if has_router:
        rw_ref = refs.pop(0)
    xo_ref = refs.pop(0)
    if has_h:
        ho_ref = refs.pop(0)
    if has_router:
        lg_ref = refs.pop(0)

    if gather:
        buf_ref, sem = refs.pop(0), refs.pop(0)
        base = (pl.program_id(0) * pl.num_programs(1) + pl.program_id(1)) * tile
        ntok = pos_ref.shape[0] // 2

        def row_copy(r, slot):
            src = pos_ref[slot * ntok + base + r]
            return pltpu.make_async_copy(y_ref.at[pl.ds(src, 1)], buf_ref.at[slot, pl.ds(r, 1)], sem)

        def start(r, carry):
            row_copy(r, 0).start()
            row_copy(r, 1).start()
            return carry

        def wait(r, carry):
            row_copy(r, 0).wait()
            row_copy(r, 1).wait()
            return carry

        lax.fori_loop(0, tile, start, 0)
        lax.fori_loop(0, tile, wait, 0)
        y = buf_ref[0] + buf_ref[1]
    else:
        y = y_ref[...].astype(F32)

    if dual:
        x = jnp.where(pl.program_id(1) == 0, c_ref[0], x_ref[0])
    else:
        x = x_ref[...]
    z = alpha * x + gate_ref[0] * y
    mu = jnp.mean(z, axis=1, keepdims=True)
    zc = z - mu
    var = jnp.mean(zc * zc, axis=1, keepdims=True)
    xn = zc * lax.rsqrt(var + LN_EPS) * lng_ref[...] + lnb_ref[...]
    xo_ref[...] = xn
    if has_h:
        h = xn * (1.0 + sc_ref[0]) + sh_ref[0]
        ho_ref[...] = h.astype(ho_ref.dtype)
        if has_router:
            lg_ref[...] = lax.dot_general(rw_ref[...], h, NT_DIMS, preferred_element_type=F32,
                                          precision=HIGHEST)


def ln_block(x, y, ada3, gate_col, lng, lnb, *, alpha, nb, per, tile, ctx_rows, ctx=None, x_per=None,
             x_off=0, nxt=None, h_dtype=None, router_wt=None, pos=None):
    d = x.shape[-1]
    dual = ctx is not None
    gather = pos is not None
    rows = nb * per * tile
    prow = (lambda b, j: jnp.where(j == 0, nb, b)) if ctx_rows else (lambda b, j: b)

    def im(f):
        return lambda b, j, *_: f(b, j)

    in_specs, args = [], []
    if dual:
        in_specs.append(pl.BlockSpec((1, tile, d), im(lambda b, j: (b, jnp.maximum(j - 1, 0), 0))))
        in_specs.append(pl.BlockSpec((1, tile, d), im(lambda b, j: (b, 0, 0))))
        args += [x, ctx]
    else:
        in_specs.append(pl.BlockSpec((tile, d), im(lambda b, j: (b * x_per + x_off + j, 0))))
        args.append(x)
    if gather:
        in_specs.append(pl.BlockSpec(memory_space=pl.ANY))
    else:
        in_specs.append(pl.BlockSpec((tile, d), im(lambda b, j: (b * per + j, 0))))
    args.append(y)
    in_specs.append(pl.BlockSpec((1, 1, d), im(lambda b, j: (prow(b, j), 0, gate_col))))
    in_specs.append(pl.BlockSpec((1, d), im(lambda b, j: (0, 0))))
    in_specs.append(pl.BlockSpec((1, d), im(lambda b, j: (0, 0))))
    args += [ada3, lng.reshape(1, d), lnb.reshape(1, d)]
    if nxt is not None:
        ada_n, shc, scc = nxt
        in_specs.append(pl.BlockSpec((1, 1, d), im(lambda b, j: (prow(b, j), 0, shc))))
        in_specs.append(pl.BlockSpec((1, 1, d), im(lambda b, j: (prow(b, j), 0, scc))))
        args += [ada_n, ada_n]
    if router_wt is not None:
        ne = router_wt.shape[0]
        in_specs.append(pl.BlockSpec((ne, d), im(lambda b, j: (0, 0))))
        args.append(router_wt)

    out_specs = [pl.BlockSpec((tile, d), im(lambda b, j: (b * per + j, 0)))]
    out_shape = [jax.ShapeDtypeStruct((rows, d), F32)]
    if nxt is not None:
        out_specs.append(pl.BlockSpec((tile, d), im(lambda b, j: (b * per + j, 0))))
        out_shape.append(jax.ShapeDtypeStruct((rows, d), h_dtype))
    if router_wt is not None:
        out_specs.append(pl.BlockSpec((ne, tile), im(lambda b, j: (0, b * per + j))))
        out_shape.append(jax.ShapeDtypeStruct((ne, rows), F32))

    scratch = [pltpu.VMEM((2, tile, d), F32), pltpu.SemaphoreType.DMA(())] if gather else []
    kern = functools.partial(_ln_kernel, dual=dual, gather=gather, tile=tile, has_h=nxt is not None,
                             has_router=router_wt is not None, alpha=alpha)
    call = pl.pallas_call(
        kern,
        grid_spec=pltpu.PrefetchScalarGridSpec(
            num_scalar_prefetch=1 if gather else 0,
            grid=(nb, per),
            in_specs=in_specs,
            out_specs=out_specs,
            scratch_shapes=scratch,
        ),
        out_shape=out_shape,
        compiler_params=_cparams("arbitrary", "arbitrary"),
        name="ln_gather" if gather else "ln_block",
    )
    return call(*(([pos] if gather else []) + args))


def _router_kernel(b_ref, l_ref, g_ref, m_ref, *, ne):
    epg = ne // N_GROUPS
    logits = l_ref[...]
    s = [_sigmoid(logits[e:e + 1, :]) for e in range(ne)]
    sel = [s[e] + b_ref[e] for e in range(ne)]
    gscore = []
    for g in range(N_GROUPS):
        v = sel[g * epg:(g + 1) * epg]
        best = None
        for i in range(epg):
            for j in range(i + 1, epg):
                pair = v[i] + v[j]
                best = pair if best is None else jnp.maximum(best, pair)
        gscore.append(best)
    chosen = []
    for g in range(N_GROUPS):
        ok = None
        for o in range(N_GROUPS):
            if o == g:
                continue
            t = (gscore[g] > gscore[o]) if o < g else (gscore[g] >= gscore[o])
            ok = t if ok is None else jnp.logical_and(ok, t)
        chosen.append(ok)
    mask = []
    for g in range(N_GROUPS):
        v = sel[g * epg:(g + 1) * epg]
        for i in range(epg):
            rank = jnp.zeros_like(v[i])
            for j in range(epg):
                if j == i:
                    continue
                ahead = (v[j] >= v[i]) if j < i else (v[j] > v[i])
                rank = rank + jnp.where(ahead, 1.0, 0.0)
            mask.append(jnp.logical_and(chosen[g], rank < TOP_K))
    wsum = jnp.zeros_like(s[0])
    for e in range(ne):
        wsum = wsum + jnp.where(mask[e], s[e], 0.0)
    for e in range(ne):
        g_ref[e:e + 1, :] = jnp.where(mask[e], s[e] / wsum, 0.0)
        m_ref[e:e + 1, :] = jnp.where(mask[e], 1, 0).astype(jnp.int32)


def router(logits_t, router_b):
    ne, t = logits_t.shape
    tl = _pick(t, (1024, 512, 256, 128))
    spec = pl.BlockSpec((ne, tl), lambda i: (0, i))
    return pl.pallas_call(
        functools.partial(_router_kernel, ne=ne),
        grid=(t // tl,),
        in_specs=[pl.BlockSpec(memory_space=pltpu.SMEM), spec],
        out_specs=[spec, spec],
        out_shape=[jax.ShapeDtypeStruct((ne, t), F32), jax.ShapeDtypeStruct((ne, t), jnp.int32)],
        compiler_params=_cparams("parallel"),
        name="router",
    )(router_b, logits_t)


def dispatch_plan(gates, mask, tm):
    ne, t = mask.shape
    n_tiles = (TOP_K * t) // tm + ne
    rows = n_tiles * tm
    counts = jnp.sum(mask, axis=1)
    ptiles = (counts + tm - 1) // tm
    tile_end = jnp.cumsum(ptiles)
    off = (tile_end - ptiles) * tm
    rank = jnp.cumsum(mask, axis=1) - mask
    dest = off[:, None] + rank
    tok = jnp.broadcast_to(jnp.arange(t, dtype=jnp.int32)[None, :], (ne, t))
    idx = jnp.where(mask > 0, dest, rows).reshape(-1)
    row_token = jnp.zeros((rows,), jnp.int32).at[idx].set(tok.reshape(-1), mode="drop")
    row_gate = jnp.zeros((rows,), F32).at[idx].set(gates.reshape(-1), mode="drop")
    pos_lo = jnp.min(jnp.where(mask > 0, dest, rows), axis=0)
    pos_hi = jnp.max(jnp.where(mask > 0, dest, -1), axis=0)
    pos = jnp.concatenate([pos_lo, pos_hi]).astype(jnp.int32)
    n_valid = tile_end[-1]
    tid = jnp.minimum(jnp.arange(n_tiles, dtype=jnp.int32), n_valid - 1)
    tile_expert = jnp.minimum(jnp.sum(tile_end[None, :] <= tid[:, None], axis=1), ne - 1).astype(jnp.int32)
    prev = jnp.concatenate([jnp.full((1,), -1, jnp.int32), tile_expert[:-1]])
    tile_first = (tile_expert != prev).astype(jnp.int32)
    nv = jnp.reshape(n_valid, (1,)).astype(jnp.int32)
    return row_token, row_gate.reshape(rows, 1), pos, tid, tile_expert, tile_first, nv


def _gather_kernel(tok_ref, nv_ref, h_ref, o_ref, buf_ref, sem, *, tm):
    i = pl.program_id(0)

    @pl.when(i < nv_ref[0])
    def _():
        def row_copy(r):
            return pltpu.make_async_copy(h_ref.at[pl.ds(tok_ref[i * tm + r], 1)], buf_ref.at[pl.ds(r, 1)], sem)

        def start(r, carry):
            row_copy(r).start()
            return carry

        def wait(r, carry):
            row_copy(r).wait()
            return carry

        lax.fori_loop(0, tm, start, 0)
        lax.fori_loop(0, tm, wait, 0)
        o_ref[...] = buf_ref[...].astype(o_ref.dtype)

    @pl.when(i >= nv_ref[0])
    def _():
        o_ref[...] = jnp.zeros_like(o_ref)


def gather_rows(h, row_token, tid, nv, tm):
    d = h.shape[1]
    rows = row_token.shape[0]
    return pl.pallas_call(
        functools.partial(_gather_kernel, tm=tm),
        grid_spec=pltpu.PrefetchScalarGridSpec(
            num_scalar_prefetch=2,
            grid=(rows // tm,),
            in_specs=[pl.BlockSpec(memory_space=pl.ANY)],
            out_specs=pl.BlockSpec((tm, d), lambda i, tok, nvr: (i, 0)),
            scratch_shapes=[pltpu.VMEM((tm, d), F32), pltpu.SemaphoreType.DMA(())],
        ),
        out_shape=jax.ShapeDtypeStruct((rows, d), BF16),
        compiler_params=_cparams("arbitrary"),
        name="moe_gather",
    )(row_token, nv, h)


def _ffn_a_kernel(tid_ref, te_ref, tf_ref, nv_ref, x_ref, w1_ref, w3_ref, o_ref, w1b, w3b):
    i = pl.program_id(1)

    @pl.when(i < nv_ref[0])
    def _():
        @pl.when(tf_ref[i] == 1)
        def _():
            w1b[...] = w1_ref[0].astype(BF16)
            w3b[...] = w3_ref[0].astype(BF16)

        x = x_ref[...]
        a = jnp.dot(x, w1b[...], preferred_element_type=F32)
        b = jnp.dot(x, w3b[...], preferred_element_type=F32)
        o_ref[...] = (a * _sigmoid(a) * b).astype(o_ref.dtype)

    @pl.when(i >= nv_ref[0])
    def _():
        o_ref[...] = jnp.zeros_like(o_ref)


def ffn_stage_a(xs, w1, w3, layer, plan, tm):
    _, _, _, tid, te, tf, nv = plan
    rows, d = xs.shape
    ff = w1.shape[3]
    tfc = _pick(ff, (512, 256, 128))
    wspec = pl.BlockSpec((None, 1, d, tfc), lambda f, i, tid, te, tf, nv: (layer, te[i], 0, f))
    return pl.pallas_call(
        _ffn_a_kernel,
        grid_spec=pltpu.PrefetchScalarGridSpec(
            num_scalar_prefetch=4,
            grid=(ff // tfc, rows // tm),
            in_specs=[pl.BlockSpec((tm, d), lambda f, i, tid, te, tf, nv: (tid[i], 0)), wspec, wspec],
            out_specs=pl.BlockSpec((tm, tfc), lambda f, i, tid, te, tf, nv: (i, f)),
            scratch_shapes=[pltpu.VMEM((d, tfc), BF16), pltpu.VMEM((d, tfc), BF16)],
        ),
        out_shape=jax.ShapeDtypeStruct((rows, ff), BF16),
        compiler_params=_cparams("arbitrary", "arbitrary"),
        name="moe_ffn_a",
    )(tid, te, tf, nv, xs, w1, w3)


def _ffn_b_kernel(tid_ref, te_ref, tf_ref, nv_ref, h_ref, w2_ref, g_ref, o_ref, w2b):
    i = pl.program_id(1)

    @pl.when(i < nv_ref[0])
    def _():
        @pl.when(tf_ref[i] == 1)
        def _():
            w2b[...] = w2_ref[0].astype(BF16)

        o_ref[...] = g_ref[...] * jnp.dot(h_ref[...], w2b[...], preferred_element_type=F32)

    @pl.when(i >= nv_ref[0])
    def _():
        o_ref[...] = jnp.zeros_like(o_ref)


def ffn_stage_b(hs, w2, layer, plan, tm):
    _, row_gate, _, tid, te, tf, nv = plan
    rows, ff = hs.shape
    d = w2.shape[3]
    tn = _pick(d, (1024, 512, 256, 128))
    return pl.pallas_call(
        _ffn_b_kernel,
        grid_spec=pltpu.PrefetchScalarGridSpec(
            num_scalar_prefetch=4,
            grid=(d // tn, rows // tm),
            in_specs=[
                pl.BlockSpec((tm, ff), lambda n, i, tid, te, tf, nv: (tid[i], 0)),
                pl.BlockSpec((None, 1, ff, tn), lambda n, i, tid, te, tf, nv: (layer, te[i], 0, n)),
                pl.BlockSpec((tm, 1), lambda n, i, tid, te, tf, nv: (tid[i], 0)),
            ],
            out_specs=pl.BlockSpec((tm, tn), lambda n, i, tid, te, tf, nv: (i, n)),
            scratch_shapes=[pltpu.VMEM((ff, tn), BF16)],
        ),
        out_shape=jax.ShapeDtypeStruct((rows, d), F32),
        compiler_params=_cparams("arbitrary", "arbitrary"),
        name="moe_ffn_b",
    )(tid, te, tf, nv, hs, w2, row_gate)


def moe_sparse(h, logits_t, router_b, w1, w3, w2, layer):
    gates, mask = router(logits_t, router_b)
    plan = dispatch_plan(gates, mask, MOE_TILE)
    xs = gather_rows(h, plan[0], plan[3], plan[6], MOE_TILE)
    hs = ffn_stage_a(xs, w1, w3, layer, plan, MOE_TILE)
    ys = ffn_stage_b(hs, w2, layer, plan, MOE_TILE)
    return ys, plan[2]


def rope_tables(n, lc, hd):
    half = hd // 2
    t = jnp.arange(n)
    inv = jnp.power(ROPE_BASE, -jnp.arange(0, half, 2, dtype=F32) / half)
    ang_r = (t // GRID_W).astype(F32)[:, None] * inv
    ang_c = (t % GRID_W).astype(F32)[:, None] * inv
    cos = jnp.concatenate([jnp.cos(ang_r)] * 2 + [jnp.cos(ang_c)] * 2, axis=1)
    sin = jnp.concatenate([-jnp.sin(ang_r), jnp.sin(ang_r), -jnp.sin(ang_c), jnp.sin(ang_c)], axis=1)
    cos = jnp.concatenate([jnp.ones((lc, hd), F32), cos], axis=0)
    sin = jnp.concatenate([jnp.zeros((lc, hd), F32), sin], axis=0)
    scale = hd ** -0.5
    ones, zeros = jnp.ones_like(cos), jnp.zeros_like(sin)
    return jnp.stack([cos * scale, cos, ones]), jnp.stack([sin * scale, sin, zeros])


def kernel(x, c, ctx, c_ctx, ada_w, ada_b, ln_g, ln_b, mlstm_w_in, mlstm_b_gates, mlstm_norm_w, mlstm_w_out,
           attn_w_in, attn_b_in, attn_sink, attn_w_out, router_w, router_b, moe_w1, moe_w3, moe_w2):
    bsz, n, d = x.shape
    lc = ctx.shape[1]
    depth = ada_w.shape[0]
    assert depth == 2 and bsz <= 4 and n % lc == 0 and lc % ML_CHUNK == 0
    alpha = (2 * depth) ** 0.25
    tile = lc
    per = (n + lc) // tile
    per_lat = n // tile
    t_all = bsz * (n + lc)

    cc = jnp.concatenate([c, c_ctx[None, :], jnp.zeros((8 - bsz - 1, d), F32)], axis=0)
    ada = adaln(cc, ada_w, ada_b)
    ada0 = ada[0].reshape(8, 1, 6 * d)
    ada1 = ada[1].reshape(8, 1, 6 * d)
    rwt = router_w.T

    heads = mlstm_b_gates.shape[1] // 4
    dv = d // heads
    dqk = dv // 2
    nmain = 2 * heads * (dqk + dv)
    w_in = mlstm_w_in[0]
    h0, gates_pre = mod0(x, ctx, ada0, w_in[:, nmain:], mlstm_b_gates[0][None, :])
    qkvo = matmul(h0, w_in, n_cols=nmain)
    g4 = gates_pre.reshape(bsz, n + lc, 4, heads)
    g_rows = jnp.transpose(g4, (0, 2, 3, 1)).reshape(bsz, 4, heads, 1, n + lc)
    g_cols = jnp.transpose(g4, (0, 3, 1, 2))
    ncc, nct = lc // ML_CHUNK, (n + lc) // ML_CHUNK
    hf = mlstm_scan(qkvo, g_rows, g_cols, bsz, heads, dqk, dv, ncc, nct)
    hg = mlstm_scan(qkvo, g_rows, g_cols, bsz, heads, dqk, dv, ncc, nct, hf=hf, norm_w=mlstm_norm_w[0])
    y = matmul(hg, mlstm_w_out[0], out_dtype=F32)
    x1, hffn, lg = ln_block(x, y, ada0, 2, ln_g[0, 0], ln_b[0, 0], alpha=alpha, nb=bsz, per=per, tile=tile,
                            ctx_rows=True, ctx=ctx, nxt=(ada0, 3, 4), h_dtype=F32, router_wt=rwt)
    ys, pos = moe_sparse(hffn, lg, router_b, moe_w1, moe_w3, moe_w2, 0)
    x2, h1 = ln_block(x1, ys, ada0, 5, ln_g[0, 1], ln_b[0, 1], alpha=alpha, nb=bsz, per=per, tile=tile,
                      ctx_rows=True, x_per=per, nxt=(ada1, 0, 1), h_dtype=BF16, pos=pos)

    heads_a = attn_sink.shape[1]
    hd = d // heads_a
    kvh = (attn_w_in.shape[2] // hd - heads_a) // 2
    cos_t, sin_t = rope_tables(n, lc, hd)
    q_end, k_end = heads_a * hd, (heads_a + kvh) * hd
    type_of = lambda col: jnp.where(col < q_end, 0, jnp.where(col < k_end, 1, 2))
    tm1 = _pick(n + lc, (1152, 768, 384, 256, 128))
    qkv = matmul(h1, attn_w_in[0], bias=attn_b_in[0], rope=(cos_t, sin_t, n + lc, type_of), tm=tm1,
                 tn=_pick(kvh * hd, (512, 256, 128)))
    att = window_attention(qkv.reshape(bsz, n + lc, -1), attn_sink[0], lc, n, heads_a, kvh, hd)
    y = matmul(att.reshape(bsz * n, heads_a * hd), attn_w_out[0], out_dtype=F32)
    x3, hffn, lg = ln_block(x2, y, ada1, 2, ln_g[1, 0], ln_b[1, 0], alpha=alpha, nb=bsz, per=per_lat, tile=tile,
                            ctx_rows=False, x_per=per, x_off=1, nxt=(ada1, 3, 4), h_dtype=F32, router_wt=rwt)
    ys, pos = moe_sparse(hffn, lg, router_b, moe_w1, moe_w3, moe_w2, 1)
    (x4,) = ln_block(x3, ys, ada1, 5, ln_g[1, 1], ln_b[1, 1], alpha=alpha, nb=bsz, per=per_lat, tile=tile,
                     ctx_rows=False, x_per=per_lat, pos=pos)
    return x4.reshape(bsz, n, d)
```

```python
import functools

import jax
import jax.numpy as jnp
from jax import lax
from jax.experimental import pallas as pl
from jax.experimental.pallas import tpu as pltpu

F32 = jnp.float32
BF16 = jnp.bfloat16
NEG_INF = float("-inf")

GRID_W = 64
ML_CHUNK = 128
AT_WINDOW = 128
AT_BLOCK = 128
ROPE_BASE = 10000.0
N_GROUPS = 4
TOP_K = 2
LN_EPS = 1e-5
RMS_EPS = 1e-6

LANES = 128
VMEM_LIMIT = 56 * 1024 * 1024
MOE_TILE = 512
HIGHEST = lax.Precision.HIGHEST
NT_DIMS = (((1,), (1,)), ((), ()))


def _cparams(*sem):
    return pltpu.CompilerParams(dimension_semantics=sem, vmem_limit_bytes=VMEM_LIMIT)


def _pick(n, candidates):
    for c in candidates:
        if n % c == 0:
            return c
    return n


def _sigmoid(x):
    return 1.0 / (1.0 + jnp.exp(-x))


def _log_sigmoid(x):
    return jnp.minimum(x, 0.0) - jnp.log1p(jnp.exp(-jnp.abs(x)))


def _adaln_kernel(c_ref, w_ref, b_ref, o_ref):
    c = c_ref[...]
    a = (c * _sigmoid(c)).astype(BF16)
    o_ref[0] = jnp.dot(a, w_ref[0].astype(BF16), preferred_element_type=F32) + b_ref[0]


def adaln(cc, ada_w, ada_b):
    depth, d, n6 = ada_w.shape
    tn = _pick(n6, (512, 256, 128))
    return pl.pallas_call(
        _adaln_kernel,
        grid=(depth, n6 // tn),
        in_specs=[
            pl.BlockSpec((8, d), lambda l, j: (0, 0)),
            pl.BlockSpec((1, d, tn), lambda l, j: (l, 0, j)),
            pl.BlockSpec((1, 1, tn), lambda l, j: (l, 0, j)),
        ],
        out_specs=pl.BlockSpec((1, 8, tn), lambda l, j: (l, 0, j)),
        out_shape=jax.ShapeDtypeStruct((depth, 8, n6), F32),
        compiler_params=_cparams("parallel", "parallel"),
        name="adaln",
    )(cc, ada_w, ada_b.reshape(depth, 1, n6))


def _mod0_kernel(x_ref, c_ref, sh_ref, sc_ref, wg_ref, bg_ref, h_ref, g_ref):
    j = pl.program_id(1)
    t = jnp.where(j == 0, c_ref[0], x_ref[0])
    h = t * (1.0 + sc_ref[0]) + sh_ref[0]
    h_ref[...] = h.astype(BF16)
    g_ref[...] = jnp.dot(h, wg_ref[...], preferred_element_type=F32, precision=HIGHEST) + bg_ref[...]


def mod0(x, ctx, ada3, wg, bg):
    bsz, n, d = x.shape
    lc = ctx.shape[1]
    per = (n + lc) // lc
    ng = wg.shape[1]
    prow = lambda b, j: jnp.where(j == 0, bsz, b)
    return pl.pallas_call(
        _mod0_kernel,
        grid=(bsz, per),
        in_specs=[
            pl.BlockSpec((1, lc, d), lambda b, j: (b, jnp.maximum(j - 1, 0), 0)),
            pl.BlockSpec((1, lc, d), lambda b, j: (b, 0, 0)),
            pl.BlockSpec((1, 1, d), lambda b, j: (prow(b, j), 0, 0)),
            pl.BlockSpec((1, 1, d), lambda b, j: (prow(b, j), 0, 1)),
            pl.BlockSpec((d, ng), lambda b, j: (0, 0)),
            pl.BlockSpec((1, ng), lambda b, j: (0, 0)),
        ],
        out_specs=[
            pl.BlockSpec((lc, d), lambda b, j: (b * per + j, 0)),
            pl.BlockSpec((lc, ng), lambda b, j: (b * per + j, 0)),
        ],
        out_shape=[
            jax.ShapeDtypeStruct((bsz * (n + lc), d), BF16),
            jax.ShapeDtypeStruct((bsz * (n + lc), ng), F32),
        ],
        compiler_params=_cparams("parallel", "arbitrary"),
        name="mod0",
    )(x, ctx, ada3, ada3, wg, bg)


def _mm_kernel(*refs, has_bias, has_rope, tn):
    x_ref, w_ref = refs[0], refs[1]
    pos = 2
    acc = jnp.dot(x_ref[...], w_ref[...].astype(BF16), preferred_element_type=F32)
    if has_bias:
        acc = acc + refs[pos][...]
        pos += 1
    o_ref = refs[-1]
    if has_rope:
        ca = refs[pos][0]
        sb = refs[pos + 1][0]
        lane = lax.broadcasted_iota(jnp.int32, ca.shape, 1)
        first = (lane % 64) < 32
        for h in range(tn // LANES):
            y = acc[:, h * LANES:(h + 1) * LANES]
            sw = jnp.where(first, pltpu.roll(y, 96, 1), pltpu.roll(y, 32, 1))
            o_ref[:, h * LANES:(h + 1) * LANES] = (y * ca + sw * sb).astype(o_ref.dtype)
    else:
        o_ref[...] = acc.astype(o_ref.dtype)


def matmul(x, w, bias=None, rope=None, out_dtype=BF16, tm=None, tn=None, n_cols=None):
    m, k = x.shape
    n = n_cols or w.shape[2]
    tm = tm or _pick(m, (1152, 1024, 768, 512, 256, 128))
    tn = tn or _pick(n, (512, 256, 128))
    in_specs = [
        pl.BlockSpec((tm, k), lambda i, j: (i, 0)),
        pl.BlockSpec((None, k, tn), lambda i, j: (0, 0, j)),
    ]
    args = [x, w]
    if bias is not None:
        in_specs.append(pl.BlockSpec((1, tn), lambda i, j: (0, j)))
        args.append(bias.reshape(1, n))
    if rope is not None:
        ca, sb, period, type_of = rope
        nper = period // tm
        spec = pl.BlockSpec((1, tm, LANES), lambda i, j: (type_of(j * tn), i % nper, 0))
        in_specs += [spec, spec]
        args += [ca, sb]
    return pl.pallas_call(
        functools.partial(_mm_kernel, has_bias=bias is not None, has_rope=rope is not None, tn=tn),
        grid=(m // tm, n // tn),
        in_specs=in_specs,
        out_specs=pl.BlockSpec((tm, tn), lambda i, j: (i, j)),
        out_shape=jax.ShapeDtypeStruct((m, n), out_dtype),
        compiler_params=_cparams("parallel", "arbitrary"),
        name="matmul",
    )(*args)


def _mlstm_chunk(q, k, v, ig_row, f_row, ig_col, f_col, st_ref, n_ref, m_ref, d, scale):
    L = ML_CHUNK
    reverse = d == 1
    lf_row = _log_sigmoid(f_row)
    lf_col = _log_sigmoid(f_col)
    r = lax.broadcasted_iota(jnp.int32, (L, L), 0)
    c = lax.broadcasted_iota(jnp.int32, (L, L), 1)
    keep = (c >= r) if reverse else (c <= r)
    keep_t = (r >= c) if reverse else (r <= c)
    b_col = jnp.sum(jnp.where(keep, lf_row, 0.0), axis=1, keepdims=True)
    b_row = jnp.sum(jnp.where(keep_t, lf_col, 0.0), axis=0, keepdims=True)

    m_prev = m_ref[d]
    logw = jnp.where(keep, b_col - b_row + ig_row, NEG_INF)
    inter = b_col + m_prev
    m_t = jnp.maximum(inter, jnp.max(logw, axis=1, keepdims=True))
    w_inter = jnp.exp(inter - m_t)
    p = jnp.exp(logw - m_t)

    q = (q.astype(F32) * scale).astype(BF16)
    st = st_ref[d]
    n_row = n_ref[d]
    s = lax.dot_general(q, k, NT_DIMS, preferred_element_type=F32) * p
    num = (w_inter * jnp.dot(q, st.astype(BF16), preferred_element_type=F32)
           + jnp.dot(s.astype(BF16), v, preferred_element_type=F32))
    qn = jnp.sum(q.astype(F32) * n_row, axis=1, keepdims=True)
    den = w_inter * qn + jnp.sum(s, axis=1, keepdims=True)
    h = num / jnp.maximum(jnp.abs(den), jnp.exp(-m_t))

    b_end = b_col[0:1, :] if reverse else b_col[L - 1:L, :]
    m_new = jnp.maximum(b_end + m_prev, jnp.max(b_end - b_row + ig_row, axis=1, keepdims=True))
    decay = jnp.exp(b_end + m_prev - m_new)
    w_src = jnp.exp(b_end - b_col + ig_col - m_new)
    kf = k.astype(F32)
    vw = (v.astype(F32) * w_src).astype(BF16)
    st_ref[d] = decay * st + jnp.dot(kf.T.astype(BF16), vw, preferred_element_type=F32)
    n_ref[d] = decay * n_row + jnp.sum(kf * w_src, axis=0, keepdims=True)
    m_ref[d] = m_new
    return h


def _mlstm_kernel(q_ref, k_ref, v_ref, o_ref, gr_ref, gc_ref, nw_ref, out_ref, acc_ref, st_ref, n_ref, m_ref,
                  *, scale, ncc, nct):
    L = ML_CHUNK
    st_ref[...] = jnp.zeros_like(st_ref)
    n_ref[...] = jnp.zeros_like(n_ref)
    m_ref[...] = jnp.zeros_like(m_ref)
    acc_ref[...] = jnp.zeros_like(acc_ref)

    def step(j, carry):
        for d, cj in ((0, j), (1, jnp.where(j < ncc, ncc - 1 - j, nct - 1 - (j - ncc)))):
            rows = pl.ds(pl.multiple_of(cj * L, L), L)
            gcol = gc_ref[0, 0, cj]
            h = _mlstm_chunk(q_ref[rows, :], k_ref[rows, :], v_ref[rows, :],
                             gr_ref[0, 2 * d, 0, cj], gr_ref[0, 2 * d + 1, 0, cj],
                             gcol[:, 2 * d:2 * d + 1], gcol[:, 2 * d + 1:2 * d + 2],
                             st_ref, n_ref, m_ref, d, scale)
            acc_ref[rows, :] += h
        return carry

    lax.fori_loop(0, nct, step, 0)

    def finish(cj, carry):
        rows = pl.ds(pl.multiple_of(cj * L, L), L)
        hs = acc_ref[rows, :]
        hn = hs * lax.rsqrt(jnp.mean(hs * hs, axis=1, keepdims=True) + RMS_EPS)
        out_ref[rows, :] = (hn * nw_ref[...] * _sigmoid(o_ref[rows, :].astype(F32))).astype(out_ref.dtype)
        return carry

    lax.fori_loop(0, nct, finish, 0)


def mlstm_bidir(qkvo, gates_pre, norm_w, bsz, heads, dqk, dv, ncc, nct):
    t = qkvo.shape[0]
    L = ML_CHUNK
    n2 = nct * L
    g4 = gates_pre.reshape(bsz, nct, L, 4, heads)
    g_rows = jnp.transpose(g4, (0, 3, 4, 1, 2)).reshape(bsz, 4, heads, nct, 1, L)
    g_cols = jnp.transpose(g4, (0, 4, 1, 2, 3))
    koff = heads
    voff = 2 * heads * dqk // dv
    ooff = voff + heads
    return pl.pallas_call(
        functools.partial(_mlstm_kernel, scale=dqk ** -0.5, ncc=ncc, nct=nct),
        grid=(bsz, heads),
        in_specs=[
            pl.BlockSpec((n2, dqk), lambda b, h: (b, h)),
            pl.BlockSpec((n2, dqk), lambda b, h: (b, koff + h)),
            pl.BlockSpec((n2, dv), lambda b, h: (b, voff + h)),
            pl.BlockSpec((n2, dv), lambda b, h: (b, ooff + h)),
            pl.BlockSpec((1, 4, 1, nct, 1, L), lambda b, h: (b, 0, h, 0, 0, 0)),
            pl.BlockSpec((1, 1, nct, L, 4), lambda b, h: (b, h, 0, 0, 0)),
            pl.BlockSpec((1, dv), lambda b, h: (0, h)),
        ],
        out_specs=pl.BlockSpec((n2, dv), lambda b, h: (b, h)),
        out_shape=jax.ShapeDtypeStruct((t, heads * dv), BF16),
        scratch_shapes=[pltpu.VMEM((n2, dv), F32), pltpu.VMEM((2, dqk, dv), F32),
                        pltpu.VMEM((2, 1, dqk), F32), pltpu.VMEM((2, 1, 1), F32)],
        compiler_params=_cparams("parallel", "parallel"),
        name="mlstm",
    )(qkvo, qkvo, qkvo, qkvo, g_rows, g_cols, norm_w.reshape(1, heads * dv))


def _attn_kernel(sink_ref, q_ref, k_ref, v_ref, o_ref, *, lc, nb, group, hd):
    kv = pl.program_id(1)
    blk = AT_BLOCK
    rows = group * blk
    kc = k_ref[0, 0:lc, :]
    vc = v_ref[0, 0:lc, :]
    rid = lax.broadcasted_iota(jnp.int32, (rows, 1), 0)
    sink_col = jnp.zeros((rows, 1), F32)
    for g in range(group):
        sink_col = jnp.where(rid // blk == g, sink_ref[kv * group + g], sink_col)
    qrow = lax.broadcasted_iota(jnp.int32, (rows, 3 * blk), 0) % blk
    kcol = lax.broadcasted_iota(jnp.int32, (rows, 3 * blk), 1)

    def body(j, carry):
        ws = jnp.clip(j - 1, 0, nb - 3)
        qs = pl.multiple_of(lc + j * blk, blk)
        ks = pl.multiple_of(lc + ws * blk, blk)
        q4 = q_ref[0, pl.ds(qs, blk), :]
        qq = jnp.concatenate([q4[:, g * hd:(g + 1) * hd] for g in range(group)], axis=0)
        kw = k_ref[0, pl.ds(ks, 3 * blk), :]
        vw = v_ref[0, pl.ds(ks, 3 * blk), :]
        s_loc = lax.dot_general(qq, kw, NT_DIMS, preferred_element_type=F32)
        s_ctx = lax.dot_general(qq, kc, NT_DIMS, preferred_element_type=F32)
        rel = (ws - j) * blk + kcol - qrow
        s_loc = jnp.where(jnp.abs(rel) <= AT_WINDOW, s_loc, NEG_INF)
        m = jnp.maximum(jnp.maximum(jnp.max(s_loc, axis=1, keepdims=True),
                                    jnp.max(s_ctx, axis=1, keepdims=True)), sink_col)
        p_loc = jnp.exp(s_loc - m)
        p_ctx = jnp.exp(s_ctx - m)
        den = (jnp.sum(p_loc, axis=1, keepdims=True) + jnp.sum(p_ctx, axis=1, keepdims=True)
               + jnp.exp(sink_col - m))
        o = (jnp.dot(p_loc.astype(BF16), vw, preferred_element_type=F32)
             + jnp.dot(p_ctx.astype(BF16), vc, preferred_element_type=F32)) / den
        os = pl.multiple_of(j * blk, blk)
        for g in range(group):
            o_ref[0, pl.ds(os, blk), g * hd:(g + 1) * hd] = o[g * blk:(g + 1) * blk].astype(o_ref.dtype)
        return carry

    lax.fori_loop(0, nb, body, 0)


def window_attention(qkv3, sink, lc, n, heads, kv_heads, hd):
    bsz = qkv3.shape[0]
    group = heads // kv_heads
    nb = n // AT_BLOCK
    assert nb >= 3
    koff = heads
    voff = heads + kv_heads
    return pl.pallas_call(
        functools.partial(_attn_kernel, lc=lc, nb=nb, group=group, hd=hd),
        grid=(bsz, kv_heads),
        in_specs=[
            pl.BlockSpec(memory_space=pltpu.SMEM),
            pl.BlockSpec((1, lc + n, group * hd), lambda b, g: (b, 0, g)),
            pl.BlockSpec((1, lc + n, hd), lambda b, g: (b, 0, koff + g)),
            pl.BlockSpec((1, lc + n, hd), lambda b, g: (b, 0, voff + g)),
        ],
        out_specs=pl.BlockSpec((1, n, group * hd), lambda b, g: (b, 0, g)),
        out_shape=jax.ShapeDtypeStruct((bsz, n, heads * hd), BF16),
        compiler_params=_cparams("parallel", "parallel"),
        name="window_attention",
    )(sink, qkv3, qkv3, qkv3)


def _ln_kernel(*refs, dual, gather, tile, has_h, has_router, alpha):
    refs = list(refs)
    if gather:
        pos_ref = refs.pop(0)
    x_ref = refs.pop(0)
    c_ref = refs.pop(0) if dual else None
    y_ref = refs.pop(0)
    gw_ref = refs.pop(0) if gather else None
    gate_ref, lng_ref, lnb_ref = refs.pop(0), refs.pop(0), refs.pop(0)
    if has_h:
        sh_ref, sc_ref = refs.pop(0), refs.pop(0)
    if has_router:
        rw_ref = refs.pop(0)
    xo_ref = refs.pop(0)
    if has_h:
        ho_ref = refs.pop(0)
    if has_router:
        lg_ref = refs.pop(0)

    if gather:
        buf_ref, sems = refs.pop(0), refs.pop(0)
        lin = pl.program_id(0) * pl.num_programs(1) + pl.program_id(1)
        ntiles = pl.num_programs(0) * pl.num_programs(1)
        ntok = pos_ref.shape[0] // 2

        def issue(t):
            slot = t % 2

            def start(r, carry):
                for k in range(2):
                    src = pos_ref[k * ntok + t * tile + r]
                    pltpu.make_async_copy(y_ref.at[pl.ds(src, 1)], buf_ref.at[slot, k, pl.ds(r, 1)],
                                          sems.at[slot]).start()
                return carry

            lax.fori_loop(0, tile, start, 0, unroll=4)

        @pl.when(lin == 0)
        def _():
            issue(lin)

        @pl.when(lin + 1 < ntiles)
        def _():
            issue(lin + 1)

        slot = lin % 2
        for k in range(2):
            pltpu.make_async_copy(y_ref.at[pl.ds(0, tile)], buf_ref.at[slot, k], sems.at[slot]).wait()
        gw = gw_ref[...]
        y = gw[:, 0:1] * buf_ref[slot, 0] + gw[:, 1:2] * buf_ref[slot, 1]
    else:
        y = y_ref[...].astype(F32)

    if dual:
        x = jnp.where(pl.program_id(1) == 0, c_ref[0], x_ref[0])
    else:
        x = x_ref[...]
    z = alpha * x + gate_ref[0] * y
    mu = jnp.mean(z, axis=1, keepdims=True)
    zc = z - mu
    var = jnp.mean(zc * zc, axis=1, keepdims=True)
    xn = zc * lax.rsqrt(var + LN_EPS) * lng_ref[...] + lnb_ref[...]
    xo_ref[...] = xn
    if has_h:
        h = xn * (1.0 + sc_ref[0]) + sh_ref[0]
        ho_ref[...] = h.astype(ho_ref.dtype)
        if has_router:
            lg_ref[...] = lax.dot_general(rw_ref[...], h, NT_DIMS, preferred_element_type=F32,
                                          precision=HIGHEST)


def ln_block(x, y, ada3, gate_col, lng, lnb, *, alpha, nb, per, tile, ctx_rows, ctx=None, x_per=None,
             x_off=0, nxt=None, h_dtype=None, router_wt=None, pos=None, pos_w=None):
    d = x.shape[-1]
    dual = ctx is not None
    gather = pos is not None
    rows = nb * per * tile
    prow = (lambda b, j: jnp.where(j == 0, nb, b)) if ctx_rows else (lambda b, j: b)

    def im(f):
        return lambda b, j, *_: f(b, j)

    in_specs, args = [], []
    if dual:
        in_specs.append(pl.BlockSpec((1, tile, d), im(lambda b, j: (b, jnp.maximum(j - 1, 0), 0))))
        in_specs.append(pl.BlockSpec((1, tile, d), im(lambda b, j: (b, 0, 0))))
        args += [x, ctx]
    else:
        in_specs.append(pl.BlockSpec((tile, d), im(lambda b, j: (b * x_per + x_off + j, 0))))
        args.append(x)
    if gather:
        in_specs.append(pl.BlockSpec(memory_space=pl.ANY))
        in_specs.append(pl.BlockSpec((tile, 2), im(lambda b, j: (b * per + j, 0))))
        args += [y, pos_w]
    else:
        in_specs.append(pl.BlockSpec((tile, d), im(lambda b, j: (b * per + j, 0))))
        args.append(y)
    in_specs.append(pl.BlockSpec((1, 1, d), im(lambda b, j: (prow(b, j), 0, gate_col))))
    in_specs.append(pl.BlockSpec((1, d), im(lambda b, j: (0, 0))))
    in_specs.append(pl.BlockSpec((1, d), im(lambda b, j: (0, 0))))
    args += [ada3, lng.reshape(1, d), lnb.reshape(1, d)]
    if nxt is not None:
        ada_n, shc, scc = nxt
        in_specs.append(pl.BlockSpec((1, 1, d), im(lambda b, j: (prow(b, j), 0, shc))))
        in_specs.append(pl.BlockSpec((1, 1, d), im(lambda b, j: (prow(b, j), 0, scc))))
        args += [ada_n, ada_n]
    if router_wt is not None:
        ne = router_wt.shape[0]
        in_specs.append(pl.BlockSpec((ne, d), im(lambda b, j: (0, 0))))
        args.append(router_wt)

    out_specs = [pl.BlockSpec((tile, d), im(lambda b, j: (b * per + j, 0)))]
    out_shape = [jax.ShapeDtypeStruct((rows, d), F32)]
    if nxt is not None:
        out_specs.append(pl.BlockSpec((tile, d), im(lambda b, j: (b * per + j, 0))))
        out_shape.append(jax.ShapeDtypeStruct((rows, d), h_dtype))
    if router_wt is not None:
        out_specs.append(pl.BlockSpec((ne, tile), im(lambda b, j: (0, b * per + j))))
        out_shape.append(jax.ShapeDtypeStruct((ne, rows), F32))

    scratch = [pltpu.VMEM((2, 2, tile, d), F32), pltpu.SemaphoreType.DMA((2,))] if gather else []
    kern = functools.partial(_ln_kernel, dual=dual, gather=gather, tile=tile, has_h=nxt is not None,
                             has_router=router_wt is not None, alpha=alpha)
    call = pl.pallas_call(
        kern,
        grid_spec=pltpu.PrefetchScalarGridSpec(
            num_scalar_prefetch=1 if gather else 0,
            grid=(nb, per),
            in_specs=in_specs,
            out_specs=out_specs,
            scratch_shapes=scratch,
        ),
        out_shape=out_shape,
        compiler_params=_cparams("arbitrary", "arbitrary"),
        name="ln_gather" if gather else "ln_block",
    )
    return call(*(([pos] if gather else []) + args))


def _router_kernel(b_ref, l_ref, g_ref, m_ref, *, ne):
    epg = ne // N_GROUPS
    logits = l_ref[...]
    s = [_sigmoid(logits[e:e + 1, :]) for e in range(ne)]
    sel = [s[e] + b_ref[e] for e in range(ne)]
    gscore = []
    for g in range(N_GROUPS):
        v = sel[g * epg:(g + 1) * epg]
        best = None
        for i in range(epg):
            for j in range(i + 1, epg):
                pair = v[i] + v[j]
                best = pair if best is None else jnp.maximum(best, pair)
        gscore.append(best)
    chosen = []
    for g in range(N_GROUPS):
        ok = None
        for o in range(N_GROUPS):
            if o == g:
                continue
            t = (gscore[g] > gscore[o]) if o < g else (gscore[g] >= gscore[o])
            ok = t if ok is None else jnp.logical_and(ok, t)
        chosen.append(ok)
    mask = []
    for g in range(N_GROUPS):
        v = sel[g * epg:(g + 1) * epg]
        for i in range(epg):
            rank = jnp.zeros_like(v[i])
            for j in range(epg):
                if j == i:
                    continue
                ahead = (v[j] >= v[i]) if j < i else (v[j] > v[i])
                rank = rank + jnp.where(ahead, 1.0, 0.0)
            mask.append(jnp.logical_and(chosen[g], rank < TOP_K))
    wsum = jnp.zeros_like(s[0])
    for e in range(ne):
        wsum = wsum + jnp.where(mask[e], s[e], 0.0)
    for e in range(ne):
        g_ref[e:e + 1, :] = jnp.where(mask[e], s[e] / wsum, 0.0)
        m_ref[e:e + 1, :] = jnp.where(mask[e], 1, 0).astype(jnp.int32)


def router(logits_t, router_b):
    ne, t = logits_t.shape
    tl = _pick(t, (1024, 512, 256, 128))
    spec = pl.BlockSpec((ne, tl), lambda i: (0, i))
    return pl.pallas_call(
        functools.partial(_router_kernel, ne=ne),
        grid=(t // tl,),
        in_specs=[pl.BlockSpec(memory_space=pltpu.SMEM), spec],
        out_specs=[spec, spec],
        out_shape=[jax.ShapeDtypeStruct((ne, t), F32), jax.ShapeDtypeStruct((ne, t), jnp.int32)],
        compiler_params=_cparams("parallel"),
        name="router",
    )(router_b, logits_t)


def dispatch_plan(gates, mask, tm):
    ne, t = mask.shape
    n_tiles = (TOP_K * t) // tm + ne
    rows = n_tiles * tm
    sel = mask > 0
    counts = jnp.sum(mask, axis=1)
    ptiles = (counts + tm - 1) // tm
    tile_end = jnp.cumsum(ptiles)
    off = (tile_end - ptiles) * tm
    dest = off[:, None] + jnp.cumsum(mask, axis=1) - mask
    pos_lo = jnp.min(jnp.where(sel, dest, rows), axis=0)
    pos_hi = jnp.max(jnp.where(sel, dest, -1), axis=0)
    g_lo = jnp.sum(jnp.where(sel & (dest == pos_lo[None, :]), gates, 0.0), axis=0)
    g_hi = jnp.sum(jnp.where(sel & (dest == pos_hi[None, :]), gates, 0.0), axis=0)
    pos = jnp.concatenate([pos_lo, pos_hi]).astype(jnp.int32)
    tok = jnp.arange(t, dtype=jnp.int32)
    row_token = jnp.zeros((rows,), jnp.int32).at[pos].set(jnp.concatenate([tok, tok]), unique_indices=True)
    n_valid = tile_end[-1]
    tid = jnp.minimum(jnp.arange(n_tiles, dtype=jnp.int32), n_valid - 1)
    tile_expert = jnp.minimum(jnp.sum(tile_end[None, :] <= tid[:, None], axis=1), ne - 1).astype(jnp.int32)
    prev = jnp.concatenate([jnp.full((1,), -1, jnp.int32), tile_expert[:-1]])
    tile_first = (tile_expert != prev).astype(jnp.int32)
    nv = jnp.reshape(n_valid, (1,)).astype(jnp.int32)
    return row_token, jnp.stack([g_lo, g_hi], axis=1), pos, tid, tile_expert, tile_first, nv


def _gather_kernel(tok_ref, nv_ref, h_ref, o_ref, buf_ref, sems, *, tm):
    i = pl.program_id(0)
    nv = nv_ref[0]

    def issue(tile):
        slot = tile % 2

        def start(r, carry):
            pltpu.make_async_copy(h_ref.at[pl.ds(tok_ref[tile * tm + r], 1)], buf_ref.at[slot, pl.ds(r, 1)],
                                  sems.at[slot]).start()
            return carry

        lax.fori_loop(0, tm, start, 0, unroll=8)

    @pl.when(i == 0)
    def _():
        issue(i)

    @pl.when(i + 1 < nv)
    def _():
        issue(i + 1)

    @pl.when(i < nv)
    def _():
        slot = i % 2
        pltpu.make_async_copy(h_ref.at[pl.ds(0, tm)], buf_ref.at[slot], sems.at[slot]).wait()
        o_ref[...] = buf_ref[slot].astype(o_ref.dtype)

    @pl.when(i >= nv)
    def _():
        o_ref[...] = jnp.zeros_like(o_ref)


def gather_rows(h, row_token, nv, tm):
    d = h.shape[1]
    rows = row_token.shape[0]
    return pl.pallas_call(
        functools.partial(_gather_kernel, tm=tm),
        grid_spec=pltpu.PrefetchScalarGridSpec(
            num_scalar_prefetch=2,
            grid=(rows // tm,),
            in_specs=[pl.BlockSpec(memory_space=pl.ANY)],
            out_specs=pl.BlockSpec((tm, d), lambda i, tok, nvr: (i, 0)),
            scratch_shapes=[pltpu.VMEM((2, tm, d), F32), pltpu.SemaphoreType.DMA((2,))],
        ),
        out_shape=jax.ShapeDtypeStruct((rows, d), BF16),
        compiler_params=_cparams("arbitrary"),
        name="moe_gather",
    )(row_token, nv, h)


def _ffn_a_kernel(tid_ref, te_ref, tf_ref, nv_ref, x_ref, w1_ref, w3_ref, o_ref, w1b, w3b):
    i = pl.program_id(1)

    @pl.when(i < nv_ref[0])
    def _():
        @pl.when(tf_ref[i] == 1)
        def _():
            w1b[...] = w1_ref[0].astype(BF16)
            w3b[...] = w3_ref[0].astype(BF16)

        x = x_ref[...]
        a = jnp.dot(x, w1b[...], preferred_element_type=F32)
        b = jnp.dot(x, w3b[...], preferred_element_type=F32)
        o_ref[...] = (a * _sigmoid(a) * b).astype(o_ref.dtype)

    @pl.when(i >= nv_ref[0])
    def _():
        o_ref[...] = jnp.zeros_like(o_ref)


def ffn_stage_a(xs, w1, w3, layer, plan, tm):
    _, _, _, tid, te, tf, nv = plan
    rows, d = xs.shape
    ff = w1.shape[3]
    tfc = _pick(ff, (512, 256, 128))
    wspec = pl.BlockSpec((None, 1, d, tfc), lambda f, i, tid, te, tf, nv: (layer, te[i], 0, f))
    return pl.pallas_call(
        _ffn_a_kernel,
        grid_spec=pltpu.PrefetchScalarGridSpec(
            num_scalar_prefetch=4,
            grid=(ff // tfc, rows // tm),
            in_specs=[pl.BlockSpec((tm, d), lambda f, i, tid, te, tf, nv: (tid[i], 0)), wspec, wspec],
            out_specs=pl.BlockSpec((tm, tfc), lambda f, i, tid, te, tf, nv: (i, f)),
            scratch_shapes=[pltpu.VMEM((d, tfc), BF16), pltpu.VMEM((d, tfc), BF16)],
        ),
        out_shape=jax.ShapeDtypeStruct((rows, ff), BF16),
        compiler_params=_cparams("arbitrary", "arbitrary"),
        name="moe_ffn_a",
    )(tid, te, tf, nv, xs, w1, w3)


def _ffn_b_kernel(tid_ref, te_ref, tf_ref, nv_ref, h_ref, w2_ref, o_ref, w2b):
    i = pl.program_id(1)

    @pl.when(i < nv_ref[0])
    def _():
        @pl.when(tf_ref[i] == 1)
        def _():
            w2b[...] = w2_ref[0].astype(BF16)

        o_ref[...] = jnp.dot(h_ref[...], w2b[...], preferred_element_type=F32)

    @pl.when(i >= nv_ref[0])
    def _():
        o_ref[...] = jnp.zeros_like(o_ref)


def ffn_stage_b(hs, w2, layer, plan, tm):
    _, _, _, tid, te, tf, nv = plan
    rows, ff = hs.shape
    d = w2.shape[3]
    tn = _pick(d, (2048, 1024, 512, 256, 128))
    return pl.pallas_call(
        _ffn_b_kernel,
        grid_spec=pltpu.PrefetchScalarGridSpec(
            num_scalar_prefetch=4,
            grid=(d // tn, rows // tm),
            in_specs=[
                pl.BlockSpec((tm, ff), lambda n, i, tid, te, tf, nv: (tid[i], 0)),
                pl.BlockSpec((None, 1, ff, tn), lambda n, i, tid, te, tf, nv: (layer, te[i], 0, n)),
            ],
            out_specs=pl.BlockSpec((tm, tn), lambda n, i, tid, te, tf, nv: (i, n)),
            scratch_shapes=[pltpu.VMEM((ff, tn), BF16)],
        ),
        out_shape=jax.ShapeDtypeStruct((rows, d), F32),
        compiler_params=_cparams("arbitrary", "arbitrary"),
        name="moe_ffn_b",
    )(tid, te, tf, nv, hs, w2)


def moe_sparse(h, logits_t, router_b, w1, w3, w2, layer):
    gates, mask = router(logits_t, router_b)
    plan = dispatch_plan(gates, mask, MOE_TILE)
    xs = gather_rows(h, plan[0], plan[6], MOE_TILE)
    hs = ffn_stage_a(xs, w1, w3, layer, plan, MOE_TILE)
    ys = ffn_stage_b(hs, w2, layer, plan, MOE_TILE)
    return ys, plan[2], plan[1]


def rope_tables(n, lc, hd):
    half = hd // 2
    t = jnp.arange(n)
    inv = jnp.power(ROPE_BASE, -jnp.arange(0, half, 2, dtype=F32) / half)
    ang_r = (t // GRID_W).astype(F32)[:, None] * inv
    ang_c = (t % GRID_W).astype(F32)[:, None] * inv
    cos = jnp.concatenate([jnp.cos(ang_r)] * 2 + [jnp.cos(ang_c)] * 2, axis=1)
    sin = jnp.concatenate([-jnp.sin(ang_r), jnp.sin(ang_r), -jnp.sin(ang_c), jnp.sin(ang_c)], axis=1)
    cos = jnp.concatenate([jnp.ones((lc, hd), F32), cos], axis=0)
    sin = jnp.concatenate([jnp.zeros((lc, hd), F32), sin], axis=0)
    scale = hd ** -0.5
    ones, zeros = jnp.ones_like(cos), jnp.zeros_like(sin)
    return jnp.stack([cos * scale, cos, ones]), jnp.stack([sin * scale, sin, zeros])


def kernel(x, c, ctx, c_ctx, ada_w, ada_b, ln_g, ln_b, mlstm_w_in, mlstm_b_gates, mlstm_norm_w, mlstm_w_out,
           attn_w_in, attn_b_in, attn_sink, attn_w_out, router_w, router_b, moe_w1, moe_w3, moe_w2):
    bsz, n, d = x.shape
    lc = ctx.shape[1]
    depth = ada_w.shape[0]
    assert depth == 2 and bsz <= 4 and n % lc == 0 and lc % ML_CHUNK == 0
    alpha = (2 * depth) ** 0.25
    tile = lc
    per = (n + lc) // tile
    per_lat = n // tile

    cc = jnp.concatenate([c, c_ctx[None, :], jnp.zeros((8 - bsz - 1, d), F32)], axis=0)
    ada = adaln(cc, ada_w, ada_b)
    ada0 = ada[0].reshape(8, 1, 6 * d)
    ada1 = ada[1].reshape(8, 1, 6 * d)
    rwt = router_w.T

    heads = mlstm_b_gates.shape[1] // 4
    dv = d // heads
    dqk = dv // 2
    nmain = 2 * heads * (dqk + dv)
    h0, gates_pre = mod0(x, ctx, ada0, mlstm_w_in[0, :, nmain:], mlstm_b_gates[0][None, :])
    qkvo = matmul(h0, mlstm_w_in, n_cols=nmain)
    ncc, nct = lc // ML_CHUNK, (n + lc) // ML_CHUNK
    hg = mlstm_bidir(qkvo, gates_pre, mlstm_norm_w[0], bsz, heads, dqk, dv, ncc, nct)
    y = matmul(hg, mlstm_w_out, out_dtype=F32)
    x1, hffn, lg = ln_block(x, y, ada0, 2, ln_g[0, 0], ln_b[0, 0], alpha=alpha, nb=bsz, per=per, tile=tile,
                            ctx_rows=True, ctx=ctx, nxt=(ada0, 3, 4), h_dtype=F32, router_wt=rwt)
    ys, pos, pos_w = moe_sparse(hffn, lg, router_b, moe_w1, moe_w3, moe_w2, 0)
    x2, h1 = ln_block(x1, ys, ada0, 5, ln_g[0, 1], ln_b[0, 1], alpha=alpha, nb=bsz, per=per, tile=tile,
                      ctx_rows=True, x_per=per, nxt=(ada1, 0, 1), h_dtype=BF16, pos=pos, pos_w=pos_w)

    heads_a = attn_sink.shape[1]
    hd = d // heads_a
    kvh = (attn_w_in.shape[2] // hd - heads_a) // 2
    cos_t, sin_t = rope_tables(n, lc, hd)
    q_end, k_end = heads_a * hd, (heads_a + kvh) * hd
    type_of = lambda col: jnp.where(col < q_end, 0, jnp.where(col < k_end, 1, 2))
    tm1 = _pick(n + lc, (1152, 768, 384, 256, 128))
    qkv = matmul(h1, attn_w_in, bias=attn_b_in[0], rope=(cos_t, sin_t, n + lc, type_of), tm=tm1,
                 tn=_pick(kvh * hd, (512, 256, 128)))
    att = window_attention(qkv.reshape(bsz, n + lc, -1), attn_sink[0], lc, n, heads_a, kvh, hd)
    y = matmul(att.reshape(bsz * n, heads_a * hd), attn_w_out, out_dtype=F32)
    x3, hffn, lg = ln_block(x2, y, ada1, 2, ln_g[1, 0], ln_b[1, 0], alpha=alpha, nb=bsz, per=per_lat, tile=tile,
                            ctx_rows=False, x_per=per, x_off=1, nxt=(ada1, 3, 4), h_dtype=F32, router_wt=rwt)
    ys, pos, pos_w = moe_sparse(hffn, lg, router_b, moe_w1, moe_w3, moe_w2, 1)
    (x4,) = ln_block(x3, ys, ada1, 5, ln_g[1, 1], ln_b[1, 1], alpha=alpha, nb=bsz, per=per_lat, tile=tile,
                     ctx_rows=False, x_per=per_lat, pos=pos, pos_w=pos_w)
    return x4.reshape(bsz, n, d)
```

```python
import functools

import jax
import jax.numpy as jnp
from jax import lax
from jax.experimental import pallas as pl
from jax.experimental.pallas import tpu as pltpu

F32 = jnp.float32
BF16 = jnp.bfloat16
NEG_INF = float("-inf")

GRID_W = 64
ML_CHUNK = 128
AT_WINDOW = 128
AT_BLOCK = 128
ROPE_BASE = 10000.0
N_GROUPS = 4
TOP_K = 2
LN_EPS = 1e-5
RMS_EPS = 1e-6

LANES = 128
VMEM_LIMIT = 56 * 1024 * 1024
MOE_TILE = 512
HIGHEST = lax.Precision.HIGHEST
NT_DIMS = (((1,), (1,)), ((), ()))


def _cparams(*sem):
    return pltpu.CompilerParams(dimension_semantics=sem, vmem_limit_bytes=VMEM_LIMIT)


def _pick(n, candidates):
    for c in candidates:
        if n % c == 0:
            return c
    return n


def _sigmoid(x):
    return 1.0 / (1.0 + jnp.exp(-x))


def _log_sigmoid(x):
    return jnp.minimum(x, 0.0) - jnp.log1p(jnp.exp(-jnp.abs(x)))


def _adaln_kernel(c_ref, w_ref, b_ref, o_ref):
    c = c_ref[...]
    a = (c * _sigmoid(c)).astype(BF16)
    o_ref[0] = jnp.dot(a, w_ref[0].astype(BF16), preferred_element_type=F32) + b_ref[0]


def adaln(cc, ada_w, ada_b):
    depth, d, n6 = ada_w.shape
    tn = _pick(n6, (512, 256, 128))
    return pl.pallas_call(
        _adaln_kernel,
        grid=(depth, n6 // tn),
        in_specs=[
            pl.BlockSpec((8, d), lambda l, j: (0, 0)),
            pl.BlockSpec((1, d, tn), lambda l, j: (l, 0, j)),
            pl.BlockSpec((1, 1, tn), lambda l, j: (l, 0, j)),
        ],
        out_specs=pl.BlockSpec((1, 8, tn), lambda l, j: (l, 0, j)),
        out_shape=jax.ShapeDtypeStruct((depth, 8, n6), F32),
        compiler_params=_cparams("parallel", "parallel"),
        name="adaln",
    )(cc, ada_w, ada_b.reshape(depth, 1, n6))


def _mod0_kernel(x_ref, c_ref, sh_ref, sc_ref, wg_ref, bg_ref, h_ref, g_ref):
    j = pl.program_id(1)
    t = jnp.where(j == 0, c_ref[0], x_ref[0])
    h = t * (1.0 + sc_ref[0]) + sh_ref[0]
    h_ref[...] = h.astype(BF16)
    g_ref[...] = lax.dot_general(h, wg_ref[...], NT_DIMS, preferred_element_type=F32,
                                 precision=HIGHEST) + bg_ref[...]


def mod0(x, ctx, ada3, wg, bg):
    bsz, n, d = x.shape
    lc = ctx.shape[1]
    per = (n + lc) // lc
    ng = wg.shape[0]
    prow = lambda b, j: jnp.where(j == 0, bsz, b)
    return pl.pallas_call(
        _mod0_kernel,
        grid=(bsz, per),
        in_specs=[
            pl.BlockSpec((1, lc, d), lambda b, j: (b, jnp.maximum(j - 1, 0), 0)),
            pl.BlockSpec((1, lc, d), lambda b, j: (b, 0, 0)),
            pl.BlockSpec((1, 1, d), lambda b, j: (prow(b, j), 0, 0)),
            pl.BlockSpec((1, 1, d), lambda b, j: (prow(b, j), 0, 1)),
            pl.BlockSpec((ng, d), lambda b, j: (0, 0)),
            pl.BlockSpec((1, ng), lambda b, j: (0, 0)),
        ],
        out_specs=[
            pl.BlockSpec((lc, d), lambda b, j: (b * per + j, 0)),
            pl.BlockSpec((lc, ng), lambda b, j: (b * per + j, 0)),
        ],
        out_shape=[
            jax.ShapeDtypeStruct((bsz * (n + lc), d), BF16),
            jax.ShapeDtypeStruct((bsz * (n + lc), ng), F32),
        ],
        compiler_params=_cparams("parallel", "arbitrary"),
        name="mod0",
    )(x, ctx, ada3, ada3, wg, bg)


def _mm_kernel(*refs, has_bias, has_rope, tn, w_t):
    x_ref, w_ref = refs[0], refs[1]
    pos = 2
    if w_t:
        acc = lax.dot_general(x_ref[...], w_ref[...].astype(BF16), NT_DIMS, preferred_element_type=F32)
    else:
        acc = jnp.dot(x_ref[...], w_ref[...].astype(BF16), preferred_element_type=F32)
    if has_bias:
        acc = acc + refs[pos][...]
        pos += 1
    o_ref = refs[-1]
    if has_rope:
        ca = refs[pos][0]
        sb = refs[pos + 1][0]
        lane = lax.broadcasted_iota(jnp.int32, ca.shape, 1)
        first = (lane % 64) < 32
        for h in range(tn // LANES):
            y = acc[:, h * LANES:(h + 1) * LANES]
            sw = jnp.where(first, pltpu.roll(y, 96, 1), pltpu.roll(y, 32, 1))
            o_ref[:, h * LANES:(h + 1) * LANES] = (y * ca + sw * sb).astype(o_ref.dtype)
    else:
        o_ref[...] = acc.astype(o_ref.dtype)


def matmul(x, w, bias=None, rope=None, out_dtype=BF16, tm=None, tn=None, n_cols=None, w_t=False):
    m, k = x.shape
    n = n_cols or w.shape[1 if w_t else 2]
    tm = tm or _pick(m, (1152, 1024, 768, 512, 256, 128))
    tn = tn or _pick(n, (512, 256, 128))
    in_specs = [
        pl.BlockSpec((tm, k), lambda i, j: (i, 0)),
        pl.BlockSpec((None, tn, k), lambda i, j: (0, j, 0)) if w_t else
        pl.BlockSpec((None, k, tn), lambda i, j: (0, 0, j)),
    ]
    args = [x, w]
    if bias is not None:
        in_specs.append(pl.BlockSpec((1, tn), lambda i, j: (0, j)))
        args.append(bias.reshape(1, n))
    if rope is not None:
        ca, sb, period, type_of = rope
        nper = period // tm
        spec = pl.BlockSpec((1, tm, LANES), lambda i, j: (type_of(j * tn), i % nper, 0))
        in_specs += [spec, spec]
        args += [ca, sb]
    return pl.pallas_call(
        functools.partial(_mm_kernel, has_bias=bias is not None, has_rope=rope is not None, tn=tn, w_t=w_t),
        grid=(m // tm, n // tn),
        in_specs=in_specs,
        out_specs=pl.BlockSpec((tm, tn), lambda i, j: (i, j)),
        out_shape=jax.ShapeDtypeStruct((m, n), out_dtype),
        compiler_params=_cparams("parallel", "arbitrary"),
        name="matmul",
    )(*args)


def _mlstm_chunk(q, k, v, ig_row, f_row, ig_col, f_col, st_ref, n_ref, m_ref, d, scale):
    L = ML_CHUNK
    reverse = d == 1
    lf_row = _log_sigmoid(f_row)
    lf_col = _log_sigmoid(f_col)
    r = lax.broadcasted_iota(jnp.int32, (L, L), 0)
    c = lax.broadcasted_iota(jnp.int32, (L, L), 1)
    keep = (c >= r) if reverse else (c <= r)
    keep_t = (r >= c) if reverse else (r <= c)
    b_col = jnp.sum(jnp.where(keep, lf_row, 0.0), axis=1, keepdims=True)
    b_row = jnp.sum(jnp.where(keep_t, lf_col, 0.0), axis=0, keepdims=True)

    m_prev = m_ref[d]
    logw = jnp.where(keep, b_col - b_row + ig_row, NEG_INF)
    inter = b_col + m_prev
    m_t = jnp.maximum(inter, jnp.max(logw, axis=1, keepdims=True))
    w_inter = jnp.exp(inter - m_t)
    p = jnp.exp(logw - m_t)

    q = (q.astype(F32) * scale).astype(BF16)
    st = st_ref[d]
    n_row = n_ref[d]
    s = lax.dot_general(q, k, NT_DIMS, preferred_element_type=F32) * p
    num = (w_inter * jnp.dot(q, st.astype(BF16), preferred_element_type=F32)
           + jnp.dot(s.astype(BF16), v, preferred_element_type=F32))
    qn = jnp.sum(q.astype(F32) * n_row, axis=1, keepdims=True)
    den = w_inter * qn + jnp.sum(s, axis=1, keepdims=True)
    h = num / jnp.maximum(jnp.abs(den), jnp.exp(-m_t))

    b_end = b_col[0:1, :] if reverse else b_col[L - 1:L, :]
    m_new = jnp.maximum(b_end + m_prev, jnp.max(b_end - b_row + ig_row, axis=1, keepdims=True))
    decay = jnp.exp(b_end + m_prev - m_new)
    w_src = jnp.exp(b_end - b_col + ig_col - m_new)
    kf = k.astype(F32)
    vw = (v.astype(F32) * w_src).astype(BF16)
    st_ref[d] = decay * st + jnp.dot(kf.T.astype(BF16), vw, preferred_element_type=F32)
    n_ref[d] = decay * n_row + jnp.sum(kf * w_src, axis=0, keepdims=True)
    m_ref[d] = m_new
    return h


def _mlstm_kernel(q_ref, k_ref, v_ref, o_ref, gr_ref, gc_ref, nw_ref, out_ref, acc_ref, st_ref, n_ref, m_ref,
                  *, scale, ncc, nct):
    L = ML_CHUNK
    st_ref[...] = jnp.zeros_like(st_ref)
    n_ref[...] = jnp.zeros_like(n_ref)
    m_ref[...] = jnp.zeros_like(m_ref)
    acc_ref[...] = jnp.zeros_like(acc_ref)

    def step(j, carry):
        for d, cj in ((0, j), (1, jnp.where(j < ncc, ncc - 1 - j, nct - 1 - (j - ncc)))):
            rows = pl.ds(pl.multiple_of(cj * L, L), L)
            gcol = gc_ref[0, 0, cj]
            h = _mlstm_chunk(q_ref[rows, :], k_ref[rows, :], v_ref[rows, :],
                             gr_ref[0, 2 * d, 0, cj], gr_ref[0, 2 * d + 1, 0, cj],
                             gcol[:, 2 * d:2 * d + 1], gcol[:, 2 * d + 1:2 * d + 2],
                             st_ref, n_ref, m_ref, d, scale)
            acc_ref[rows, :] += h
        return carry

    lax.fori_loop(0, nct, step, 0)

    def finish(cj, carry):
        rows = pl.ds(pl.multiple_of(cj * L, L), L)
        hs = acc_ref[rows, :]
        hn = hs * lax.rsqrt(jnp.mean(hs * hs, axis=1, keepdims=True) + RMS_EPS)
        out_ref[rows, :] = (hn * nw_ref[...] * _sigmoid(o_ref[rows, :].astype(F32))).astype(out_ref.dtype)
        return carry

    lax.fori_loop(0, nct, finish, 0)


def mlstm_bidir(qkvo, gates_pre, norm_w, bsz, heads, dqk, dv, ncc, nct):
    t = qkvo.shape[0]
    L = ML_CHUNK
    n2 = nct * L
    g4 = gates_pre.reshape(bsz, nct, L, 4, heads)
    g_rows = jnp.transpose(g4, (0, 3, 4, 1, 2)).reshape(bsz, 4, heads, nct, 1, L)
    g_cols = jnp.transpose(g4, (0, 4, 1, 2, 3))
    koff = heads
    voff = 2 * heads * dqk // dv
    ooff = voff + heads
    return pl.pallas_call(
        functools.partial(_mlstm_kernel, scale=dqk ** -0.5, ncc=ncc, nct=nct),
        grid=(bsz, heads),
        in_specs=[
            pl.BlockSpec((n2, dqk), lambda b, h: (b, h)),
            pl.BlockSpec((n2, dqk), lambda b, h: (b, koff + h)),
            pl.BlockSpec((n2, dv), lambda b, h: (b, voff + h)),
            pl.BlockSpec((n2, dv), lambda b, h: (b, ooff + h)),
            pl.BlockSpec((1, 4, 1, nct, 1, L), lambda b, h: (b, 0, h, 0, 0, 0)),
            pl.BlockSpec((1, 1, nct, L, 4), lambda b, h: (b, h, 0, 0, 0)),
            pl.BlockSpec((1, dv), lambda b, h: (0, h)),
        ],
        out_specs=pl.BlockSpec((n2, dv), lambda b, h: (b, h)),
        out_shape=jax.ShapeDtypeStruct((t, heads * dv), BF16),
        scratch_shapes=[pltpu.VMEM((n2, dv), F32), pltpu.VMEM((2, dqk, dv), F32),
                        pltpu.VMEM((2, 1, dqk), F32), pltpu.VMEM((2, 1, 1), F32)],
        compiler_params=_cparams("parallel", "parallel"),
        name="mlstm",
    )(qkvo, qkvo, qkvo, qkvo, g_rows, g_cols, norm_w.reshape(1, heads * dv))


def _attn_kernel(sink_ref, q_ref, k_ref, v_ref, o_ref, bias_ref, *, lc, nb, group, hd):
    kv = pl.program_id(1)
    blk = AT_BLOCK
    rows = group * blk
    kc = k_ref[0, 0:lc, :]
    vc = v_ref[0, 0:lc, :]
    rid = lax.broadcasted_iota(jnp.int32, (rows, 1), 0)
    sink_col = jnp.zeros((rows, 1), F32)
    for g in range(group):
        sink_col = jnp.where(rid // blk == g, sink_ref[kv * group + g], sink_col)
    qrow = lax.broadcasted_iota(jnp.int32, (rows, 3 * blk), 0) % blk
    kcol = lax.broadcasted_iota(jnp.int32, (rows, 3 * blk), 1)
    for back in range(3):
        rel = kcol - back * blk - qrow
        bias_ref[back] = jnp.where(jnp.abs(rel) <= AT_WINDOW, 0.0, NEG_INF)

    def body(j, carry):
        ws = jnp.clip(j - 1, 0, nb - 3)
        qs = pl.multiple_of(lc + j * blk, blk)
        ks = pl.multiple_of(lc + ws * blk, blk)
        q4 = q_ref[0, pl.ds(qs, blk), :]
        qq = jnp.concatenate([q4[:, g * hd:(g + 1) * hd] for g in range(group)], axis=0)
        kw = k_ref[0, pl.ds(ks, 3 * blk), :]
        vw = v_ref[0, pl.ds(ks, 3 * blk), :]
        s_loc = lax.dot_general(qq, kw, NT_DIMS, preferred_element_type=F32)
        s_ctx = lax.dot_general(qq, kc, NT_DIMS, preferred_element_type=F32)
        s_loc = s_loc + bias_ref[j - ws]
        m = jnp.maximum(jnp.maximum(jnp.max(s_loc, axis=1, keepdims=True),
                                    jnp.max(s_ctx, axis=1, keepdims=True)), sink_col)
        p_loc = jnp.exp(s_loc - m)
        p_ctx = jnp.exp(s_ctx - m)
        den = (jnp.sum(p_loc, axis=1, keepdims=True) + jnp.sum(p_ctx, axis=1, keepdims=True)
               + jnp.exp(sink_col - m))
        o = (jnp.dot(p_loc.astype(BF16), vw, preferred_element_type=F32)
             + jnp.dot(p_ctx.astype(BF16), vc, preferred_element_type=F32)) / den
        os = pl.multiple_of(j * blk, blk)
        for g in range(group):
            o_ref[0, pl.ds(os, blk), g * hd:(g + 1) * hd] = o[g * blk:(g + 1) * blk].astype(o_ref.dtype)
        return carry

    lax.fori_loop(0, nb, body, 0, unroll=2)


def window_attention(qkv3, sink, lc, n, heads, kv_heads, hd):
    bsz = qkv3.shape[0]
    group = heads // kv_heads
    nb = n // AT_BLOCK
    assert nb >= 3
    koff = heads
    voff = heads + kv_heads
    return pl.pallas_call(
        functools.partial(_attn_kernel, lc=lc, nb=nb, group=group, hd=hd),
        grid=(bsz, kv_heads),
        in_specs=[
            pl.BlockSpec(memory_space=pltpu.SMEM),
            pl.BlockSpec((1, lc + n, group * hd), lambda b, g: (b, 0, g)),
            pl.BlockSpec((1, lc + n, hd), lambda b, g: (b, 0, koff + g)),
            pl.BlockSpec((1, lc + n, hd), lambda b, g: (b, 0, voff + g)),
        ],
        out_specs=pl.BlockSpec((1, n, group * hd), lambda b, g: (b, 0, g)),
        out_shape=jax.ShapeDtypeStruct((bsz, n, heads * hd), BF16),
        scratch_shapes=[pltpu.VMEM((3, group * AT_BLOCK, 3 * AT_BLOCK), F32)],
        compiler_params=_cparams("parallel", "parallel"),
        name="window_attention",
    )(sink, qkv3, qkv3, qkv3)


def _ln_kernel(*refs, dual, gather, tile, has_h, has_router, alpha):
    refs = list(refs)
    if gather:
        pos_ref = refs.pop(0)
    x_ref = refs.pop(0)
    c_ref = refs.pop(0) if dual else None
    y_ref = refs.pop(0)
    gw_ref = refs.pop(0) if gather else None
    gate_ref, lng_ref, lnb_ref = refs.pop(0), refs.pop(0), refs.pop(0)
    if has_h:
        sh_ref, sc_ref = refs.pop(0), refs.pop(0)
    if has_router:
        rw_ref = refs.pop(0)
    xo_ref = refs.pop(0)
    if has_h:
        ho_ref = refs.pop(0)
    if has_router:
        lg_ref = refs.pop(0)

    if gather:
        buf_ref, sems = refs.pop(0), refs.pop(0)
        lin = pl.program_id(0) * pl.num_programs(1) + pl.program_id(1)
        ntiles = pl.num_programs(0) * pl.num_programs(1)
        ntok = pos_ref.shape[0] // 2

        def issue(t):
            slot = t % 2

            def start(r, carry):
                for k in range(2):
                    src = pos_ref[k * ntok + t * tile + r]
                    pltpu.make_async_copy(y_ref.at[pl.ds(src, 1)], buf_ref.at[slot, k, pl.ds(r, 1)],
                                          sems.at[slot]).start(priority=k)
                return carry

            lax.fori_loop(0, tile, start, 0, unroll=4)

        @pl.when(lin == 0)
        def _():
            issue(lin)

        @pl.when(lin + 1 < ntiles)
        def _():
            issue(lin + 1)

        slot = lin % 2
        for k in range(2):
            pltpu.make_async_copy(y_ref.at[pl.ds(0, tile)], buf_ref.at[slot, k], sems.at[slot]).wait()
        gw = gw_ref[...]
        y = gw[:, 0:1] * buf_ref[slot, 0] + gw[:, 1:2] * buf_ref[slot, 1]
    else:
        y = y_ref[...].astype(F32)

    if dual:
        x = jnp.where(pl.program_id(1) == 0, c_ref[0], x_ref[0])
    else:
        x = x_ref[...]
    z = alpha * x + gate_ref[0] * y
    mu = jnp.mean(z, axis=1, keepdims=True)
    zc = z - mu
    var = jnp.mean(zc * zc, axis=1, keepdims=True)
    xn = zc * lax.rsqrt(var + LN_EPS) * lng_ref[...] + lnb_ref[...]
    xo_ref[...] = xn
    if has_h:
        h = xn * (1.0 + sc_ref[0]) + sh_ref[0]
        ho_ref[...] = h.astype(ho_ref.dtype)
        if has_router:
            lg_ref[...] = lax.dot_general(rw_ref[...], h, NT_DIMS, preferred_element_type=F32,
                                          precision=HIGHEST)


def ln_block(x, y, ada3, gate_col, lng, lnb, *, alpha, nb, per, tile, ctx_rows, ctx=None, x_per=None,
             x_off=0, nxt=None, h_dtype=None, router_wt=None, pos=None, pos_w=None):
    d = x.shape[-1]
    dual = ctx is not None
    gather = pos is not None
    rows = nb * per * tile
    prow = (lambda b, j: jnp.where(j == 0, nb, b)) if ctx_rows else (lambda b, j: b)

    def im(f):
        return lambda b, j, *_: f(b, j)

    in_specs, args = [], []
    if dual:
        in_specs.append(pl.BlockSpec((1, tile, d), im(lambda b, j: (b, jnp.maximum(j - 1, 0), 0))))
        in_specs.append(pl.BlockSpec((1, tile, d), im(lambda b, j: (b, 0, 0))))
        args += [x, ctx]
    else:
        in_specs.append(pl.BlockSpec((tile, d), im(lambda b, j: (b * x_per + x_off + j, 0))))
        args.append(x)
    if gather:
        in_specs.append(pl.BlockSpec(memory_space=pl.ANY))
        in_specs.append(pl.BlockSpec((tile, 2), im(lambda b, j: (b * per + j, 0))))
        args += [y, pos_w]
    else:
        in_specs.append(pl.BlockSpec((tile, d), im(lambda b, j: (b * per + j, 0))))
        args.append(y)
    in_specs.append(pl.BlockSpec((1, 1, d), im(lambda b, j: (prow(b, j), 0, gate_col))))
    in_specs.append(pl.BlockSpec((1, d), im(lambda b, j: (0, 0))))
    in_specs.append(pl.BlockSpec((1, d), im(lambda b, j: (0, 0))))
    args += [ada3, lng.reshape(1, d), lnb.reshape(1, d)]
    if nxt is not None:
        ada_n, shc, scc = nxt
        in_specs.append(pl.BlockSpec((1, 1, d), im(lambda b, j: (prow(b, j), 0, shc))))
        in_specs.append(pl.BlockSpec((1, 1, d), im(lambda b, j: (prow(b, j), 0, scc))))
        args += [ada_n, ada_n]
    if router_wt is not None:
        ne = router_wt.shape[0]
        in_specs.append(pl.BlockSpec((ne, d), im(lambda b, j: (0, 0))))
        args.append(router_wt)

    out_specs = [pl.BlockSpec((tile, d), im(lambda b, j: (b * per + j, 0)))]
    out_shape = [jax.ShapeDtypeStruct((rows, d), F32)]
    if nxt is not None:
        out_specs.append(pl.BlockSpec((tile, d), im(lambda b, j: (b * per + j, 0))))
        out_shape.append(jax.ShapeDtypeStruct((rows, d), h_dtype))
    if router_wt is not None:
        out_specs.append(pl.BlockSpec((ne, tile), im(lambda b, j: (0, b * per + j))))
        out_shape.append(jax.ShapeDtypeStruct((ne, rows), F32))

    scratch = [pltpu.VMEM((2, 2, tile, d), F32), pltpu.SemaphoreType.DMA((2,))] if gather else []
    kern = functools.partial(_ln_kernel, dual=dual, gather=gather, tile=tile, has_h=nxt is not None,
                             has_router=router_wt is not None, alpha=alpha)
    call = pl.pallas_call(
        kern,
        grid_spec=pltpu.PrefetchScalarGridSpec(
            num_scalar_prefetch=1 if gather else 0,
            grid=(nb, per),
            in_specs=in_specs,
            out_specs=out_specs,
            scratch_shapes=scratch,
        ),
        out_shape=out_shape,
        compiler_params=_cparams("arbitrary", "arbitrary"),
        name="ln_gather" if gather else "ln_block",
    )
    return call(*(([pos] if gather else []) + args))


def _router_kernel(b_ref, l_ref, g_ref, m_ref, *, ne):
    epg = ne // N_GROUPS
    logits = l_ref[...]
    s = [_sigmoid(logits[e:e + 1, :]) for e in range(ne)]
    sel = [s[e] + b_ref[e] for e in range(ne)]
    gscore = []
    for g in range(N_GROUPS):
        v = sel[g * epg:(g + 1) * epg]
        best = None
        for i in range(epg):
            for j in range(i + 1, epg):
                pair = v[i] + v[j]
                best = pair if best is None else jnp.maximum(best, pair)
        gscore.append(best)
    chosen = []
    for g in range(N_GROUPS):
        ok = None
        for o in range(N_GROUPS):
            if o == g:
                continue
            t = (gscore[g] > gscore[o]) if o < g else (gscore[g] >= gscore[o])
            ok = t if ok is None else jnp.logical_and(ok, t)
        chosen.append(ok)
    mask = []
    for g in range(N_GROUPS):
        v = sel[g * epg:(g + 1) * epg]
        for i in range(epg):
            rank = jnp.zeros_like(v[i])
            for j in range(epg):
                if j == i:
                    continue
                ahead = (v[j] >= v[i]) if j < i else (v[j] > v[i])
                rank = rank + jnp.where(ahead, 1.0, 0.0)
            mask.append(jnp.logical_and(chosen[g], rank < TOP_K))
    wsum = jnp.zeros_like(s[0])
    for e in range(ne):
        wsum = wsum + jnp.where(mask[e], s[e], 0.0)
    for e in range(ne):
        g_ref[e:e + 1, :] = jnp.where(mask[e], s[e] / wsum, 0.0)
        m_ref[e:e + 1, :] = jnp.where(mask[e], 1, 0).astype(jnp.int32)


def router(logits_t, router_b):
    ne, t = logits_t.shape
    tl = _pick(t, (1024, 512, 256, 128))
    spec = pl.BlockSpec((ne, tl), lambda i: (0, i))
    return pl.pallas_call(
        functools.partial(_router_kernel, ne=ne),
        grid=(t // tl,),
        in_specs=[pl.BlockSpec(memory_space=pltpu.SMEM), spec],
        out_specs=[spec, spec],
        out_shape=[jax.ShapeDtypeStruct((ne, t), F32), jax.ShapeDtypeStruct((ne, t), jnp.int32)],
        compiler_params=_cparams("parallel"),
        name="router",
    )(router_b, logits_t)


def dispatch_plan(gates, mask, tm):
    ne, t = mask.shape
    n_tiles = (TOP_K * t) // tm + ne
    rows = n_tiles * tm
    sel = mask > 0
    counts = jnp.sum(mask, axis=1)
    ptiles = (counts + tm - 1) // tm
    tile_end = jnp.cumsum(ptiles)
    off = (tile_end - ptiles) * tm
    dest = off[:, None] + jnp.cumsum(mask, axis=1) - mask
    pos_lo = jnp.min(jnp.where(sel, dest, rows), axis=0)
    pos_hi = jnp.max(jnp.where(sel, dest, -1), axis=0)
    g_lo = jnp.sum(jnp.where(sel & (dest == pos_lo[None, :]), gates, 0.0), axis=0)
    g_hi = jnp.sum(jnp.where(sel & (dest == pos_hi[None, :]), gates, 0.0), axis=0)
    pos = jnp.concatenate([pos_lo, pos_hi]).astype(jnp.int32)
    tok = jnp.arange(t, dtype=jnp.int32)
    row_token = jnp.zeros((rows,), jnp.int32).at[pos].set(jnp.concatenate([tok, tok]), unique_indices=True)
    n_valid = tile_end[-1]
    tid = jnp.minimum(jnp.arange(n_tiles, dtype=jnp.int32), n_valid - 1)
    tile_expert = jnp.minimum(jnp.sum(tile_end[None, :] <= tid[:, None], axis=1), ne - 1).astype(jnp.int32)
    prev = jnp.concatenate([jnp.full((1,), -1, jnp.int32), tile_expert[:-1]])
    tile_first = (tile_expert != prev).astype(jnp.int32)
    nv = jnp.reshape(n_valid, (1,)).astype(jnp.int32)
    return row_token, jnp.stack([g_lo, g_hi], axis=1), pos, tid, tile_expert, tile_first, nv


def _gather_kernel(tok_ref, nv_ref, h_ref, o_ref, buf_ref, sems, *, tm):
    i = pl.program_id(0)
    nv = nv_ref[0]

    def issue(tile):
        slot = tile % 2

        def start(r2, carry):
            for u in range(2):
                r = 2 * r2 + u
                pltpu.make_async_copy(h_ref.at[pl.ds(tok_ref[tile * tm + r], 1)], buf_ref.at[slot, pl.ds(r, 1)],
                                      sems.at[slot]).start(priority=u)
            return carry

        lax.fori_loop(0, tm // 2, start, 0, unroll=4)

    @pl.when(i == 0)
    def _():
        issue(i)

    @pl.when(i + 1 < nv)
    def _():
        issue(i + 1)

    @pl.when(i < nv)
    def _():
        slot = i % 2
        pltpu.make_async_copy(h_ref.at[pl.ds(0, tm)], buf_ref.at[slot], sems.at[slot]).wait()
        o_ref[...] = buf_ref[slot].astype(o_ref.dtype)

    @pl.when(i >= nv)
    def _():
        o_ref[...] = jnp.zeros_like(o_ref)


def gather_rows(h, row_token, nv, tm):
    d = h.shape[1]
    rows = row_token.shape[0]
    return pl.pallas_call(
        functools.partial(_gather_kernel, tm=tm),
        grid_spec=pltpu.PrefetchScalarGridSpec(
            num_scalar_prefetch=2,
            grid=(rows // tm,),
            in_specs=[pl.BlockSpec(memory_space=pl.ANY)],
            out_specs=pl.BlockSpec((tm, d), lambda i, tok, nvr: (i, 0)),
            scratch_shapes=[pltpu.VMEM((2, tm, d), F32), pltpu.SemaphoreType.DMA((2,))],
        ),
        out_shape=jax.ShapeDtypeStruct((rows, d), BF16),
        compiler_params=_cparams("arbitrary"),
        name="moe_gather",
    )(row_token, nv, h)


def _ffn_a_kernel(tid_ref, te_ref, tf_ref, nv_ref, x_ref, w1_ref, w3_ref, o_ref, w1b, w3b):
    i = pl.program_id(1)

    @pl.when(i < nv_ref[0])
    def _():
        @pl.when(tf_ref[i] == 1)
        def _():
            w1b[...] = w1_ref[0].astype(BF16)
            w3b[...] = w3_ref[0].astype(BF16)

        x = x_ref[...]
        a = jnp.dot(x, w1b[...], preferred_element_type=F32)
        b = jnp.dot(x, w3b[...], preferred_element_type=F32)
        o_ref[...] = (a * _sigmoid(a) * b).astype(o_ref.dtype)

    @pl.when(i >= nv_ref[0])
    def _():
        o_ref[...] = jnp.zeros_like(o_ref)


def ffn_stage_a(xs, w1, w3, layer, plan, tm):
    _, _, _, tid, te, tf, nv = plan
    rows, d = xs.shape
    ff = w1.shape[3]
    tfc = _pick(ff, (512, 256, 128))
    wspec = pl.BlockSpec((None, 1, d, tfc), lambda f, i, tid, te, tf, nv: (layer, te[i], 0, f))
    return pl.pallas_call(
        _ffn_a_kernel,
        grid_spec=pltpu.PrefetchScalarGridSpec(
            num_scalar_prefetch=4,
            grid=(ff // tfc, rows // tm),
            in_specs=[pl.BlockSpec((tm, d), lambda f, i, tid, te, tf, nv: (tid[i], 0)), wspec, wspec],
            out_specs=pl.BlockSpec((tm, tfc), lambda f, i, tid, te, tf, nv: (i, f)),
            scratch_shapes=[pltpu.VMEM((d, tfc), BF16), pltpu.VMEM((d, tfc), BF16)],
        ),
        out_shape=jax.ShapeDtypeStruct((rows, ff), BF16),
        compiler_params=_cparams("arbitrary", "arbitrary"),
        name="moe_ffn_a",
    )(tid, te, tf, nv, xs, w1, w3)


def _ffn_b_kernel(tid_ref, te_ref, tf_ref, nv_ref, h_ref, w2_ref, o_ref, w2b):
    i = pl.program_id(1)

    @pl.when(i < nv_ref[0])
    def _():
        @pl.when(tf_ref[i] == 1)
        def _():
            w2b[...] = w2_ref[0].astype(BF16)

        o_ref[...] = jnp.dot(h_ref[...], w2b[...], preferred_element_type=F32)

    @pl.when(i >= nv_ref[0])
    def _():
        o_ref[...] = jnp.zeros_like(o_ref)


def ffn_stage_b(hs, w2, layer, plan, tm):
    _, _, _, tid, te, tf, nv = plan
    rows, ff = hs.shape
    d = w2.shape[3]
    tn = _pick(d, (2048, 1024, 512, 256, 128))
    return pl.pallas_call(
        _ffn_b_kernel,
        grid_spec=pltpu.PrefetchScalarGridSpec(
            num_scalar_prefetch=4,
            grid=(d // tn, rows // tm),
            in_specs=[
                pl.BlockSpec((tm, ff), lambda n, i, tid, te, tf, nv: (tid[i], 0)),
                pl.BlockSpec((None, 1, ff, tn), lambda n, i, tid, te, tf, nv: (layer, te[i], 0, n)),
            ],
            out_specs=pl.BlockSpec((tm, tn), lambda n, i, tid, te, tf, nv: (i, n)),
            scratch_shapes=[pltpu.VMEM((ff, tn), BF16)],
        ),
        out_shape=jax.ShapeDtypeStruct((rows, d), F32),
        compiler_params=_cparams("arbitrary", "arbitrary"),
        name="moe_ffn_b",
    )(tid, te, tf, nv, hs, w2)


def moe_sparse(h, logits_t, router_b, w1, w3, w2, layer):
    gates, mask = router(logits_t, router_b)
    plan = dispatch_plan(gates, mask, MOE_TILE)
    xs = gather_rows(h, plan[0], plan[6], MOE_TILE)
    hs = ffn_stage_a(xs, w1, w3, layer, plan, MOE_TILE)
    ys = ffn_stage_b(hs, w2, layer, plan, MOE_TILE)
    return ys, plan[2], plan[1]


def rope_tables(n, lc, hd):
    half = hd // 2
    t = jnp.arange(n)
    inv = jnp.power(ROPE_BASE, -jnp.arange(0, half, 2, dtype=F32) / half)
    ang_r = (t // GRID_W).astype(F32)[:, None] * inv
    ang_c = (t % GRID_W).astype(F32)[:, None] * inv
    cos = jnp.concatenate([jnp.cos(ang_r)] * 2 + [jnp.cos(ang_c)] * 2, axis=1)
    sin = jnp.concatenate([-jnp.sin(ang_r), jnp.sin(ang_r), -jnp.sin(ang_c), jnp.sin(ang_c)], axis=1)
    cos = jnp.concatenate([jnp.ones((lc, hd), F32), cos], axis=0)
    sin = jnp.concatenate([jnp.zeros((lc, hd), F32), sin], axis=0)
    scale = hd ** -0.5
    ones, zeros = jnp.ones_like(cos), jnp.zeros_like(sin)
    return jnp.stack([cos * scale, cos, ones]), jnp.stack([sin * scale, sin, zeros])


def kernel(x, c, ctx, c_ctx, ada_w, ada_b, ln_g, ln_b, mlstm_w_in, mlstm_b_gates, mlstm_norm_w, mlstm_w_out,
           attn_w_in, attn_b_in, attn_sink, attn_w_out, router_w, router_b, moe_w1, moe_w3, moe_w2):
    bsz, n, d = x.shape
    lc = ctx.shape[1]
    depth = ada_w.shape[0]
    assert depth == 2 and bsz <= 4 and n % lc == 0 and lc % ML_CHUNK == 0
    alpha = (2 * depth) ** 0.25
    tile = lc
    per = (n + lc) // tile
    per_lat = n // tile

    cc = jnp.concatenate([c, c_ctx[None, :], jnp.zeros((8 - bsz - 1, d), F32)], axis=0)
    ada = adaln(cc, ada_w, ada_b)
    ada0 = ada[0].reshape(8, 1, 6 * d)
    ada1 = ada[1].reshape(8, 1, 6 * d)
    rwt = router_w.T

    heads = mlstm_b_gates.shape[1] // 4
    dv = d // heads
    dqk = dv // 2
    nmain = 2 * heads * (dqk + dv)
    w_in_t = jnp.swapaxes(mlstm_w_in, 1, 2)
    h0, gates_pre = mod0(x, ctx, ada0, w_in_t[0, nmain:, :], mlstm_b_gates[0][None, :])
    qkvo = matmul(h0, w_in_t, n_cols=nmain, w_t=True)
    ncc, nct = lc // ML_CHUNK, (n + lc) // ML_CHUNK
    hg = mlstm_bidir(qkvo, gates_pre, mlstm_norm_w[0], bsz, heads, dqk, dv, ncc, nct)
    y = matmul(hg, mlstm_w_out, out_dtype=F32)
    x1, hffn, lg = ln_block(x, y, ada0, 2, ln_g[0, 0], ln_b[0, 0], alpha=alpha, nb=bsz, per=per, tile=tile,
                            ctx_rows=True, ctx=ctx, nxt=(ada0, 3, 4), h_dtype=F32, router_wt=rwt)
    ys, pos, pos_w = moe_sparse(hffn, lg, router_b, moe_w1, moe_w3, moe_w2, 0)
    x2, h1 = ln_block(x1, ys, ada0, 5, ln_g[0, 1], ln_b[0, 1], alpha=alpha, nb=bsz, per=per, tile=tile,
                      ctx_rows=True, x_per=per, nxt=(ada1, 0, 1), h_dtype=BF16, pos=pos, pos_w=pos_w)

    heads_a = attn_sink.shape[1]
    hd = d // heads_a
    kvh = (attn_w_in.shape[2] // hd - heads_a) // 2
    cos_t, sin_t = rope_tables(n, lc, hd)
    q_end, k_end = heads_a * hd, (heads_a + kvh) * hd
    type_of = lambda col: jnp.where(col < q_end, 0, jnp.where(col < k_end, 1, 2))
    tm1 = _pick(n + lc, (1152, 768, 384, 256, 128))
    qkv = matmul(h1, attn_w_in, bias=attn_b_in[0], rope=(cos_t, sin_t, n + lc, type_of), tm=tm1,
                 tn=_pick(kvh * hd, (512, 256, 128)))
    att = window_attention(qkv.reshape(bsz, n + lc, -1), attn_sink[0], lc, n, heads_a, kvh, hd)
    y = matmul(att.reshape(bsz * n, heads_a * hd), attn_w_out, out_dtype=F32)
    x3, hffn, lg = ln_block(x2, y, ada1, 2, ln_g[1, 0], ln_b[1, 0], alpha=alpha, nb=bsz, per=per_lat, tile=tile,
                            ctx_rows=False, x_per=per, x_off=1, nxt=(ada1, 3, 4), h_dtype=F32, router_wt=rwt)
    ys, pos, pos_w = moe_sparse(hffn, lg, router_b, moe_w1, moe_w3, moe_w2, 1)
    (x4,) = ln_block(x3, ys, ada1, 5, ln_g[1, 1], ln_b[1, 1], alpha=alpha, nb=bsz, per=per_lat, tile=tile,
                     ctx_rows=False, x_per=per_lat, pos=pos, pos_w=pos_w)
    return x4.reshape(bsz, n, d)
```

```python
import functools

import jax
import jax.numpy as jnp
from jax import lax
from jax.experimental import pallas as pl
from jax.experimental.pallas import tpu as pltpu

F32 = jnp.float32
BF16 = jnp.bfloat16
NEG_INF = float("-inf")

GRID_W = 64
ML_CHUNK = 128
AT_WINDOW = 128
AT_BLOCK = 128
ROPE_BASE = 10000.0
N_GROUPS = 4
TOP_K = 2
LN_EPS = 1e-5
RMS_EPS = 1e-6

LANES = 128
VMEM_LIMIT = 56 * 1024 * 1024
MOE_TILE = 512
HIGHEST = lax.Precision.HIGHEST
NT_DIMS = (((1,), (1,)), ((), ()))


def _cparams(*sem):
    return pltpu.CompilerParams(dimension_semantics=sem, vmem_limit_bytes=VMEM_LIMIT)


def _pick(n, candidates):
    for c in candidates:
        if n % c == 0:
            return c
    return n


def _sigmoid(x):
    return 1.0 / (1.0 + jnp.exp(-x))


def _log_sigmoid(x):
    return jnp.minimum(x, 0.0) - jnp.log1p(jnp.exp(-jnp.abs(x)))


def _adaln_kernel(c_ref, w_ref, b_ref, o_ref):
    c = c_ref[...]
    a = (c * _sigmoid(c)).astype(BF16)
    o_ref[0] = jnp.dot(a, w_ref[0].astype(BF16), preferred_element_type=F32) + b_ref[0]


def adaln(cc, ada_w, ada_b):
    depth, d, n6 = ada_w.shape
    tn = _pick(n6, (512, 256, 128))
    return pl.pallas_call(
        _adaln_kernel,
        grid=(depth, n6 // tn),
        in_specs=[
            pl.BlockSpec((8, d), lambda l, j: (0, 0)),
            pl.BlockSpec((1, d, tn), lambda l, j: (l, 0, j)),
            pl.BlockSpec((1, 1, tn), lambda l, j: (l, 0, j)),
        ],
        out_specs=pl.BlockSpec((1, 8, tn), lambda l, j: (l, 0, j)),
        out_shape=jax.ShapeDtypeStruct((depth, 8, n6), F32),
        compiler_params=_cparams("parallel", "parallel"),
        name="adaln",
    )(cc, ada_w, ada_b.reshape(depth, 1, n6))


def _mod0_kernel(x_ref, c_ref, sh_ref, sc_ref, wg_ref, bg_ref, h_ref, g_ref):
    j = pl.program_id(1)
    t = jnp.where(j == 0, c_ref[0], x_ref[0])
    h = t * (1.0 + sc_ref[0]) + sh_ref[0]
    h_ref[...] = h.astype(BF16)
    g_ref[...] = lax.dot_general(h, wg_ref[...], NT_DIMS, preferred_element_type=F32,
                                 precision=HIGHEST) + bg_ref[...]


def mod0(x, ctx, ada3, wg, bg):
    bsz, n, d = x.shape
    lc = ctx.shape[1]
    per = (n + lc) // lc
    ng = wg.shape[0]
    prow = lambda b, j: jnp.where(j == 0, bsz, b)
    return pl.pallas_call(
        _mod0_kernel,
        grid=(bsz, per),
        in_specs=[
            pl.BlockSpec((1, lc, d), lambda b, j: (b, jnp.maximum(j - 1, 0), 0)),
            pl.BlockSpec((1, lc, d), lambda b, j: (b, 0, 0)),
            pl.BlockSpec((1, 1, d), lambda b, j: (prow(b, j), 0, 0)),
            pl.BlockSpec((1, 1, d), lambda b, j: (prow(b, j), 0, 1)),
            pl.BlockSpec((ng, d), lambda b, j: (0, 0)),
            pl.BlockSpec((1, ng), lambda b, j: (0, 0)),
        ],
        out_specs=[
            pl.BlockSpec((lc, d), lambda b, j: (b * per + j, 0)),
            pl.BlockSpec((lc, ng), lambda b, j: (b * per + j, 0)),
        ],
        out_shape=[
            jax.ShapeDtypeStruct((bsz * (n + lc), d), BF16),
            jax.ShapeDtypeStruct((bsz * (n + lc), ng), F32),
        ],
        compiler_params=_cparams("parallel", "arbitrary"),
        name="mod0",
    )(x, ctx, ada3, ada3, wg, bg)


def _mm_kernel(*refs, has_bias, has_rope, tn, w_t):
    x_ref, w_ref = refs[0], refs[1]
    pos = 2
    if w_t:
        acc = lax.dot_general(x_ref[...], w_ref[...].astype(BF16), NT_DIMS, preferred_element_type=F32)
    else:
        acc = jnp.dot(x_ref[...], w_ref[...].astype(BF16), preferred_element_type=F32)
    if has_bias:
        acc = acc + refs[pos][...]
        pos += 1
    o_ref = refs[-1]
    if has_rope:
        ca = refs[pos][0]
        sb = refs[pos + 1][0]
        lane = lax.broadcasted_iota(jnp.int32, ca.shape, 1)
        first = (lane % 64) < 32
        for h in range(tn // LANES):
            y = acc[:, h * LANES:(h + 1) * LANES]
            sw = jnp.where(first, pltpu.roll(y, 96, 1), pltpu.roll(y, 32, 1))
            o_ref[:, h * LANES:(h + 1) * LANES] = (y * ca + sw * sb).astype(o_ref.dtype)
    else:
        o_ref[...] = acc.astype(o_ref.dtype)


def matmul(x, w, bias=None, rope=None, out_dtype=BF16, tm=None, tn=None, n_cols=None, w_t=False):
    m, k = x.shape
    n = n_cols or w.shape[1 if w_t else 2]
    tm = tm or _pick(m, (1152, 1024, 768, 512, 256, 128))
    tn = tn or _pick(n, (512, 256, 128))
    in_specs = [
        pl.BlockSpec((tm, k), lambda i, j: (i, 0)),
        pl.BlockSpec((None, tn, k), lambda i, j: (0, j, 0)) if w_t else
        pl.BlockSpec((None, k, tn), lambda i, j: (0, 0, j)),
    ]
    args = [x, w]
    if bias is not None:
        in_specs.append(pl.BlockSpec((1, tn), lambda i, j: (0, j)))
        args.append(bias.reshape(1, n))
    if rope is not None:
        ca, sb, period, type_of = rope
        nper = period // tm
        spec = pl.BlockSpec((1, tm, LANES), lambda i, j: (type_of(j * tn), i % nper, 0))
        in_specs += [spec, spec]
        args += [ca, sb]
    return pl.pallas_call(
        functools.partial(_mm_kernel, has_bias=bias is not None, has_rope=rope is not None, tn=tn, w_t=w_t),
        grid=(m // tm, n // tn),
        in_specs=in_specs,
        out_specs=pl.BlockSpec((tm, tn), lambda i, j: (i, j)),
        out_shape=jax.ShapeDtypeStruct((m, n), out_dtype),
        compiler_params=_cparams("parallel", "arbitrary"),
        name="matmul",
    )(*args)


def _mlstm_chunks(chains, scale):
    L = ML_CHUNK
    r = lax.broadcasted_iota(jnp.int32, (L, L), 0)
    c = lax.broadcasted_iota(jnp.int32, (L, L), 1)
    gate = []
    for q, k, v, ig_row, f_row, ig_col, f_col, st, n_row, m_prev, d in chains:
        reverse = d == 1
        lf_row = _log_sigmoid(f_row)
        lf_col = _log_sigmoid(f_col)
        keep = (c >= r) if reverse else (c <= r)
        keep_t = (r >= c) if reverse else (r <= c)
        b_col = jnp.sum(jnp.where(keep, lf_row, 0.0), axis=1, keepdims=True)
        b_row = jnp.sum(jnp.where(keep_t, lf_col, 0.0), axis=0, keepdims=True)
        logw = jnp.where(keep, b_col - b_row + ig_row, NEG_INF)
        inter = b_col + m_prev
        m_t = jnp.maximum(inter, jnp.max(logw, axis=1, keepdims=True))
        w_inter = jnp.exp(inter - m_t)
        p = jnp.exp(logw - m_t)
        b_end = b_col[0:1, :] if reverse else b_col[L - 1:L, :]
        m_new = jnp.maximum(b_end + m_prev, jnp.max(b_end - b_row + ig_row, axis=1, keepdims=True))
        decay = jnp.exp(b_end + m_prev - m_new)
        w_src = jnp.exp(b_end - b_col + ig_col - m_new)
        gate.append((m_t, w_inter, p, m_new, decay, w_src))

    qk = [lax.dot_general(ch[0], ch[1], NT_DIMS, preferred_element_type=F32) for ch in chains]
    qs = [jnp.dot(ch[0], ch[7].astype(BF16), preferred_element_type=F32) for ch in chains]
    s = [a * (g[2] * scale) for a, g in zip(qk, gate)]
    sv = [jnp.dot(a.astype(BF16), ch[2], preferred_element_type=F32) for a, ch in zip(s, chains)]
    kw = [ch[1].astype(F32) * g[5] for ch, g in zip(chains, gate)]
    kv = [jnp.dot(a.T.astype(BF16), ch[2], preferred_element_type=F32) for a, ch in zip(kw, chains)]

    out = []
    for ch, g, s_i, qs_i, sv_i, kw_i, kv_i in zip(chains, gate, s, qs, sv, kw, kv):
        q, n_row, st = ch[0], ch[8], ch[7]
        m_t, w_inter, _, m_new, decay, _ = g
        num = (w_inter * scale) * qs_i + sv_i
        qn = jnp.sum(q.astype(F32) * n_row, axis=1, keepdims=True) * scale
        den = w_inter * qn + jnp.sum(s_i, axis=1, keepdims=True)
        h = num * (1.0 / jnp.maximum(jnp.abs(den), jnp.exp(-m_t)))
        out.append((h, decay * st + kv_i, decay * n_row + jnp.sum(kw_i, axis=0, keepdims=True), m_new))
    return out


def _mlstm_kernel(q_ref, k_ref, v_ref, o_ref, gr_ref, gc_ref, nw_ref, out_ref, acc_ref, st_ref, n_ref, m_ref,
                  *, scale, ncc, nct, hpb):
    L = ML_CHUNK
    st_ref[...] = jnp.zeros_like(st_ref)
    n_ref[...] = jnp.zeros_like(n_ref)
    m_ref[...] = jnp.zeros_like(m_ref)
    acc_ref[...] = jnp.zeros_like(acc_ref)

    dqk, dv = st_ref.shape[1], st_ref.shape[2]

    def step(j, carry):
        where, chains = [], []
        for hh in range(hpb):
            for d, cj in ((0, j), (1, jnp.where(j < ncc, ncc - 1 - j, nct - 1 - (j - ncc)))):
                rows = pl.ds(pl.multiple_of(cj * L, L), L)
                slot = 2 * hh + d
                gcol = gc_ref[0, hh, cj]
                where.append((rows, hh, slot, acc_ref[rows, hh * dv:(hh + 1) * dv]))
                chains.append((q_ref[rows, hh * dqk:(hh + 1) * dqk], k_ref[rows, hh * dqk:(hh + 1) * dqk],
                               v_ref[rows, hh * dv:(hh + 1) * dv],
                               gr_ref[0, 2 * d, hh, cj], gr_ref[0, 2 * d + 1, hh, cj],
                               gcol[:, 2 * d:2 * d + 1], gcol[:, 2 * d + 1:2 * d + 2],
                               st_ref[slot], n_ref[slot], m_ref[slot], d))
        for (rows, hh, slot, acc), (h, st_new, n_new, m_new) in zip(where, _mlstm_chunks(chains, scale)):
            acc_ref[rows, hh * dv:(hh + 1) * dv] = acc + h
            st_ref[slot] = st_new
            n_ref[slot] = n_new
            m_ref[slot] = m_new
        return carry

    lax.fori_loop(0, nct, step, 0)

    def finish(cj, carry):
        rows = pl.ds(pl.multiple_of(cj * L, L), L)
        for hh in range(hpb):
            cols = slice(hh * dv, (hh + 1) * dv)
            hs = acc_ref[rows, cols]
            hn = hs * lax.rsqrt(jnp.mean(hs * hs, axis=1, keepdims=True) + RMS_EPS)
            out_ref[rows, cols] = (hn * nw_ref[:, cols]
                                   * _sigmoid(o_ref[rows, cols].astype(F32))).astype(out_ref.dtype)
        return carry

    lax.fori_loop(0, nct, finish, 0)


def mlstm_bidir(qkvo, gates_pre, norm_w, bsz, heads, dqk, dv, ncc, nct):
    t = qkvo.shape[0]
    L = ML_CHUNK
    n2 = nct * L
    hpb = 1
    assert ncc % 2 == 0 and (nct - ncc) % 2 == 0
    g4 = gates_pre.reshape(bsz, nct, L, 4, heads)
    g_rows = jnp.transpose(g4, (0, 3, 4, 1, 2)).reshape(bsz, 4, heads, nct, 1, L)
    g_cols = jnp.transpose(g4, (0, 4, 1, 2, 3))
    koff = heads // hpb
    voff = 2 * heads * dqk // (dv * hpb)
    ooff = voff + heads // hpb
    return pl.pallas_call(
        functools.partial(_mlstm_kernel, scale=dqk ** -0.5, ncc=ncc, nct=nct, hpb=hpb),
        grid=(bsz, heads // hpb),
        in_specs=[
            pl.BlockSpec((n2, hpb * dqk), lambda b, h: (b, h)),
            pl.BlockSpec((n2, hpb * dqk), lambda b, h: (b, koff + h)),
            pl.BlockSpec((n2, hpb * dv), lambda b, h: (b, voff + h)),
            pl.BlockSpec((n2, hpb * dv), lambda b, h: (b, ooff + h)),
            pl.BlockSpec((1, 4, hpb, nct, 1, L), lambda b, h: (b, 0, h, 0, 0, 0)),
            pl.BlockSpec((1, hpb, nct, L, 4), lambda b, h: (b, h, 0, 0, 0)),
            pl.BlockSpec((1, hpb * dv), lambda b, h: (0, h)),
        ],
        out_specs=pl.BlockSpec((n2, hpb * dv), lambda b, h: (b, h)),
        out_shape=jax.ShapeDtypeStruct((t, heads * dv), BF16),
        scratch_shapes=[pltpu.VMEM((n2, hpb * dv), F32), pltpu.VMEM((2 * hpb, dqk, dv), F32),
                        pltpu.VMEM((2 * hpb, 1, dqk), F32), pltpu.VMEM((2 * hpb, 1, 1), F32)],
        compiler_params=_cparams("parallel", "parallel"),
        name="mlstm",
    )(qkvo, qkvo, qkvo, qkvo, g_rows, g_cols, norm_w.reshape(1, heads * dv))


def _attn_kernel(sink_ref, q_ref, k_ref, v_ref, o_ref, bias_ref, *, lc, nb, group, hd):
    kv = pl.program_id(1)
    blk = AT_BLOCK
    rows = group * blk
    kc = k_ref[0, 0:lc, :]
    vc = v_ref[0, 0:lc, :]
    rid = lax.broadcasted_iota(jnp.int32, (rows, 1), 0)
    sink_col = jnp.zeros((rows, 1), F32)
    for g in range(group):
        sink_col = jnp.where(rid // blk == g, sink_ref[kv * group + g], sink_col)
    qrow = lax.broadcasted_iota(jnp.int32, (rows, 3 * blk), 0) % blk
    kcol = lax.broadcasted_iota(jnp.int32, (rows, 3 * blk), 1)
    for back in range(3):
        rel = kcol - back * blk - qrow
        bias_ref[back] = jnp.where(jnp.abs(rel) <= AT_WINDOW, 0.0, NEG_INF)

    def body(j, carry):
        ws = jnp.clip(j - 1, 0, nb - 3)
        qs = pl.multiple_of(lc + j * blk, blk)
        ks = pl.multiple_of(lc + ws * blk, blk)
        q4 = q_ref[0, pl.ds(qs, blk), :]
        qq = jnp.concatenate([q4[:, g * hd:(g + 1) * hd] for g in range(group)], axis=0)
        kw = k_ref[0, pl.ds(ks, 3 * blk), :]
        vw = v_ref[0, pl.ds(ks, 3 * blk), :]
        s_loc = lax.dot_general(qq, kw, NT_DIMS, preferred_element_type=F32)
        s_ctx = lax.dot_general(qq, kc, NT_DIMS, preferred_element_type=F32)
        s_loc = s_loc + bias_ref[j - ws]
        m = jnp.maximum(jnp.maximum(jnp.max(s_loc, axis=1, keepdims=True),
                                    jnp.max(s_ctx, axis=1, keepdims=True)), sink_col)
        p_loc = jnp.exp(s_loc - m)
        p_ctx = jnp.exp(s_ctx - m)
        den = (jnp.sum(p_loc, axis=1, keepdims=True) + jnp.sum(p_ctx, axis=1, keepdims=True)
               + jnp.exp(sink_col - m))
        o = (jnp.dot(p_loc.astype(BF16), vw, preferred_element_type=F32)
             + jnp.dot(p_ctx.astype(BF16), vc, preferred_element_type=F32)) / den
        os = pl.multiple_of(j * blk, blk)
        for g in range(group):
            o_ref[0, pl.ds(os, blk), g * hd:(g + 1) * hd] = o[g * blk:(g + 1) * blk].astype(o_ref.dtype)
        return carry

    lax.fori_loop(0, nb, body, 0, unroll=2)


def window_attention(qkv3, sink, lc, n, heads, kv_heads, hd):
    bsz = qkv3.shape[0]
    group = heads // kv_heads
    nb = n // AT_BLOCK
    assert nb >= 3
    koff = heads
    voff = heads + kv_heads
    return pl.pallas_call(
        functools.partial(_attn_kernel, lc=lc, nb=nb, group=group, hd=hd),
        grid=(bsz, kv_heads),
        in_specs=[
            pl.BlockSpec(memory_space=pltpu.SMEM),
            pl.BlockSpec((1, lc + n, group * hd), lambda b, g: (b, 0, g)),
            pl.BlockSpec((1, lc + n, hd), lambda b, g: (b, 0, koff + g)),
            pl.BlockSpec((1, lc + n, hd), lambda b, g: (b, 0, voff + g)),
        ],
        out_specs=pl.BlockSpec((1, n, group * hd), lambda b, g: (b, 0, g)),
        out_shape=jax.ShapeDtypeStruct((bsz, n, heads * hd), BF16),
        scratch_shapes=[pltpu.VMEM((3, group * AT_BLOCK, 3 * AT_BLOCK), F32)],
        compiler_params=_cparams("parallel", "parallel"),
        name="window_attention",
    )(sink, qkv3, qkv3, qkv3)


def _ln_kernel(*refs, dual, gather, tile, has_h, has_router, alpha):
    refs = list(refs)
    if gather:
        pos_ref = refs.pop(0)
    x_ref = refs.pop(0)
    c_ref = refs.pop(0) if dual else None
    y_ref = refs.pop(0)
    gw_ref = refs.pop(0) if gather else None
    gate_ref, lng_ref, lnb_ref = refs.pop(0), refs.pop(0), refs.pop(0)
    if has_h:
        sh_ref, sc_ref = refs.pop(0), refs.pop(0)
    if has_router:
        rw_ref = refs.pop(0)
    xo_ref = refs.pop(0)
    if has_h:
        ho_ref = refs.pop(0)
    if has_router:
        lg_ref = refs.pop(0)

    if gather:
        buf_ref, sems = refs.pop(0), refs.pop(0)
        lin = pl.program_id(0) * pl.num_programs(1) + pl.program_id(1)
        ntiles = pl.num_programs(0) * pl.num_programs(1)
        ntok = pos_ref.shape[0] // 2

        def issue(t):
            slot = t % 2

            def start(r, carry):
                for k in range(2):
                    src = pos_ref[k * ntok + t * tile + r]
                    pltpu.make_async_copy(y_ref.at[pl.ds(src, 1)], buf_ref.at[slot, k, pl.ds(r, 1)],
                                          sems.at[slot]).start()
                return carry

            lax.fori_loop(0, tile, start, 0, unroll=4)

        @pl.when(lin == 0)
        def _():
            issue(lin)

        @pl.when(lin + 1 < ntiles)
        def _():
            issue(lin + 1)

        slot = lin % 2
        for k in range(2):
            pltpu.make_async_copy(y_ref.at[pl.ds(0, tile)], buf_ref.at[slot, k], sems.at[slot]).wait()
        gw = gw_ref[...]
        y = gw[:, 0:1] * buf_ref[slot, 0] + gw[:, 1:2] * buf_ref[slot, 1]
    else:
        y = y_ref[...].astype(F32)

    if dual:
        x = jnp.where(pl.program_id(1) == 0, c_ref[0], x_ref[0])
    else:
        x = x_ref[...]
    z = alpha * x + gate_ref[0] * y
    mu = jnp.mean(z, axis=1, keepdims=True)
    zc = z - mu
    var = jnp.mean(zc * zc, axis=1, keepdims=True)
    xn = zc * lax.rsqrt(var + LN_EPS) * lng_ref[...] + lnb_ref[...]
    xo_ref[...] = xn
    if has_h:
        h = xn * (1.0 + sc_ref[0]) + sh_ref[0]
        ho_ref[...] = h.astype(ho_ref.dtype).reshape(ho_ref.shape)
        if has_router:
            lg_ref[...] = lax.dot_general(rw_ref[...], h, NT_DIMS, preferred_element_type=F32,
                                          precision=HIGHEST)


def ln_block(x, y, ada3, gate_col, lng, lnb, *, alpha, nb, per, tile, ctx_rows, ctx=None, x_per=None,
             x_off=0, nxt=None, h_dtype=None, h_rows3d=False, router_wt=None, pos=None, pos_w=None):
    d = x.shape[-1]
    dual = ctx is not None
    gather = pos is not None
    rows = nb * per * tile
    prow = (lambda b, j: jnp.where(j == 0, nb, b)) if ctx_rows else (lambda b, j: b)

    def im(f):
        return lambda b, j, *_: f(b, j)

    in_specs, args = [], []
    if dual:
        in_specs.append(pl.BlockSpec((1, tile, d), im(lambda b, j: (b, jnp.maximum(j - 1, 0), 0))))
        in_specs.append(pl.BlockSpec((1, tile, d), im(lambda b, j: (b, 0, 0))))
        args += [x, ctx]
    else:
        in_specs.append(pl.BlockSpec((tile, d), im(lambda b, j: (b * x_per + x_off + j, 0))))
        args.append(x)
    if gather:
        in_specs.append(pl.BlockSpec(memory_space=pl.ANY))
        in_specs.append(pl.BlockSpec((tile, 2), im(lambda b, j: (b * per + j, 0))))
        args += [y, pos_w]
    else:
        in_specs.append(pl.BlockSpec((tile, d), im(lambda b, j: (b * per + j, 0))))
        args.append(y)
    in_specs.append(pl.BlockSpec((1, 1, d), im(lambda b, j: (prow(b, j), 0, gate_col))))
    in_specs.append(pl.BlockSpec((1, d), im(lambda b, j: (0, 0))))
    in_specs.append(pl.BlockSpec((1, d), im(lambda b, j: (0, 0))))
    args += [ada3, lng.reshape(1, d), lnb.reshape(1, d)]
    if nxt is not None:
        ada_n, shc, scc = nxt
        in_specs.append(pl.BlockSpec((1, 1, d), im(lambda b, j: (prow(b, j), 0, shc))))
        in_specs.append(pl.BlockSpec((1, 1, d), im(lambda b, j: (prow(b, j), 0, scc))))
        args += [ada_n, ada_n]
    if router_wt is not None:
        ne = router_wt.shape[0]
        in_specs.append(pl.BlockSpec((ne, d), im(lambda b, j: (0, 0))))
        args.append(router_wt)

    out_specs = [pl.BlockSpec((tile, d), im(lambda b, j: (b * per + j, 0)))]
    out_shape = [jax.ShapeDtypeStruct((rows, d), F32)]
    if nxt is not None and h_rows3d:
        out_specs.append(pl.BlockSpec((tile, d // LANES, LANES), im(lambda b, j: (b * per + j, 0, 0))))
        out_shape.append(jax.ShapeDtypeStruct((rows, d // LANES, LANES), h_dtype))
    elif nxt is not None:
        out_specs.append(pl.BlockSpec((tile, d), im(lambda b, j: (b * per + j, 0))))
        out_shape.append(jax.ShapeDtypeStruct((rows, d), h_dtype))
    if router_wt is not None:
        out_specs.append(pl.BlockSpec((ne, tile), im(lambda b, j: (0, b * per + j))))
        out_shape.append(jax.ShapeDtypeStruct((ne, rows), F32))

    scratch = [pltpu.VMEM((2, 2, tile, d), F32), pltpu.SemaphoreType.DMA((2,))] if gather else []
    kern = functools.partial(_ln_kernel, dual=dual, gather=gather, tile=tile, has_h=nxt is not None,
                             has_router=router_wt is not None, alpha=alpha)
    call = pl.pallas_call(
        kern,
        grid_spec=pltpu.PrefetchScalarGridSpec(
            num_scalar_prefetch=1 if gather else 0,
            grid=(nb, per),
            in_specs=in_specs,
            out_specs=out_specs,
            scratch_shapes=scratch,
        ),
        out_shape=out_shape,
        compiler_params=_cparams("arbitrary", "arbitrary"),
        name="ln_gather" if gather else "ln_block",
    )
    return call(*(([pos] if gather else []) + args))


def _router_kernel(b_ref, l_ref, g_ref, m_ref, *, ne):
    epg = ne // N_GROUPS
    logits = l_ref[...]
    s = [_sigmoid(logits[e:e + 1, :]) for e in range(ne)]
    sel = [s[e] + b_ref[e] for e in range(ne)]
    gscore = []
    for g in range(N_GROUPS):
        v = sel[g * epg:(g + 1) * epg]
        best = None
        for i in range(epg):
            for j in range(i + 1, epg):
                pair = v[i] + v[j]
                best = pair if best is None else jnp.maximum(best, pair)
        gscore.append(best)
    chosen = []
    for g in range(N_GROUPS):
        ok = None
        for o in range(N_GROUPS):
            if o == g:
                continue
            t = (gscore[g] > gscore[o]) if o < g else (gscore[g] >= gscore[o])
            ok = t if ok is None else jnp.logical_and(ok, t)
        chosen.append(ok)
    mask = []
    for g in range(N_GROUPS):
        v = sel[g * epg:(g + 1) * epg]
        for i in range(epg):
            rank = jnp.zeros_like(v[i])
            for j in range(epg):
                if j == i:
                    continue
                ahead = (v[j] >= v[i]) if j < i else (v[j] > v[i])
                rank = rank + jnp.where(ahead, 1.0, 0.0)
            mask.append(jnp.logical_and(chosen[g], rank < TOP_K))
    wsum = jnp.zeros_like(s[0])
    for e in range(ne):
        wsum = wsum + jnp.where(mask[e], s[e], 0.0)
    for e in range(ne):
        g_ref[e:e + 1, :] = jnp.where(mask[e], s[e] / wsum, 0.0)
        m_ref[e:e + 1, :] = jnp.where(mask[e], 1, 0).astype(jnp.int32)


def router(logits_t, router_b):
    ne, t = logits_t.shape
    tl = _pick(t, (1024, 512, 256, 128))
    spec = pl.BlockSpec((ne, tl), lambda i: (0, i))
    return pl.pallas_call(
        functools.partial(_router_kernel, ne=ne),
        grid=(t // tl,),
        in_specs=[pl.BlockSpec(memory_space=pltpu.SMEM), spec],
        out_specs=[spec, spec],
        out_shape=[jax.ShapeDtypeStruct((ne, t), F32), jax.ShapeDtypeStruct((ne, t), jnp.int32)],
        compiler_params=_cparams("parallel"),
        name="router",
    )(router_b, logits_t)


def dispatch_plan(gates, mask, tm):
    ne, t = mask.shape
    n_tiles = (TOP_K * t) // tm + ne
    rows = n_tiles * tm
    sel = mask > 0
    counts = jnp.sum(mask, axis=1)
    ptiles = (counts + tm - 1) // tm
    tile_end = jnp.cumsum(ptiles)
    off = (tile_end - ptiles) * tm
    dest = off[:, None] + jnp.cumsum(mask, axis=1) - mask
    pos_lo = jnp.min(jnp.where(sel, dest, rows), axis=0)
    pos_hi = jnp.max(jnp.where(sel, dest, -1), axis=0)
    g_lo = jnp.sum(jnp.where(sel & (dest == pos_lo[None, :]), gates, 0.0), axis=0)
    g_hi = jnp.sum(jnp.where(sel & (dest == pos_hi[None, :]), gates, 0.0), axis=0)
    pos = jnp.concatenate([pos_lo, pos_hi]).astype(jnp.int32)
    tok = jnp.arange(t, dtype=jnp.int32)
    row_token = jnp.zeros((rows,), jnp.int32).at[pos].set(jnp.concatenate([tok, tok]), unique_indices=True)
    n_valid = tile_end[-1]
    tid = jnp.minimum(jnp.arange(n_tiles, dtype=jnp.int32), n_valid - 1)
    tile_expert = jnp.minimum(jnp.sum(tile_end[None, :] <= tid[:, None], axis=1), ne - 1).astype(jnp.int32)
    prev = jnp.concatenate([jnp.full((1,), -1, jnp.int32), tile_expert[:-1]])
    tile_first = (tile_expert != prev).astype(jnp.int32)
    nv = jnp.reshape(n_valid, (1,)).astype(jnp.int32)
    return row_token, jnp.stack([g_lo, g_hi], axis=1), pos, tid, tile_expert, tile_first, nv


def _gather_kernel(tok_ref, nv_ref, h_ref, o_ref, buf_ref, sems, *, tm):
    i = pl.program_id(0)
    nv = nv_ref[0]

    def issue(tile):
        slot = tile % 2

        def start(r, carry):
            pltpu.make_async_copy(h_ref.at[tok_ref[tile * tm + r]], buf_ref.at[slot, r], sems.at[slot]).start()
            return carry

        lax.fori_loop(0, tm, start, 0, unroll=8)

    @pl.when(i == 0)
    def _():
        issue(i)

    @pl.when(i + 1 < nv)
    def _():
        issue(i + 1)

    @pl.when(i < nv)
    def _():
        slot = i % 2
        pltpu.make_async_copy(h_ref.at[pl.ds(0, tm)], buf_ref.at[slot], sems.at[slot]).wait()
        o_ref[...] = buf_ref[slot].reshape(o_ref.shape).astype(o_ref.dtype)

    @pl.when(i >= nv)
    def _():
        o_ref[...] = jnp.zeros_like(o_ref)


def gather_rows(h, row_token, nv, tm):
    d = h.shape[1] * h.shape[2]
    rows = row_token.shape[0]
    return pl.pallas_call(
        functools.partial(_gather_kernel, tm=tm),
        grid_spec=pltpu.PrefetchScalarGridSpec(
            num_scalar_prefetch=2,
            grid=(rows // tm,),
            in_specs=[pl.BlockSpec(memory_space=pl.ANY)],
            out_specs=pl.BlockSpec((tm, d), lambda i, tok, nvr: (i, 0)),
            scratch_shapes=[pltpu.VMEM((2, tm) + h.shape[1:], F32), pltpu.SemaphoreType.DMA((2,))],
        ),
        out_shape=jax.ShapeDtypeStruct((rows, d), BF16),
        compiler_params=_cparams("arbitrary"),
        name="moe_gather",
    )(row_token, nv, h)


def _ffn_a_kernel(tid_ref, te_ref, tf_ref, nv_ref, x_ref, w1_ref, w3_ref, o_ref, w1b, w3b):
    i = pl.program_id(1)

    @pl.when(i < nv_ref[0])
    def _():
        @pl.when(tf_ref[i] == 1)
        def _():
            w1b[...] = w1_ref[0].astype(BF16)
            w3b[...] = w3_ref[0].astype(BF16)

        x = x_ref[...]
        a = jnp.dot(x, w1b[...], preferred_element_type=F32)
        b = jnp.dot(x, w3b[...], preferred_element_type=F32)
        o_ref[...] = (a * _sigmoid(a) * b).astype(o_ref.dtype)

    @pl.when(i >= nv_ref[0])
    def _():
        o_ref[...] = jnp.zeros_like(o_ref)


def ffn_stage_a(xs, w1, w3, layer, plan, tm):
    _, _, _, tid, te, tf, nv = plan
    rows, d = xs.shape
    ff = w1.shape[3]
    tfc = _pick(ff, (512, 256, 128))
    wspec = pl.BlockSpec((None, 1, d, tfc), lambda f, i, tid, te, tf, nv: (layer, te[i], 0, f))
    return pl.pallas_call(
        _ffn_a_kernel,
        grid_spec=pltpu.PrefetchScalarGridSpec(
            num_scalar_prefetch=4,
            grid=(ff // tfc, rows // tm),
            in_specs=[pl.BlockSpec((tm, d), lambda f, i, tid, te, tf, nv: (tid[i], 0)), wspec, wspec],
            out_specs=pl.BlockSpec((tm, tfc), lambda f, i, tid, te, tf, nv: (i, f)),
            scratch_shapes=[pltpu.VMEM((d, tfc), BF16), pltpu.VMEM((d, tfc), BF16)],
        ),
        out_shape=jax.ShapeDtypeStruct((rows, ff), BF16),
        compiler_params=_cparams("arbitrary", "arbitrary"),
        name="moe_ffn_a",
    )(tid, te, tf, nv, xs, w1, w3)


def _ffn_b_kernel(tid_ref, te_ref, tf_ref, nv_ref, h_ref, w2_ref, o_ref, w2b):
    i = pl.program_id(1)

    @pl.when(i < nv_ref[0])
    def _():
        @pl.when(tf_ref[i] == 1)
        def _():
            w2b[...] = w2_ref[0].astype(BF16)

        o_ref[...] = jnp.dot(h_ref[...], w2b[...], preferred_element_type=F32)

    @pl.when(i >= nv_ref[0])
    def _():
        o_ref[...] = jnp.zeros_like(o_ref)


def ffn_stage_b(hs, w2, layer, plan, tm):
    _, _, _, tid, te, tf, nv = plan
    rows, ff = hs.shape
    d = w2.shape[3]
    tn = _pick(d, (2048, 1024, 512, 256, 128))
    return pl.pallas_call(
        _ffn_b_kernel,
        grid_spec=pltpu.PrefetchScalarGridSpec(
            num_scalar_prefetch=4,
            grid=(d // tn, rows // tm),
            in_specs=[
                pl.BlockSpec((tm, ff), lambda n, i, tid, te, tf, nv: (tid[i], 0)),
                pl.BlockSpec((None, 1, ff, tn), lambda n, i, tid, te, tf, nv: (layer, te[i], 0, n)),
            ],
            out_specs=pl.BlockSpec((tm, tn), lambda n, i, tid, te, tf, nv: (i, n)),
            scratch_shapes=[pltpu.VMEM((ff, tn), BF16)],
        ),
        out_shape=jax.ShapeDtypeStruct((rows, d), F32),
        compiler_params=_cparams("arbitrary", "arbitrary"),
        name="moe_ffn_b",
    )(tid, te, tf, nv, hs, w2)


def moe_sparse(h, logits_t, router_b, w1, w3, w2, layer):
    gates, mask = router(logits_t, router_b)
    plan = dispatch_plan(gates, mask, MOE_TILE)
    xs = gather_rows(h, plan[0], plan[6], MOE_TILE)
    hs = ffn_stage_a(xs, w1, w3, layer, plan, MOE_TILE)
    ys = ffn_stage_b(hs, w2, layer, plan, MOE_TILE)
    return ys, plan[2], plan[1]


def rope_tables(n, lc, hd):
    half = hd // 2
    t = jnp.arange(n)
    inv = jnp.power(ROPE_BASE, -jnp.arange(0, half, 2, dtype=F32) / half)
    ang_r = (t // GRID_W).astype(F32)[:, None] * inv
    ang_c = (t % GRID_W).astype(F32)[:, None] * inv
    cos = jnp.concatenate([jnp.cos(ang_r)] * 2 + [jnp.cos(ang_c)] * 2, axis=1)
    sin = jnp.concatenate([-jnp.sin(ang_r), jnp.sin(ang_r), -jnp.sin(ang_c), jnp.sin(ang_c)], axis=1)
    cos = jnp.concatenate([jnp.ones((lc, hd), F32), cos], axis=0)
    sin = jnp.concatenate([jnp.zeros((lc, hd), F32), sin], axis=0)
    scale = hd ** -0.5
    ones, zeros = jnp.ones_like(cos), jnp.zeros_like(sin)
    return jnp.stack([cos * scale, cos, ones]), jnp.stack([sin * scale, sin, zeros])


def kernel(x, c, ctx, c_ctx, ada_w, ada_b, ln_g, ln_b, mlstm_w_in, mlstm_b_gates, mlstm_norm_w, mlstm_w_out,
           attn_w_in, attn_b_in, attn_sink, attn_w_out, router_w, router_b, moe_w1, moe_w3, moe_w2):
    bsz, n, d = x.shape
    lc = ctx.shape[1]
    depth = ada_w.shape[0]
    assert depth == 2 and bsz <= 4 and n % lc == 0 and lc % ML_CHUNK == 0
    alpha = (2 * depth) ** 0.25
    tile = lc
    per = (n + lc) // tile
    per_lat = n // tile

    cc = jnp.concatenate([c, c_ctx[None, :], jnp.zeros((8 - bsz - 1, d), F32)], axis=0)
    ada = adaln(cc, ada_w, ada_b)
    ada0 = ada[0].reshape(8, 1, 6 * d)
    ada1 = ada[1].reshape(8, 1, 6 * d)
    rwt = router_w.T

    heads = mlstm_b_gates.shape[1] // 4
    dv = d // heads
    dqk = dv // 2
    nmain = 2 * heads * (dqk + dv)
    w_in_t = jnp.swapaxes(mlstm_w_in, 1, 2)
    h0, gates_pre = mod0(x, ctx, ada0, w_in_t[0, nmain:, :], mlstm_b_gates[0][None, :])
    qkvo = matmul(h0, w_in_t, n_cols=nmain, w_t=True)
    ncc, nct = lc // ML_CHUNK, (n + lc) // ML_CHUNK
    hg = mlstm_bidir(qkvo, gates_pre, mlstm_norm_w[0], bsz, heads, dqk, dv, ncc, nct)
    y = matmul(hg, mlstm_w_out, out_dtype=F32)
    x1, hffn, lg = ln_block(x, y, ada0, 2, ln_g[0, 0], ln_b[0, 0], alpha=alpha, nb=bsz, per=per, tile=tile,
                            ctx_rows=True, ctx=ctx, nxt=(ada0, 3, 4), h_dtype=F32, h_rows3d=True, router_wt=rwt)
    ys, pos, pos_w = moe_sparse(hffn, lg, router_b, moe_w1, moe_w3, moe_w2, 0)
    x2, h1 = ln_block(x1, ys, ada0, 5, ln_g[0, 1], ln_b[0, 1], alpha=alpha, nb=bsz, per=per, tile=tile,
                      ctx_rows=True, x_per=per, nxt=(ada1, 0, 1), h_dtype=BF16, pos=pos, pos_w=pos_w)

    heads_a = attn_sink.shape[1]
    hd = d // heads_a
    kvh = (attn_w_in.shape[2] // hd - heads_a) // 2
    cos_t, sin_t = rope_tables(n, lc, hd)
    q_end, k_end = heads_a * hd, (heads_a + kvh) * hd
    type_of = lambda col: jnp.where(col < q_end, 0, jnp.where(col < k_end, 1, 2))
    tm1 = _pick(n + lc, (1152, 768, 384, 256, 128))
    qkv = matmul(h1, attn_w_in, bias=attn_b_in[0], rope=(cos_t, sin_t, n + lc, type_of), tm=tm1,
                 tn=_pick(kvh * hd, (512, 256, 128)))
    att = window_attention(qkv.reshape(bsz, n + lc, -1), attn_sink[0], lc, n, heads_a, kvh, hd)
    y = matmul(att.reshape(bsz * n, heads_a * hd), attn_w_out, out_dtype=F32)
    x3, hffn, lg = ln_block(x2, y, ada1, 2, ln_g[1, 0], ln_b[1, 0], alpha=alpha, nb=bsz, per=per_lat, tile=tile,
                            ctx_rows=False, x_per=per, x_off=1, nxt=(ada1, 3, 4), h_dtype=F32, h_rows3d=True,
                            router_wt=rwt)
    ys, pos, pos_w = moe_sparse(hffn, lg, router_b, moe_w1, moe_w3, moe_w2, 1)
    (x4,) = ln_block(x3, ys, ada1, 5, ln_g[1, 1], ln_b[1, 1], alpha=alpha, nb=bsz, per=per_lat, tile=tile,
                     ctx_rows=False, x_per=per_lat, pos=pos, pos_w=pos_w)
    return x4.reshape(bsz, n, d)
```

```python
import functools

import jax
import jax.numpy as jnp
from jax import lax
from jax.experimental import pallas as pl
from jax.experimental.pallas import tpu as pltpu

F32 = jnp.float32
BF16 = jnp.bfloat16
NEG_INF = float("-inf")

GRID_W = 64
ML_CHUNK = 128
AT_WINDOW = 128
AT_BLOCK = 128
ROPE_BASE = 10000.0
N_GROUPS = 4
TOP_K = 2
LN_EPS = 1e-5
RMS_EPS = 1e-6

LANES = 128
VMEM_LIMIT = 56 * 1024 * 1024
MOE_TILE = 512
HIGHEST = lax.Precision.HIGHEST
NT_DIMS = (((1,), (1,)), ((), ()))


def _cparams(*sem):
    return pltpu.CompilerParams(dimension_semantics=sem, vmem_limit_bytes=VMEM_LIMIT)


def _pick(n, candidates):
    for c in candidates:
        if n % c == 0:
            return c
    return n


def _sigmoid(x):
    return 1.0 / (1.0 + jnp.exp(-x))


def _log_sigmoid(x):
    return jnp.minimum(x, 0.0) - jnp.log1p(jnp.exp(-jnp.abs(x)))


def _adaln_kernel(c_ref, w_ref, b_ref, o_ref):
    c = c_ref[...]
    a = (c * _sigmoid(c)).astype(BF16)
    o_ref[0] = jnp.dot(a, w_ref[0].astype(BF16), preferred_element_type=F32) + b_ref[0]


def adaln(cc, ada_w, ada_b):
    depth, d, n6 = ada_w.shape
    tn = _pick(n6, (512, 256, 128))
    return pl.pallas_call(
        _adaln_kernel,
        grid=(depth, n6 // tn),
        in_specs=[
            pl.BlockSpec((8, d), lambda l, j: (0, 0)),
            pl.BlockSpec((1, d, tn), lambda l, j: (l, 0, j)),
            pl.BlockSpec((1, 1, tn), lambda l, j: (l, 0, j)),
        ],
        out_specs=pl.BlockSpec((1, 8, tn), lambda l, j: (l, 0, j)),
        out_shape=jax.ShapeDtypeStruct((depth, 8, n6), F32),
        compiler_params=_cparams("parallel", "parallel"),
        name="adaln",
    )(cc, ada_w, ada_b.reshape(depth, 1, n6))


def _mod0_kernel(x_ref, c_ref, sh_ref, sc_ref, wg_ref, bg_ref, h_ref, g_ref):
    j = pl.program_id(1)
    t = jnp.where(j == 0, c_ref[0], x_ref[0])
    h = t * (1.0 + sc_ref[0]) + sh_ref[0]
    h_ref[...] = h.astype(BF16)
    g_ref[...] = lax.dot_general(h, wg_ref[...], NT_DIMS, preferred_element_type=F32,
                                 precision=HIGHEST) + bg_ref[...]


def mod0(x, ctx, ada3, wg, bg):
    bsz, n, d = x.shape
    lc = ctx.shape[1]
    per = (n + lc) // lc
    ng = wg.shape[0]
    prow = lambda b, j: jnp.where(j == 0, bsz, b)
    return pl.pallas_call(
        _mod0_kernel,
        grid=(bsz, per),
        in_specs=[
            pl.BlockSpec((1, lc, d), lambda b, j: (b, jnp.maximum(j - 1, 0), 0)),
            pl.BlockSpec((1, lc, d), lambda b, j: (b, 0, 0)),
            pl.BlockSpec((1, 1, d), lambda b, j: (prow(b, j), 0, 0)),
            pl.BlockSpec((1, 1, d), lambda b, j: (prow(b, j), 0, 1)),
            pl.BlockSpec((ng, d), lambda b, j: (0, 0)),
            pl.BlockSpec((1, ng), lambda b, j: (0, 0)),
        ],
        out_specs=[
            pl.BlockSpec((lc, d), lambda b, j: (b * per + j, 0)),
            pl.BlockSpec((lc, ng), lambda b, j: (b * per + j, 0)),
        ],
        out_shape=[
            jax.ShapeDtypeStruct((bsz * (n + lc), d), BF16),
            jax.ShapeDtypeStruct((bsz * (n + lc), ng), F32),
        ],
        compiler_params=_cparams("parallel", "arbitrary"),
        name="mod0",
    )(x, ctx, ada3, ada3, wg, bg)


def _mm_kernel(*refs, has_bias, has_rope, tn, w_t):
    x_ref, w_ref = refs[0], refs[1]
    pos = 2
    if w_t:
        acc = lax.dot_general(x_ref[...], w_ref[...].astype(BF16), NT_DIMS, preferred_element_type=F32)
    else:
        acc = jnp.dot(x_ref[...], w_ref[...].astype(BF16), preferred_element_type=F32)
    if has_bias:
        acc = acc + refs[pos][...]
        pos += 1
    o_ref = refs[-1]
    if has_rope:
        ca = refs[pos][0]
        sb = refs[pos + 1][0]
        lane = lax.broadcasted_iota(jnp.int32, ca.shape, 1)
        first = (lane % 64) < 32
        for h in range(tn // LANES):
            y = acc[:, h * LANES:(h + 1) * LANES]
            sw = jnp.where(first, pltpu.roll(y, 96, 1), pltpu.roll(y, 32, 1))
            o_ref[:, h * LANES:(h + 1) * LANES] = (y * ca + sw * sb).astype(o_ref.dtype)
    else:
        o_ref[...] = acc.astype(o_ref.dtype)


def matmul(x, w, bias=None, rope=None, out_dtype=BF16, tm=None, tn=None, n_cols=None, w_t=False):
    m, k = x.shape
    n = n_cols or w.shape[1 if w_t else 2]
    tm = tm or _pick(m, (1152, 1024, 768, 512, 256, 128))
    tn = tn or _pick(n, (512, 256, 128))
    in_specs = [
        pl.BlockSpec((tm, k), lambda i, j: (i, 0)),
        pl.BlockSpec((None, tn, k), lambda i, j: (0, j, 0)) if w_t else
        pl.BlockSpec((None, k, tn), lambda i, j: (0, 0, j)),
    ]
    args = [x, w]
    if bias is not None:
        in_specs.append(pl.BlockSpec((1, tn), lambda i, j: (0, j)))
        args.append(bias.reshape(1, n))
    if rope is not None:
        ca, sb, period, type_of = rope
        nper = period // tm
        spec = pl.BlockSpec((1, tm, LANES), lambda i, j: (type_of(j * tn), i % nper, 0))
        in_specs += [spec, spec]
        args += [ca, sb]
    return pl.pallas_call(
        functools.partial(_mm_kernel, has_bias=bias is not None, has_rope=rope is not None, tn=tn, w_t=w_t),
        grid=(m // tm, n // tn),
        in_specs=in_specs,
        out_specs=pl.BlockSpec((tm, tn), lambda i, j: (i, j)),
        out_shape=jax.ShapeDtypeStruct((m, n), out_dtype),
        compiler_params=_cparams("parallel", "arbitrary"),
        name="matmul",
    )(*args)


def _mlstm_chunks(chains, scale):
    L = ML_CHUNK
    r = lax.broadcasted_iota(jnp.int32, (L, L), 0)
    c = lax.broadcasted_iota(jnp.int32, (L, L), 1)
    gate = []
    for q, k, v, ig_row, f_row, ig_col, f_col, st, n_row, m_prev, d in chains:
        reverse = d == 1
        lf_row = _log_sigmoid(f_row)
        lf_col = _log_sigmoid(f_col)
        keep = (c >= r) if reverse else (c <= r)
        keep_t = (r >= c) if reverse else (r <= c)
        b_col = jnp.sum(jnp.where(keep, lf_row, 0.0), axis=1, keepdims=True)
        b_row = jnp.sum(jnp.where(keep_t, lf_col, 0.0), axis=0, keepdims=True)
        logw = jnp.where(keep, b_col - b_row + ig_row, NEG_INF)
        inter = b_col + m_prev
        m_t = jnp.maximum(inter, jnp.max(logw, axis=1, keepdims=True))
        w_inter = jnp.exp(inter - m_t)
        p = jnp.exp(logw - m_t)
        b_end = b_col[0:1, :] if reverse else b_col[L - 1:L, :]
        m_new = jnp.maximum(b_end + m_prev, jnp.max(b_end - b_row + ig_row, axis=1, keepdims=True))
        decay = jnp.exp(b_end + m_prev - m_new)
        w_src = jnp.exp(b_end - b_col + ig_col - m_new)
        gate.append((m_t, w_inter, p, m_new, decay, w_src))

    qk = [lax.dot_general(ch[0], ch[1], NT_DIMS, preferred_element_type=F32) for ch in chains]
    qs = [jnp.dot(ch[0], ch[7].astype(BF16), preferred_element_type=F32) for ch in chains]
    s = [a * (g[2] * scale) for a, g in zip(qk, gate)]
    sv = [jnp.dot(a.astype(BF16), ch[2], preferred_element_type=F32) for a, ch in zip(s, chains)]
    kw = [ch[1].astype(F32) * g[5] for ch, g in zip(chains, gate)]
    kv = [jnp.dot(a.T.astype(BF16), ch[2], preferred_element_type=F32) for a, ch in zip(kw, chains)]

    out = []
    for ch, g, s_i, qs_i, sv_i, kw_i, kv_i in zip(chains, gate, s, qs, sv, kw, kv):
        q, n_row, st = ch[0], ch[8], ch[7]
        m_t, w_inter, _, m_new, decay, _ = g
        num = (w_inter * scale) * qs_i + sv_i
        qn = jnp.sum(q.astype(F32) * n_row, axis=1, keepdims=True) * scale
        den = w_inter * qn + jnp.sum(s_i, axis=1, keepdims=True)
        h = num * (1.0 / jnp.maximum(jnp.abs(den), jnp.exp(-m_t)))
        out.append((h, decay * st + kv_i, decay * n_row + jnp.sum(kw_i, axis=0, keepdims=True), m_new))
    return out


def _mlstm_kernel(q_ref, k_ref, v_ref, o_ref, gr_ref, gc_ref, nw_ref, out_ref, acc_ref, st_ref, n_ref, m_ref,
                  *, scale, ncc, nct, hpb):
    L = ML_CHUNK
    st_ref[...] = jnp.zeros_like(st_ref)
    n_ref[...] = jnp.zeros_like(n_ref)
    m_ref[...] = jnp.zeros_like(m_ref)
    acc_ref[...] = jnp.zeros_like(acc_ref)

    dqk, dv = st_ref.shape[1], st_ref.shape[2]

    def step(j, carry):
        where, chains = [], []
        for hh in range(hpb):
            for d, cj in ((0, j), (1, jnp.where(j < ncc, ncc - 1 - j, nct - 1 - (j - ncc)))):
                rows = pl.ds(pl.multiple_of(cj * L, L), L)
                slot = 2 * hh + d
                gcol = gc_ref[0, hh, cj]
                where.append((rows, hh, slot, acc_ref[rows, hh * dv:(hh + 1) * dv]))
                chains.append((q_ref[rows, hh * dqk:(hh + 1) * dqk], k_ref[rows, hh * dqk:(hh + 1) * dqk],
                               v_ref[rows, hh * dv:(hh + 1) * dv],
                               gr_ref[0, 2 * d, hh, cj], gr_ref[0, 2 * d + 1, hh, cj],
                               gcol[:, 2 * d:2 * d + 1], gcol[:, 2 * d + 1:2 * d + 2],
                               st_ref[slot], n_ref[slot], m_ref[slot], d))
        for (rows, hh, slot, acc), (h, st_new, n_new, m_new) in zip(where, _mlstm_chunks(chains, scale)):
            acc_ref[rows, hh * dv:(hh + 1) * dv] = acc + h
            st_ref[slot] = st_new
            n_ref[slot] = n_new
            m_ref[slot] = m_new
        return carry

    lax.fori_loop(0, nct, step, 0)

    def finish(cj, carry):
        rows = pl.ds(pl.multiple_of(cj * L, L), L)
        for hh in range(hpb):
            cols = slice(hh * dv, (hh + 1) * dv)
            hs = acc_ref[rows, cols]
            hn = hs * lax.rsqrt(jnp.mean(hs * hs, axis=1, keepdims=True) + RMS_EPS)
            out_ref[rows, cols] = (hn * nw_ref[:, cols]
                                   * _sigmoid(o_ref[rows, cols].astype(F32))).astype(out_ref.dtype)
        return carry

    lax.fori_loop(0, nct, finish, 0)


def mlstm_bidir(qkvo, gates_pre, norm_w, bsz, heads, dqk, dv, ncc, nct):
    t = qkvo.shape[0]
    L = ML_CHUNK
    n2 = nct * L
    hpb = 1
    assert ncc % 2 == 0 and (nct - ncc) % 2 == 0
    g4 = gates_pre.reshape(bsz, nct, L, 4, heads)
    g_rows = jnp.transpose(g4, (0, 3, 4, 1, 2)).reshape(bsz, 4, heads, nct, 1, L)
    g_cols = jnp.transpose(g4, (0, 4, 1, 2, 3))
    koff = heads // hpb
    voff = 2 * heads * dqk // (dv * hpb)
    ooff = voff + heads // hpb
    return pl.pallas_call(
        functools.partial(_mlstm_kernel, scale=dqk ** -0.5, ncc=ncc, nct=nct, hpb=hpb),
        grid=(bsz, heads // hpb),
        in_specs=[
            pl.BlockSpec((n2, hpb * dqk), lambda b, h: (b, h)),
            pl.BlockSpec((n2, hpb * dqk), lambda b, h: (b, koff + h)),
            pl.BlockSpec((n2, hpb * dv), lambda b, h: (b, voff + h)),
            pl.BlockSpec((n2, hpb * dv), lambda b, h: (b, ooff + h)),
            pl.BlockSpec((1, 4, hpb, nct, 1, L), lambda b, h: (b, 0, h, 0, 0, 0)),
            pl.BlockSpec((1, hpb, nct, L, 4), lambda b, h: (b, h, 0, 0, 0)),
            pl.BlockSpec((1, hpb * dv), lambda b, h: (0, h)),
        ],
        out_specs=pl.BlockSpec((n2, hpb * dv), lambda b, h: (b, h)),
        out_shape=jax.ShapeDtypeStruct((t, heads * dv), BF16),
        scratch_shapes=[pltpu.VMEM((n2, hpb * dv), F32), pltpu.VMEM((2 * hpb, dqk, dv), F32),
                        pltpu.VMEM((2 * hpb, 1, dqk), F32), pltpu.VMEM((2 * hpb, 1, 1), F32)],
        compiler_params=_cparams("parallel", "parallel"),
        name="mlstm",
    )(qkvo, qkvo, qkvo, qkvo, g_rows, g_cols, norm_w.reshape(1, heads * dv))


def _attn_kernel(sink_ref, q_ref, k_ref, v_ref, o_ref, bias_ref, *, lc, nb, group, hd):
    kv = pl.program_id(1)
    blk = AT_BLOCK
    rows = group * blk
    kc = k_ref[0, 0:lc, :]
    vc = v_ref[0, 0:lc, :]
    rid = lax.broadcasted_iota(jnp.int32, (rows, 1), 0)
    sink_col = jnp.zeros((rows, 1), F32)
    for g in range(group):
        sink_col = jnp.where(rid // blk == g, sink_ref[kv * group + g], sink_col)
    qrow = lax.broadcasted_iota(jnp.int32, (rows, 3 * blk), 0) % blk
    kcol = lax.broadcasted_iota(jnp.int32, (rows, 3 * blk), 1)
    for back in range(3):
        rel = kcol - back * blk - qrow
        bias_ref[back] = jnp.where(jnp.abs(rel) <= AT_WINDOW, 0.0, NEG_INF)

    def body(j, carry):
        ws = jnp.clip(j - 1, 0, nb - 3)
        qs = pl.multiple_of(lc + j * blk, blk)
        ks = pl.multiple_of(lc + ws * blk, blk)
        q4 = q_ref[0, pl.ds(qs, blk), :]
        qq = jnp.concatenate([q4[:, g * hd:(g + 1) * hd] for g in range(group)], axis=0)
        kw = k_ref[0, pl.ds(ks, 3 * blk), :]
        vw = v_ref[0, pl.ds(ks, 3 * blk), :]
        s_loc = lax.dot_general(qq, kw, NT_DIMS, preferred_element_type=F32)
        s_ctx = lax.dot_general(qq, kc, NT_DIMS, preferred_element_type=F32)
        s_loc = s_loc + bias_ref[j - ws]
        m = jnp.maximum(jnp.maximum(jnp.max(s_loc, axis=1, keepdims=True),
                                    jnp.max(s_ctx, axis=1, keepdims=True)), sink_col)
        p_loc = jnp.exp(s_loc - m)
        p_ctx = jnp.exp(s_ctx - m)
        den = (jnp.sum(p_loc, axis=1, keepdims=True) + jnp.sum(p_ctx, axis=1, keepdims=True)
               + jnp.exp(sink_col - m))
        o = (jnp.dot(p_loc.astype(BF16), vw, preferred_element_type=F32)
             + jnp.dot(p_ctx.astype(BF16), vc, preferred_element_type=F32)) / den
        os = pl.multiple_of(j * blk, blk)
        for g in range(group):
            o_ref[0, pl.ds(os, blk), g * hd:(g + 1) * hd] = o[g * blk:(g + 1) * blk].astype(o_ref.dtype)
        return carry

    lax.fori_loop(0, nb, body, 0, unroll=2)


def window_attention(qkv3, sink, lc, n, heads, kv_heads, hd):
    bsz = qkv3.shape[0]
    group = heads // kv_heads
    nb = n // AT_BLOCK
    assert nb >= 3
    koff = heads
    voff = heads + kv_heads
    return pl.pallas_call(
        functools.partial(_attn_kernel, lc=lc, nb=nb, group=group, hd=hd),
        grid=(bsz, kv_heads),
        in_specs=[
            pl.BlockSpec(memory_space=pltpu.SMEM),
            pl.BlockSpec((1, lc + n, group * hd), lambda b, g: (b, 0, g)),
            pl.BlockSpec((1, lc + n, hd), lambda b, g: (b, 0, koff + g)),
            pl.BlockSpec((1, lc + n, hd), lambda b, g: (b, 0, voff + g)),
        ],
        out_specs=pl.BlockSpec((1, n, group * hd), lambda b, g: (b, 0, g)),
        out_shape=jax.ShapeDtypeStruct((bsz, n, heads * hd), BF16),
        scratch_shapes=[pltpu.VMEM((3, group * AT_BLOCK, 3 * AT_BLOCK), F32)],
        compiler_params=_cparams("parallel", "parallel"),
        name="window_attention",
    )(sink, qkv3, qkv3, qkv3)


def _ln_kernel(*refs, dual, gather, tile, has_h, has_router, alpha):
    refs = list(refs)
    if gather:
        pos_ref = refs.pop(0)
    x_ref = refs.pop(0)
    c_ref = refs.pop(0) if dual else None
    y_ref = refs.pop(0)
    gw_ref = refs.pop(0) if gather else None
    gate_ref, lng_ref, lnb_ref = refs.pop(0), refs.pop(0), refs.pop(0)
    if has_h:
        sh_ref, sc_ref = refs.pop(0), refs.pop(0)
    if has_router:
        rw_ref = refs.pop(0)
    xo_ref = refs.pop(0)
    if has_h:
        ho_ref = refs.pop(0)
    if has_router:
        lg_ref = refs.pop(0)

    if gather:
        buf_ref, sems = refs.pop(0), refs.pop(0)
        lin = pl.program_id(0) * pl.num_programs(1) + pl.program_id(1)
        ntiles = pl.num_programs(0) * pl.num_programs(1)
        ntok = pos_ref.shape[0] // 2

        def issue(t):
            slot = t % 2

            def start(r, carry):
                for k in range(2):
                    src = pos_ref[k * ntok + t * tile + r]
                    pltpu.make_async_copy(y_ref.at[src], buf_ref.at[slot, k, r], sems.at[slot]).start(priority=k)
                return carry

            lax.fori_loop(0, tile, start, 0, unroll=4)

        @pl.when(lin == 0)
        def _():
            issue(lin)

        @pl.when(lin + 1 < ntiles)
        def _():
            issue(lin + 1)

        slot = lin % 2
        for k in range(2):
            pltpu.make_async_copy(y_ref.at[pl.ds(0, tile)], buf_ref.at[slot, k], sems.at[slot]).wait()
        gw = gw_ref[...]
        d = x_ref.shape[-1]
        y = (gw[:, 0:1] * buf_ref[slot, 0].reshape(tile, d).astype(F32)
             + gw[:, 1:2] * buf_ref[slot, 1].reshape(tile, d).astype(F32))
    else:
        y = y_ref[...].astype(F32)

    if dual:
        x = jnp.where(pl.program_id(1) == 0, c_ref[0], x_ref[0])
    else:
        x = x_ref[...]
    z = alpha * x + gate_ref[0] * y
    mu = jnp.mean(z, axis=1, keepdims=True)
    zc = z - mu
    var = jnp.mean(zc * zc, axis=1, keepdims=True)
    xn = zc * lax.rsqrt(var + LN_EPS) * lng_ref[...] + lnb_ref[...]
    xo_ref[...] = xn
    if has_h:
        h = xn * (1.0 + sc_ref[0]) + sh_ref[0]
        ho_ref[...] = h.reshape(ho_ref.shape).astype(ho_ref.dtype)
        if has_router:
            lg_ref[...] = lax.dot_general(rw_ref[...], h, NT_DIMS, preferred_element_type=F32,
                                          precision=HIGHEST)


def ln_block(x, y, ada3, gate_col, lng, lnb, *, alpha, nb, per, tile, ctx_rows, ctx=None, x_per=None,
             x_off=0, nxt=None, h_dtype=None, h_rows3d=False, router_wt=None, pos=None, pos_w=None):
    d = x.shape[-1]
    dual = ctx is not None
    gather = pos is not None
    rows = nb * per * tile
    prow = (lambda b, j: jnp.where(j == 0, nb, b)) if ctx_rows else (lambda b, j: b)

    def im(f):
        return lambda b, j, *_: f(b, j)

    in_specs, args = [], []
    if dual:
        in_specs.append(pl.BlockSpec((1, tile, d), im(lambda b, j: (b, jnp.maximum(j - 1, 0), 0))))
        in_specs.append(pl.BlockSpec((1, tile, d), im(lambda b, j: (b, 0, 0))))
        args += [x, ctx]
    else:
        in_specs.append(pl.BlockSpec((tile, d), im(lambda b, j: (b * x_per + x_off + j, 0))))
        args.append(x)
    if gather:
        in_specs.append(pl.BlockSpec(memory_space=pl.ANY))
        in_specs.append(pl.BlockSpec((tile, 2), im(lambda b, j: (b * per + j, 0))))
        args += [y, pos_w]
    else:
        in_specs.append(pl.BlockSpec((tile, d), im(lambda b, j: (b * per + j, 0))))
        args.append(y)
    in_specs.append(pl.BlockSpec((1, 1, d), im(lambda b, j: (prow(b, j), 0, gate_col))))
    in_specs.append(pl.BlockSpec((1, d), im(lambda b, j: (0, 0))))
    in_specs.append(pl.BlockSpec((1, d), im(lambda b, j: (0, 0))))
    args += [ada3, lng.reshape(1, d), lnb.reshape(1, d)]
    if nxt is not None:
        ada_n, shc, scc = nxt
        in_specs.append(pl.BlockSpec((1, 1, d), im(lambda b, j: (prow(b, j), 0, shc))))
        in_specs.append(pl.BlockSpec((1, 1, d), im(lambda b, j: (prow(b, j), 0, scc))))
        args += [ada_n, ada_n]
    if router_wt is not None:
        ne = router_wt.shape[0]
        in_specs.append(pl.BlockSpec((ne, d), im(lambda b, j: (0, 0))))
        args.append(router_wt)

    out_specs = [pl.BlockSpec((tile, d), im(lambda b, j: (b * per + j, 0)))]
    out_shape = [jax.ShapeDtypeStruct((rows, d), F32)]
    if nxt is not None and h_rows3d:
        out_specs.append(pl.BlockSpec((tile, d // LANES, LANES), im(lambda b, j: (b * per + j, 0, 0))))
        out_shape.append(jax.ShapeDtypeStruct((rows, d // LANES, LANES), h_dtype))
    elif nxt is not None:
        out_specs.append(pl.BlockSpec((tile, d), im(lambda b, j: (b * per + j, 0))))
        out_shape.append(jax.ShapeDtypeStruct((rows, d), h_dtype))
    if router_wt is not None:
        out_specs.append(pl.BlockSpec((ne, tile), im(lambda b, j: (0, b * per + j))))
        out_shape.append(jax.ShapeDtypeStruct((ne, rows), F32))

    scratch = [pltpu.VMEM((2, 2, tile) + y.shape[1:], y.dtype), pltpu.SemaphoreType.DMA((2,))] if gather else []
    kern = functools.partial(_ln_kernel, dual=dual, gather=gather, tile=tile, has_h=nxt is not None,
                             has_router=router_wt is not None, alpha=alpha)
    call = pl.pallas_call(
        kern,
        grid_spec=pltpu.PrefetchScalarGridSpec(
            num_scalar_prefetch=1 if gather else 0,
            grid=(nb, per),
            in_specs=in_specs,
            out_specs=out_specs,
            scratch_shapes=scratch,
        ),
        out_shape=out_shape,
        compiler_params=_cparams("arbitrary", "arbitrary"),
        name="ln_gather" if gather else "ln_block",
    )
    return call(*(([pos] if gather else []) + args))


def _router_kernel(b_ref, l_ref, g_ref, m_ref, *, ne):
    epg = ne // N_GROUPS
    logits = l_ref[...]
    s = [_sigmoid(logits[e:e + 1, :]) for e in range(ne)]
    sel = [s[e] + b_ref[e] for e in range(ne)]
    gscore = []
    for g in range(N_GROUPS):
        v = sel[g * epg:(g + 1) * epg]
        best = None
        for i in range(epg):
            for j in range(i + 1, epg):
                pair = v[i] + v[j]
                best = pair if best is None else jnp.maximum(best, pair)
        gscore.append(best)
    chosen = []
    for g in range(N_GROUPS):
        ok = None
        for o in range(N_GROUPS):
            if o == g:
                continue
            t = (gscore[g] > gscore[o]) if o < g else (gscore[g] >= gscore[o])
            ok = t if ok is None else jnp.logical_and(ok, t)
        chosen.append(ok)
    mask = []
    for g in range(N_GROUPS):
        v = sel[g * epg:(g + 1) * epg]
        for i in range(epg):
            rank = jnp.zeros_like(v[i])
            for j in range(epg):
                if j == i:
                    continue
                ahead = (v[j] >= v[i]) if j < i else (v[j] > v[i])
                rank = rank + jnp.where(ahead, 1.0, 0.0)
            mask.append(jnp.logical_and(chosen[g], rank < TOP_K))
    wsum = jnp.zeros_like(s[0])
    for e in range(ne):
        wsum = wsum + jnp.where(mask[e], s[e], 0.0)
    for e in range(ne):
        g_ref[e:e + 1, :] = jnp.where(mask[e], s[e] / wsum, 0.0)
        m_ref[e:e + 1, :] = jnp.where(mask[e], 1, 0).astype(jnp.int32)


def router(logits_t, router_b):
    ne, t = logits_t.shape
    tl = _pick(t, (1024, 512, 256, 128))
    spec = pl.BlockSpec((ne, tl), lambda i: (0, i))
    return pl.pallas_call(
        functools.partial(_router_kernel, ne=ne),
        grid=(t // tl,),
        in_specs=[pl.BlockSpec(memory_space=pltpu.SMEM), spec],
        out_specs=[spec, spec],
        out_shape=[jax.ShapeDtypeStruct((ne, t), F32), jax.ShapeDtypeStruct((ne, t), jnp.int32)],
        compiler_params=_cparams("parallel"),
        name="router",
    )(router_b, logits_t)


def dispatch_plan(gates, mask, tm):
    ne, t = mask.shape
    n_tiles = (TOP_K * t) // tm + ne
    rows = n_tiles * tm
    sel = mask > 0
    counts = jnp.sum(mask, axis=1)
    ptiles = (counts + tm - 1) // tm
    tile_end = jnp.cumsum(ptiles)
    off = (tile_end - ptiles) * tm
    dest = off[:, None] + jnp.cumsum(mask, axis=1) - mask
    pos_lo = jnp.min(jnp.where(sel, dest, rows), axis=0)
    pos_hi = jnp.max(jnp.where(sel, dest, -1), axis=0)
    g_lo = jnp.sum(jnp.where(sel & (dest == pos_lo[None, :]), gates, 0.0), axis=0)
    g_hi = jnp.sum(jnp.where(sel & (dest == pos_hi[None, :]), gates, 0.0), axis=0)
    pos = jnp.concatenate([pos_lo, pos_hi]).astype(jnp.int32)
    tok = jnp.arange(t, dtype=jnp.int32)
    row_token = jnp.zeros((rows,), jnp.int32).at[pos].set(jnp.concatenate([tok, tok]), unique_indices=True)
    n_valid = tile_end[-1]
    tid = jnp.minimum(jnp.arange(n_tiles, dtype=jnp.int32), n_valid - 1)
    tile_expert = jnp.minimum(jnp.sum(tile_end[None, :] <= tid[:, None], axis=1), ne - 1).astype(jnp.int32)
    prev = jnp.concatenate([jnp.full((1,), -1, jnp.int32), tile_expert[:-1]])
    tile_first = (tile_expert != prev).astype(jnp.int32)
    nv = jnp.reshape(n_valid, (1,)).astype(jnp.int32)
    return row_token, jnp.stack([g_lo, g_hi], axis=1), pos, tid, tile_expert, tile_first, nv


def _gather_kernel(tok_ref, nv_ref, h_ref, o_ref, buf_ref, sems, *, tm):
    i = pl.program_id(0)
    nv = nv_ref[0]

    def issue(tile):
        slot = tile % 2

        def start(r2, carry):
            for u in range(2):
                r = 2 * r2 + u
                pltpu.make_async_copy(h_ref.at[tok_ref[tile * tm + r]], buf_ref.at[slot, r],
                                      sems.at[slot]).start(priority=u)
            return carry

        lax.fori_loop(0, tm // 2, start, 0, unroll=4)

    @pl.when(i == 0)
    def _():
        issue(i)

    @pl.when(i + 1 < nv)
    def _():
        issue(i + 1)

    @pl.when(i < nv)
    def _():
        slot = i % 2
        pltpu.make_async_copy(h_ref.at[pl.ds(0, tm)], buf_ref.at[slot], sems.at[slot]).wait()
        o_ref[...] = buf_ref[slot].reshape(o_ref.shape)

    @pl.when(i >= nv)
    def _():
        o_ref[...] = jnp.zeros_like(o_ref)


def gather_rows(h, row_token, nv, tm):
    d = h.shape[1] * h.shape[2]
    rows = row_token.shape[0]
    return pl.pallas_call(
        functools.partial(_gather_kernel, tm=tm),
        grid_spec=pltpu.PrefetchScalarGridSpec(
            num_scalar_prefetch=2,
            grid=(rows // tm,),
            in_specs=[pl.BlockSpec(memory_space=pl.ANY)],
            out_specs=pl.BlockSpec((tm, d), lambda i, tok, nvr: (i, 0)),
            scratch_shapes=[pltpu.VMEM((2, tm) + h.shape[1:], h.dtype), pltpu.SemaphoreType.DMA((2,))],
        ),
        out_shape=jax.ShapeDtypeStruct((rows, d), BF16),
        compiler_params=_cparams("arbitrary"),
        name="moe_gather",
    )(row_token, nv, h)


def _ffn_a_kernel(tid_ref, te_ref, tf_ref, nv_ref, x_ref, w1_ref, w3_ref, o_ref, w1b, w3b):
    i = pl.program_id(1)

    @pl.when(i < nv_ref[0])
    def _():
        @pl.when(tf_ref[i] == 1)
        def _():
            w1b[...] = w1_ref[0].astype(BF16)
            w3b[...] = w3_ref[0].astype(BF16)

        x = x_ref[...]
        a = jnp.dot(x, w1b[...], preferred_element_type=F32)
        b = jnp.dot(x, w3b[...], preferred_element_type=F32)
        o_ref[...] = (a * _sigmoid(a) * b).astype(o_ref.dtype)

    @pl.when(i >= nv_ref[0])
    def _():
        o_ref[...] = jnp.zeros_like(o_ref)


def ffn_stage_a(xs, w1, w3, layer, plan, tm):
    _, _, _, tid, te, tf, nv = plan
    rows, d = xs.shape
    ff = w1.shape[3]
    tfc = _pick(ff, (512, 256, 128))
    wspec = pl.BlockSpec((None, 1, d, tfc), lambda f, i, tid, te, tf, nv: (layer, te[i], 0, f))
    return pl.pallas_call(
        _ffn_a_kernel,
        grid_spec=pltpu.PrefetchScalarGridSpec(
            num_scalar_prefetch=4,
            grid=(ff // tfc, rows // tm),
            in_specs=[pl.BlockSpec((tm, d), lambda f, i, tid, te, tf, nv: (tid[i], 0)), wspec, wspec],
            out_specs=pl.BlockSpec((tm, tfc), lambda f, i, tid, te, tf, nv: (i, f)),
            scratch_shapes=[pltpu.VMEM((d, tfc), BF16), pltpu.VMEM((d, tfc), BF16)],
        ),
        out_shape=jax.ShapeDtypeStruct((rows, ff), BF16),
        compiler_params=_cparams("arbitrary", "arbitrary"),
        name="moe_ffn_a",
    )(tid, te, tf, nv, xs, w1, w3)


def _ffn_b_kernel(tid_ref, te_ref, tf_ref, nv_ref, h_ref, w2_ref, o_ref, w2b):
    i = pl.program_id(1)

    @pl.when(i < nv_ref[0])
    def _():
        @pl.when(tf_ref[i] == 1)
        def _():
            w2b[...] = w2_ref[0].astype(BF16)

        y = jnp.dot(h_ref[...], w2b[...], preferred_element_type=F32)
        o_ref[...] = y.reshape(o_ref.shape).astype(o_ref.dtype)

    @pl.when(i >= nv_ref[0])
    def _():
        o_ref[...] = jnp.zeros_like(o_ref)


def ffn_stage_b(hs, w2, layer, plan, tm):
    _, _, _, tid, te, tf, nv = plan
    rows, ff = hs.shape
    d = w2.shape[3]
    tn = _pick(d, (2048, 1024, 512, 256, 128))
    return pl.pallas_call(
        _ffn_b_kernel,
        grid_spec=pltpu.PrefetchScalarGridSpec(
            num_scalar_prefetch=4,
            grid=(d // tn, rows // tm),
            in_specs=[
                pl.BlockSpec((tm, ff), lambda n, i, tid, te, tf, nv: (tid[i], 0)),
                pl.BlockSpec((None, 1, ff, tn), lambda n, i, tid, te, tf, nv: (layer, te[i], 0, n)),
            ],
            out_specs=pl.BlockSpec((tm, tn // LANES, LANES), lambda n, i, tid, te, tf, nv: (i, n, 0)),
            scratch_shapes=[pltpu.VMEM((ff, tn), BF16)],
        ),
        out_shape=jax.ShapeDtypeStruct((rows, d // LANES, LANES), BF16),
        compiler_params=_cparams("arbitrary", "arbitrary"),
        name="moe_ffn_b",
    )(tid, te, tf, nv, hs, w2)


def moe_sparse(h, logits_t, router_b, w1, w3, w2, layer):
    gates, mask = router(logits_t, router_b)
    plan = dispatch_plan(gates, mask, MOE_TILE)
    xs = gather_rows(h, plan[0], plan[6], MOE_TILE)
    hs = ffn_stage_a(xs, w1, w3, layer, plan, MOE_TILE)
    ys = ffn_stage_b(hs, w2, layer, plan, MOE_TILE)
    return ys, plan[2], plan[1]


def rope_tables(n, lc, hd):
    half = hd // 2
    t = jnp.arange(n)
    inv = jnp.power(ROPE_BASE, -jnp.arange(0, half, 2, dtype=F32) / half)
    ang_r = (t // GRID_W).astype(F32)[:, None] * inv
    ang_c = (t % GRID_W).astype(F32)[:, None] * inv
    cos = jnp.concatenate([jnp.cos(ang_r)] * 2 + [jnp.cos(ang_c)] * 2, axis=1)
    sin = jnp.concatenate([-jnp.sin(ang_r), jnp.sin(ang_r), -jnp.sin(ang_c), jnp.sin(ang_c)], axis=1)
    cos = jnp.concatenate([jnp.ones((lc, hd), F32), cos], axis=0)
    sin = jnp.concatenate([jnp.zeros((lc, hd), F32), sin], axis=0)
    scale = hd ** -0.5
    ones, zeros = jnp.ones_like(cos), jnp.zeros_like(sin)
    return jnp.stack([cos * scale, cos, ones]), jnp.stack([sin * scale, sin, zeros])


def kernel(x, c, ctx, c_ctx, ada_w, ada_b, ln_g, ln_b, mlstm_w_in, mlstm_b_gates, mlstm_norm_w, mlstm_w_out,
           attn_w_in, attn_b_in, attn_sink, attn_w_out, router_w, router_b, moe_w1, moe_w3, moe_w2):
    bsz, n, d = x.shape
    lc = ctx.shape[1]
    depth = ada_w.shape[0]
    assert depth == 2 and bsz <= 4 and n % lc == 0 and lc % ML_CHUNK == 0
    alpha = (2 * depth) ** 0.25
    tile = lc
    per = (n + lc) // tile
    per_lat = n // tile

    cc = jnp.concatenate([c, c_ctx[None, :], jnp.zeros((8 - bsz - 1, d), F32)], axis=0)
    ada = adaln(cc, ada_w, ada_b)
    ada0 = ada[0].reshape(8, 1, 6 * d)
    ada1 = ada[1].reshape(8, 1, 6 * d)
    rwt = router_w.T

    heads = mlstm_b_gates.shape[1] // 4
    dv = d // heads
    dqk = dv // 2
    nmain = 2 * heads * (dqk + dv)
    w_in_t = jnp.swapaxes(mlstm_w_in, 1, 2)
    h0, gates_pre = mod0(x, ctx, ada0, w_in_t[0, nmain:, :], mlstm_b_gates[0][None, :])
    qkvo = matmul(h0, w_in_t, n_cols=nmain, w_t=True)
    ncc, nct = lc // ML_CHUNK, (n + lc) // ML_CHUNK
    hg = mlstm_bidir(qkvo, gates_pre, mlstm_norm_w[0], bsz, heads, dqk, dv, ncc, nct)
    y = matmul(hg, mlstm_w_out, out_dtype=F32)
    x1, hffn, lg = ln_block(x, y, ada0, 2, ln_g[0, 0], ln_b[0, 0], alpha=alpha, nb=bsz, per=per, tile=tile,
                            ctx_rows=True, ctx=ctx, nxt=(ada0, 3, 4), h_dtype=BF16, h_rows3d=True, router_wt=rwt)
    ys, pos, pos_w = moe_sparse(hffn, lg, router_b, moe_w1, moe_w3, moe_w2, 0)
    x2, h1 = ln_block(x1, ys, ada0, 5, ln_g[0, 1], ln_b[0, 1], alpha=alpha, nb=bsz, per=per, tile=tile,
                      ctx_rows=True, x_per=per, nxt=(ada1, 0, 1), h_dtype=BF16, pos=pos, pos_w=pos_w)

    heads_a = attn_sink.shape[1]
    hd = d // heads_a
    kvh = (attn_w_in.shape[2] // hd - heads_a) // 2
    cos_t, sin_t = rope_tables(n, lc, hd)
    q_end, k_end = heads_a * hd, (heads_a + kvh) * hd
    type_of = lambda col: jnp.where(col < q_end, 0, jnp.where(col < k_end, 1, 2))
    tm1 = _pick(n + lc, (1152, 768, 384, 256, 128))
    qkv = matmul(h1, attn_w_in, bias=attn_b_in[0], rope=(cos_t, sin_t, n + lc, type_of), tm=tm1,
                 tn=_pick(kvh * hd, (512, 256, 128)))
    att = window_attention(qkv.reshape(bsz, n + lc, -1), attn_sink[0], lc, n, heads_a, kvh, hd)
    y = matmul(att.reshape(bsz * n, heads_a * hd), attn_w_out, out_dtype=F32)
    x3, hffn, lg = ln_block(x2, y, ada1, 2, ln_g[1, 0], ln_b[1, 0], alpha=alpha, nb=bsz, per=per_lat, tile=tile,
                            ctx_rows=False, x_per=per, x_off=1, nxt=(ada1, 3, 4), h_dtype=BF16, h_rows3d=True,
                            router_wt=rwt)
    ys, pos, pos_w = moe_sparse(hffn, lg, router_b, moe_w1, moe_w3, moe_w2, 1)
    (x4,) = ln_block(x3, ys, ada1, 5, ln_g[1, 1], ln_b[1, 1], alpha=alpha, nb=bsz, per=per_lat, tile=tile,
                     ctx_rows=False, x_per=per_lat, pos=pos, pos_w=pos_w)
    return x4.reshape(bsz, n, d)
```

```python
import functools

import jax
import jax.numpy as jnp
from jax import lax
from jax.experimental import pallas as pl
from jax.experimental.pallas import tpu as pltpu

F32 = jnp.float32
BF16 = jnp.bfloat16
NEG_INF = float("-inf")

GRID_W = 64
ML_CHUNK = 128
AT_WINDOW = 128
AT_BLOCK = 128
ROPE_BASE = 10000.0
N_GROUPS = 4
TOP_K = 2
LN_EPS = 1e-5
RMS_EPS = 1e-6

LANES = 128
VMEM_LIMIT = 56 * 1024 * 1024
MOE_TILE = 512
HIGHEST = lax.Precision.HIGHEST
NT_DIMS = (((1,), (1,)), ((), ()))


def _cparams(*sem):
    return pltpu.CompilerParams(dimension_semantics=sem, vmem_limit_bytes=VMEM_LIMIT)


def _pick(n, candidates):
    for c in candidates:
        if n % c == 0:
            return c
    return n


def _sigmoid(x):
    return 1.0 / (1.0 + jnp.exp(-x))


def _log_sigmoid(x):
    return jnp.minimum(x, 0.0) - jnp.log1p(jnp.exp(-jnp.abs(x)))


def _adaln_kernel(c_ref, w_ref, b_ref, o_ref):
    c = c_ref[...]
    a = (c * _sigmoid(c)).astype(BF16)
    o_ref[0] = jnp.dot(a, w_ref[0].astype(BF16), preferred_element_type=F32) + b_ref[0]


def adaln(cc, ada_w, ada_b):
    depth, d, n6 = ada_w.shape
    tn = _pick(n6, (512, 256, 128))
    return pl.pallas_call(
        _adaln_kernel,
        grid=(depth, n6 // tn),
        in_specs=[
            pl.BlockSpec((8, d), lambda l, j: (0, 0)),
            pl.BlockSpec((1, d, tn), lambda l, j: (l, 0, j)),
            pl.BlockSpec((1, 1, tn), lambda l, j: (l, 0, j)),
        ],
        out_specs=pl.BlockSpec((1, 8, tn), lambda l, j: (l, 0, j)),
        out_shape=jax.ShapeDtypeStruct((depth, 8, n6), F32),
        compiler_params=_cparams("parallel", "parallel"),
        name="adaln",
    )(cc, ada_w, ada_b.reshape(depth, 1, n6))


def _mod0_kernel(x_ref, c_ref, sh_ref, sc_ref, wg_ref, bg_ref, h_ref, g_ref):
    j = pl.program_id(1)
    t = jnp.where(j == 0, c_ref[0], x_ref[0])
    h = t * (1.0 + sc_ref[0]) + sh_ref[0]
    h_ref[...] = h.astype(BF16)
    g_ref[...] = lax.dot_general(h, wg_ref[...], NT_DIMS, preferred_element_type=F32,
                                 precision=HIGHEST) + bg_ref[...]


def mod0(x, ctx, ada3, wg, bg):
    bsz, n, d = x.shape
    lc = ctx.shape[1]
    per = (n + lc) // lc
    ng = wg.shape[0]
    prow = lambda b, j: jnp.where(j == 0, bsz, b)
    return pl.pallas_call(
        _mod0_kernel,
        grid=(bsz, per),
        in_specs=[
            pl.BlockSpec((1, lc, d), lambda b, j: (b, jnp.maximum(j - 1, 0), 0)),
            pl.BlockSpec((1, lc, d), lambda b, j: (b, 0, 0)),
            pl.BlockSpec((1, 1, d), lambda b, j: (prow(b, j), 0, 0)),
            pl.BlockSpec((1, 1, d), lambda b, j: (prow(b, j), 0, 1)),
            pl.BlockSpec((ng, d), lambda b, j: (0, 0)),
            pl.BlockSpec((1, ng), lambda b, j: (0, 0)),
        ],
        out_specs=[
            pl.BlockSpec((lc, d), lambda b, j: (b * per + j, 0)),
            pl.BlockSpec((lc, ng), lambda b, j: (b * per + j, 0)),
        ],
        out_shape=[
            jax.ShapeDtypeStruct((bsz * (n + lc), d), BF16),
            jax.ShapeDtypeStruct((bsz * (n + lc), ng), F32),
        ],
        compiler_params=_cparams("parallel", "arbitrary"),
        name="mod0",
    )(x, ctx, ada3, ada3, wg, bg)


def _mm_kernel(*refs, has_bias, has_rope, tn, w_t):
    x_ref, w_ref = refs[0], refs[1]
    pos = 2
    if w_t:
        acc = lax.dot_general(x_ref[...], w_ref[...].astype(BF16), NT_DIMS, preferred_element_type=F32)
    else:
        acc = jnp.dot(x_ref[...], w_ref[...].astype(BF16), preferred_element_type=F32)
    if has_bias:
        acc = acc + refs[pos][...]
        pos += 1
    o_ref = refs[-1]
    if has_rope:
        ca = refs[pos][0]
        sb = refs[pos + 1][0]
        lane = lax.broadcasted_iota(jnp.int32, ca.shape, 1)
        first = (lane % 64) < 32
        for h in range(tn // LANES):
            y = acc[:, h * LANES:(h + 1) * LANES]
            sw = jnp.where(first, pltpu.roll(y, 96, 1), pltpu.roll(y, 32, 1))
            o_ref[:, h * LANES:(h + 1) * LANES] = (y * ca + sw * sb).astype(o_ref.dtype)
    else:
        o_ref[...] = acc.astype(o_ref.dtype)


def matmul(x, w, bias=None, rope=None, out_dtype=BF16, tm=None, tn=None, n_cols=None, w_t=False):
    m, k = x.shape
    n = n_cols or w.shape[1 if w_t else 2]
    tm = tm or _pick(m, (1152, 1024, 768, 512, 256, 128))
    tn = tn or _pick(n, (512, 256, 128))
    in_specs = [
        pl.BlockSpec((tm, k), lambda i, j: (i, 0)),
        pl.BlockSpec((None, tn, k), lambda i, j: (0, j, 0)) if w_t else
        pl.BlockSpec((None, k, tn), lambda i, j: (0, 0, j)),
    ]
    args = [x, w]
    if bias is not None:
        in_specs.append(pl.BlockSpec((1, tn), lambda i, j: (0, j)))
        args.append(bias.reshape(1, n))
    if rope is not None:
        ca, sb, period, type_of = rope
        nper = period // tm
        spec = pl.BlockSpec((1, tm, LANES), lambda i, j: (type_of(j * tn), i % nper, 0))
        in_specs += [spec, spec]
        args += [ca, sb]
    return pl.pallas_call(
        functools.partial(_mm_kernel, has_bias=bias is not None, has_rope=rope is not None, tn=tn, w_t=w_t),
        grid=(m // tm, n // tn),
        in_specs=in_specs,
        out_specs=pl.BlockSpec((tm, tn), lambda i, j: (i, j)),
        out_shape=jax.ShapeDtypeStruct((m, n), out_dtype),
        compiler_params=_cparams("parallel", "arbitrary"),
        name="matmul",
    )(*args)


def _mlstm_chunks(chains, scale):
    L = ML_CHUNK
    r = lax.broadcasted_iota(jnp.int32, (L, L), 0)
    c = lax.broadcasted_iota(jnp.int32, (L, L), 1)
    gate = []
    for q, k, v, ig_row, f_row, ig_col, f_col, st, n_row, m_prev, d in chains:
        reverse = d == 1
        lf_row = _log_sigmoid(f_row)
        lf_col = _log_sigmoid(f_col)
        keep = (c >= r) if reverse else (c <= r)
        keep_t = (r >= c) if reverse else (r <= c)
        b_col = jnp.sum(jnp.where(keep, lf_row, 0.0), axis=1, keepdims=True)
        b_row = jnp.sum(jnp.where(keep_t, lf_col, 0.0), axis=0, keepdims=True)
        logw = jnp.where(keep, b_col - b_row + ig_row, NEG_INF)
        inter = b_col + m_prev
        m_t = jnp.maximum(inter, jnp.max(logw, axis=1, keepdims=True))
        w_inter = jnp.exp(inter - m_t)
        p = jnp.exp(logw - m_t)
        b_end = b_col[0:1, :] if reverse else b_col[L - 1:L, :]
        m_new = jnp.maximum(b_end + m_prev, jnp.max(b_end - b_row + ig_row, axis=1, keepdims=True))
        decay = jnp.exp(b_end + m_prev - m_new)
        w_src = jnp.exp(b_end - b_col + ig_col - m_new)
        gate.append((m_t, w_inter, p, m_new, decay, w_src))

    qk = [lax.dot_general(ch[0], ch[1], NT_DIMS, preferred_element_type=F32) for ch in chains]
    qs = [jnp.dot(ch[0], ch[7].astype(BF16), preferred_element_type=F32) for ch in chains]
    s = [a * (g[2] * scale) for a, g in zip(qk, gate)]
    sv = [jnp.dot(a.astype(BF16), ch[2], preferred_element_type=F32) for a, ch in zip(s, chains)]
    kw = [ch[1].astype(F32) * g[5] for ch, g in zip(chains, gate)]
    kv = [jnp.dot(a.T.astype(BF16), ch[2], preferred_element_type=F32) for a, ch in zip(kw, chains)]

    out = []
    for ch, g, s_i, qs_i, sv_i, kw_i, kv_i in zip(chains, gate, s, qs, sv, kw, kv):
        q, n_row, st = ch[0], ch[8], ch[7]
        m_t, w_inter, _, m_new, decay, _ = g
        num = (w_inter * scale) * qs_i + sv_i
        qn = jnp.sum(q.astype(F32) * n_row, axis=1, keepdims=True) * scale
        den = w_inter * qn + jnp.sum(s_i, axis=1, keepdims=True)
        h = num * (1.0 / jnp.maximum(jnp.abs(den), jnp.exp(-m_t)))
        out.append((h, decay * st + kv_i, decay * n_row + jnp.sum(kw_i, axis=0, keepdims=True), m_new))
    return out


def _mlstm_kernel(q_ref, k_ref, v_ref, o_ref, gr_ref, gc_ref, nw_ref, out_ref, acc_ref, st_ref, n_ref, m_ref,
                  *, scale, ncc, nct, hpb):
    L = ML_CHUNK
    st_ref[...] = jnp.zeros_like(st_ref)
    n_ref[...] = jnp.zeros_like(n_ref)
    m_ref[...] = jnp.zeros_like(m_ref)
    acc_ref[...] = jnp.zeros_like(acc_ref)

    dqk, dv = st_ref.shape[1], st_ref.shape[2]

    def step(j, carry):
        where, chains = [], []
        for hh in range(hpb):
            for d, cj in ((0, j), (1, jnp.where(j < ncc, ncc - 1 - j, nct - 1 - (j - ncc)))):
                rows = pl.ds(pl.multiple_of(cj * L, L), L)
                slot = 2 * hh + d
                gcol = gc_ref[0, hh, cj]
                where.append((rows, hh, slot, acc_ref[rows, hh * dv:(hh + 1) * dv]))
                chains.append((q_ref[rows, hh * dqk:(hh + 1) * dqk], k_ref[rows, hh * dqk:(hh + 1) * dqk],
                               v_ref[rows, hh * dv:(hh + 1) * dv],
                               gr_ref[0, 2 * d, hh, cj], gr_ref[0, 2 * d + 1, hh, cj],
                               gcol[:, 2 * d:2 * d + 1], gcol[:, 2 * d + 1:2 * d + 2],
                               st_ref[slot], n_ref[slot], m_ref[slot], d))
        for (rows, hh, slot, acc), (h, st_new, n_new, m_new) in zip(where, _mlstm_chunks(chains, scale)):
            acc_ref[rows, hh * dv:(hh + 1) * dv] = acc + h
            st_ref[slot] = st_new
            n_ref[slot] = n_new
            m_ref[slot] = m_new
        return carry

    lax.fori_loop(0, nct, step, 0)

    def finish(cj, carry):
        rows = pl.ds(pl.multiple_of(cj * L, L), L)
        for hh in range(hpb):
            cols = slice(hh * dv, (hh + 1) * dv)
            hs = acc_ref[rows, cols]
            hn = hs * lax.rsqrt(jnp.mean(hs * hs, axis=1, keepdims=True) + RMS_EPS)
            out_ref[rows, cols] = (hn * nw_ref[:, cols]
                                   * _sigmoid(o_ref[rows, cols].astype(F32))).astype(out_ref.dtype)
        return carry

    lax.fori_loop(0, nct, finish, 0)


def mlstm_bidir(qkvo, gates_pre, norm_w, bsz, heads, dqk, dv, ncc, nct):
    t = qkvo.shape[0]
    L = ML_CHUNK
    n2 = nct * L
    hpb = 1
    assert ncc % 2 == 0 and (nct - ncc) % 2 == 0
    g4 = gates_pre.reshape(bsz, nct, L, 4, heads)
    g_rows = jnp.transpose(g4, (0, 3, 4, 1, 2)).reshape(bsz, 4, heads, nct, 1, L)
    g_cols = jnp.transpose(g4, (0, 4, 1, 2, 3))
    koff = heads // hpb
    voff = 2 * heads * dqk // (dv * hpb)
    ooff = voff + heads // hpb
    return pl.pallas_call(
        functools.partial(_mlstm_kernel, scale=dqk ** -0.5, ncc=ncc, nct=nct, hpb=hpb),
        grid=(bsz, heads // hpb),
        in_specs=[
            pl.BlockSpec((n2, hpb * dqk), lambda b, h: (b, h)),
            pl.BlockSpec((n2, hpb * dqk), lambda b, h: (b, koff + h)),
            pl.BlockSpec((n2, hpb * dv), lambda b, h: (b, voff + h)),
            pl.BlockSpec((n2, hpb * dv), lambda b, h: (b, ooff + h)),
            pl.BlockSpec((1, 4, hpb, nct, 1, L), lambda b, h: (b, 0, h, 0, 0, 0)),
            pl.BlockSpec((1, hpb, nct, L, 4), lambda b, h: (b, h, 0, 0, 0)),
            pl.BlockSpec((1, hpb * dv), lambda b, h: (0, h)),
        ],
        out_specs=pl.BlockSpec((n2, hpb * dv), lambda b, h: (b, h)),
        out_shape=jax.ShapeDtypeStruct((t, heads * dv), BF16),
        scratch_shapes=[pltpu.VMEM((n2, hpb * dv), F32), pltpu.VMEM((2 * hpb, dqk, dv), F32),
                        pltpu.VMEM((2 * hpb, 1, dqk), F32), pltpu.VMEM((2 * hpb, 1, 1), F32)],
        compiler_params=_cparams("parallel", "parallel"),
        name="mlstm",
    )(qkvo, qkvo, qkvo, qkvo, g_rows, g_cols, norm_w.reshape(1, heads * dv))


def _attn_kernel(sink_ref, q_ref, k_ref, v_ref, o_ref, bias_ref, *, lc, nb, group, hd):
    kv = pl.program_id(1)
    blk = AT_BLOCK
    rows = group * blk
    kc = k_ref[0, 0:lc, :]
    vc = v_ref[0, 0:lc, :]
    rid = lax.broadcasted_iota(jnp.int32, (rows, 1), 0)
    sink_col = jnp.zeros((rows, 1), F32)
    for g in range(group):
        sink_col = jnp.where(rid // blk == g, sink_ref[kv * group + g], sink_col)
    qrow = lax.broadcasted_iota(jnp.int32, (rows, 3 * blk), 0) % blk
    kcol = lax.broadcasted_iota(jnp.int32, (rows, 3 * blk), 1)
    for back in range(3):
        rel = kcol - back * blk - qrow
        bias_ref[back] = jnp.where(jnp.abs(rel) <= AT_WINDOW, 0.0, NEG_INF)

    def body(j, carry):
        ws = jnp.clip(j - 1, 0, nb - 3)
        qs = pl.multiple_of(lc + j * blk, blk)
        ks = pl.multiple_of(lc + ws * blk, blk)
        q4 = q_ref[0, pl.ds(qs, blk), :]
        qq = jnp.concatenate([q4[:, g * hd:(g + 1) * hd] for g in range(group)], axis=0)
        kw = k_ref[0, pl.ds(ks, 3 * blk), :]
        vw = v_ref[0, pl.ds(ks, 3 * blk), :]
        s_loc = lax.dot_general(qq, kw, NT_DIMS, preferred_element_type=F32)
        s_ctx = lax.dot_general(qq, kc, NT_DIMS, preferred_element_type=F32)
        s_loc = s_loc + bias_ref[j - ws]
        m = jnp.maximum(jnp.maximum(jnp.max(s_loc, axis=1, keepdims=True),
                                    jnp.max(s_ctx, axis=1, keepdims=True)), sink_col)
        p_loc = jnp.exp(s_loc - m)
        p_ctx = jnp.exp(s_ctx - m)
        den = (jnp.sum(p_loc, axis=1, keepdims=True) + jnp.sum(p_ctx, axis=1, keepdims=True)
               + jnp.exp(sink_col - m))
        o = (jnp.dot(p_loc.astype(BF16), vw, preferred_element_type=F32)
             + jnp.dot(p_ctx.astype(BF16), vc, preferred_element_type=F32)) / den
        os = pl.multiple_of(j * blk, blk)
        for g in range(group):
            o_ref[0, pl.ds(os, blk), g * hd:(g + 1) * hd] = o[g * blk:(g + 1) * blk].astype(o_ref.dtype)
        return carry

    lax.fori_loop(0, nb, body, 0, unroll=2)


def window_attention(qkv3, sink, lc, n, heads, kv_heads, hd):
    bsz = qkv3.shape[0]
    group = heads // kv_heads
    nb = n // AT_BLOCK
    assert nb >= 3
    koff = heads
    voff = heads + kv_heads
    return pl.pallas_call(
        functools.partial(_attn_kernel, lc=lc, nb=nb, group=group, hd=hd),
        grid=(bsz, kv_heads),
        in_specs=[
            pl.BlockSpec(memory_space=pltpu.SMEM),
            pl.BlockSpec((1, lc + n, group * hd), lambda b, g: (b, 0, g)),
            pl.BlockSpec((1, lc + n, hd), lambda b, g: (b, 0, koff + g)),
            pl.BlockSpec((1, lc + n, hd), lambda b, g: (b, 0, voff + g)),
        ],
        out_specs=pl.BlockSpec((1, n, group * hd), lambda b, g: (b, 0, g)),
        out_shape=jax.ShapeDtypeStruct((bsz, n, heads * hd), BF16),
        scratch_shapes=[pltpu.VMEM((3, group * AT_BLOCK, 3 * AT_BLOCK), F32)],
        compiler_params=_cparams("parallel", "parallel"),
        name="window_attention",
    )(sink, qkv3, qkv3, qkv3)


def _ln_kernel(*refs, dual, gather, tile, has_h, has_router, alpha):
    refs = list(refs)
    if gather:
        pos_ref = refs.pop(0)
    x_ref = refs.pop(0)
    c_ref = refs.pop(0) if dual else None
    y_ref = refs.pop(0)
    gw_ref = refs.pop(0) if gather else None
    gate_ref, lng_ref, lnb_ref = refs.pop(0), refs.pop(0), refs.pop(0)
    if has_h:
        sh_ref, sc_ref = refs.pop(0), refs.pop(0)
    if has_router:
        rw_ref = refs.pop(0)
    xo_ref = refs.pop(0)
    if has_h:
        ho_ref = refs.pop(0)
    if has_router:
        lg_ref = refs.pop(0)

    if gather:
        buf_ref, sems = refs.pop(0), refs.pop(0)
        lin = pl.program_id(0) * pl.num_programs(1) + pl.program_id(1)
        ntiles = pl.num_programs(0) * pl.num_programs(1)
        ntok = pos_ref.shape[0] // 2

        def issue(t):
            slot = t % 2

            def start(r, carry):
                for k in range(2):
                    src = pos_ref[k * ntok + t * tile + r]
                    pltpu.make_async_copy(y_ref.at[src], buf_ref.at[slot, k, r], sems.at[slot]).start(priority=k)
                return carry

            lax.fori_loop(0, tile, start, 0, unroll=4)

        @pl.when(lin == 0)
        def _():
            issue(lin)

        @pl.when(lin + 1 < ntiles)
        def _():
            issue(lin + 1)

        slot = lin % 2
        for k in range(2):
            pltpu.make_async_copy(y_ref.at[pl.ds(0, tile)], buf_ref.at[slot, k], sems.at[slot]).wait()
        gw = gw_ref[...]
        d = x_ref.shape[-1]
        y = (gw[:, 0:1] * buf_ref[slot, 0].reshape(tile, d).astype(F32)
             + gw[:, 1:2] * buf_ref[slot, 1].reshape(tile, d).astype(F32))
    else:
        y = y_ref[...].astype(F32)

    if dual:
        x = jnp.where(pl.program_id(1) == 0, c_ref[0], x_ref[0])
    else:
        x = x_ref[...]
    z = alpha * x + gate_ref[0] * y
    mu = jnp.mean(z, axis=1, keepdims=True)
    zc = z - mu
    var = jnp.mean(zc * zc, axis=1, keepdims=True)
    xn = zc * lax.rsqrt(var + LN_EPS) * lng_ref[...] + lnb_ref[...]
    xo_ref[...] = xn
    if has_h:
        h = xn * (1.0 + sc_ref[0]) + sh_ref[0]
        ho_ref[...] = h.reshape(ho_ref.shape).astype(ho_ref.dtype)
        if has_router:
            lg_ref[...] = lax.dot_general(rw_ref[...], h, NT_DIMS, preferred_element_type=F32,
                                          precision=HIGHEST)


def ln_block(x, y, ada3, gate_col, lng, lnb, *, alpha, nb, per, tile, ctx_rows, ctx=None, x_per=None,
             x_off=0, nxt=None, h_dtype=None, h_rows3d=False, router_wt=None, pos=None, pos_w=None):
    d = x.shape[-1]
    dual = ctx is not None
    gather = pos is not None
    rows = nb * per * tile
    prow = (lambda b, j: jnp.where(j == 0, nb, b)) if ctx_rows else (lambda b, j: b)

    def im(f):
        return lambda b, j, *_: f(b, j)

    in_specs, args = [], []
    if dual:
        in_specs.append(pl.BlockSpec((1, tile, d), im(lambda b, j: (b, jnp.maximum(j - 1, 0), 0))))
        in_specs.append(pl.BlockSpec((1, tile, d), im(lambda b, j: (b, 0, 0))))
        args += [x, ctx]
    else:
        in_specs.append(pl.BlockSpec((tile, d), im(lambda b, j: (b * x_per + x_off + j, 0))))
        args.append(x)
    if gather:
        in_specs.append(pl.BlockSpec(memory_space=pl.ANY))
        in_specs.append(pl.BlockSpec((tile, 2), im(lambda b, j: (b * per + j, 0))))
        args += [y, pos_w]
    else:
        in_specs.append(pl.BlockSpec((tile, d), im(lambda b, j: (b * per + j, 0))))
        args.append(y)
    in_specs.append(pl.BlockSpec((1, 1, d), im(lambda b, j: (prow(b, j), 0, gate_col))))
    in_specs.append(pl.BlockSpec((1, d), im(lambda b, j: (0, 0))))
    in_specs.append(pl.BlockSpec((1, d), im(lambda b, j: (0, 0))))
    args += [ada3, lng.reshape(1, d), lnb.reshape(1, d)]
    if nxt is not None:
        ada_n, shc, scc = nxt
        in_specs.append(pl.BlockSpec((1, 1, d), im(lambda b, j: (prow(b, j), 0, shc))))
        in_specs.append(pl.BlockSpec((1, 1, d), im(lambda b, j: (prow(b, j), 0, scc))))
        args += [ada_n, ada_n]
    if router_wt is not None:
        ne = router_wt.shape[0]
        in_specs.append(pl.BlockSpec((ne, d), im(lambda b, j: (0, 0))))
        args.append(router_wt)

    out_specs = [pl.BlockSpec((tile, d), im(lambda b, j: (b * per + j, 0)))]
    out_shape = [jax.ShapeDtypeStruct((rows, d), F32)]
    if nxt is not None and h_rows3d:
        out_specs.append(pl.BlockSpec((tile, d // LANES, LANES), im(lambda b, j: (b * per + j, 0, 0))))
        out_shape.append(jax.ShapeDtypeStruct((rows, d // LANES, LANES), h_dtype))
    elif nxt is not None:
        out_specs.append(pl.BlockSpec((tile, d), im(lambda b, j: (b * per + j, 0))))
        out_shape.append(jax.ShapeDtypeStruct((rows, d), h_dtype))
    if router_wt is not None:
        out_specs.append(pl.BlockSpec((ne, tile), im(lambda b, j: (0, b * per + j))))
        out_shape.append(jax.ShapeDtypeStruct((ne, rows), F32))

    scratch = [pltpu.VMEM((2, 2, tile) + y.shape[1:], y.dtype), pltpu.SemaphoreType.DMA((2,))] if gather else []
    kern = functools.partial(_ln_kernel, dual=dual, gather=gather, tile=tile, has_h=nxt is not None,
                             has_router=router_wt is not None, alpha=alpha)
    call = pl.pallas_call(
        kern,
        grid_spec=pltpu.PrefetchScalarGridSpec(
            num_scalar_prefetch=1 if gather else 0,
            grid=(nb, per),
            in_specs=in_specs,
            out_specs=out_specs,
            scratch_shapes=scratch,
        ),
        out_shape=out_shape,
        compiler_params=_cparams("arbitrary", "arbitrary"),
        name="ln_gather" if gather else "ln_block",
    )
    return call(*(([pos] if gather else []) + args))


def _router_kernel(b_ref, l_ref, g_ref, m_ref, *, ne):
    epg = ne // N_GROUPS
    logits = l_ref[...]
    s = [_sigmoid(logits[e:e + 1, :]) for e in range(ne)]
    sel = [s[e] + b_ref[e] for e in range(ne)]
    gscore = []
    for g in range(N_GROUPS):
        v = sel[g * epg:(g + 1) * epg]
        best = None
        for i in range(epg):
            for j in range(i + 1, epg):
                pair = v[i] + v[j]
                best = pair if best is None else jnp.maximum(best, pair)
        gscore.append(best)
    chosen = []
    for g in range(N_GROUPS):
        ok = None
        for o in range(N_GROUPS):
            if o == g:
                continue
            t = (gscore[g] > gscore[o]) if o < g else (gscore[g] >= gscore[o])
            ok = t if ok is None else jnp.logical_and(ok, t)
        chosen.append(ok)
    mask = []
    for g in range(N_GROUPS):
        v = sel[g * epg:(g + 1) * epg]
        for i in range(epg):
            rank = jnp.zeros_like(v[i])
            for j in range(epg):
                if j == i:
                    continue
                ahead = (v[j] >= v[i]) if j < i else (v[j] > v[i])
                rank = rank + jnp.where(ahead, 1.0, 0.0)
            mask.append(jnp.logical_and(chosen[g], rank < TOP_K))
    wsum = jnp.zeros_like(s[0])
    for e in range(ne):
        wsum = wsum + jnp.where(mask[e], s[e], 0.0)
    for e in range(ne):
        g_ref[e:e + 1, :] = jnp.where(mask[e], s[e] / wsum, 0.0)
        m_ref[e:e + 1, :] = jnp.where(mask[e], 1, 0).astype(jnp.int32)


def router(logits_t, router_b):
    ne, t = logits_t.shape
    tl = _pick(t, (1024, 512, 256, 128))
    spec = pl.BlockSpec((ne, tl), lambda i: (0, i))
    return pl.pallas_call(
        functools.partial(_router_kernel, ne=ne),
        grid=(t // tl,),
        in_specs=[pl.BlockSpec(memory_space=pltpu.SMEM), spec],
        out_specs=[spec, spec],
        out_shape=[jax.ShapeDtypeStruct((ne, t), F32), jax.ShapeDtypeStruct((ne, t), jnp.int32)],
        compiler_params=_cparams("parallel"),
        name="router",
    )(router_b, logits_t)


def dispatch_plan(gates, mask, tm):
    ne, t = mask.shape
    n_tiles = (TOP_K * t) // tm + ne
    rows = n_tiles * tm
    sel = mask > 0
    counts = jnp.sum(mask, axis=1)
    ptiles = (counts + tm - 1) // tm
    tile_end = jnp.cumsum(ptiles)
    off = (tile_end - ptiles) * tm
    dest = off[:, None] + jnp.cumsum(mask, axis=1) - mask
    pos_lo = jnp.min(jnp.where(sel, dest, rows), axis=0)
    pos_hi = jnp.max(jnp.where(sel, dest, -1), axis=0)
    g_lo = jnp.sum(jnp.where(sel & (dest == pos_lo[None, :]), gates, 0.0), axis=0)
    g_hi = jnp.sum(jnp.where(sel & (dest == pos_hi[None, :]), gates, 0.0), axis=0)
    pos = jnp.concatenate([pos_lo, pos_hi]).astype(jnp.int32)
    n_valid = tile_end[-1]
    tiles = jnp.arange(n_tiles, dtype=jnp.int32)
    tid = jnp.minimum(tiles, n_valid - 1)
    tile_expert = jnp.minimum(jnp.sum(tile_end[None, :] <= tid[:, None], axis=1), ne - 1).astype(jnp.int32)
    prev = jnp.concatenate([jnp.full((1,), -1, jnp.int32), tile_expert[:-1]])
    tile_first = (tile_expert != prev).astype(jnp.int32)
    last = jnp.any((tiles[:, None] == tile_end[None, :] - 1) & (ptiles[None, :] > 0), axis=1)
    tile_fill = (last | (tiles >= n_valid)).astype(jnp.int32)
    nv = jnp.reshape(n_valid, (1,)).astype(jnp.int32)
    return tile_fill, jnp.stack([g_lo, g_hi], axis=1), pos, tid, tile_expert, tile_first, nv


def _scatter_kernel(pos_ref, fill_ref, x_ref, o_ref, stage_ref, zero_ref, sems, zsem, *, tile, tm):
    i = pl.program_id(0)
    n = pl.num_programs(0)
    ntok = pos_ref.shape[0] // 2
    slot = i % 2

    def slot_wait(s):
        for _ in range(2):
            pltpu.make_async_copy(stage_ref.at[s], o_ref.at[pl.ds(0, tile)], sems.at[s]).wait()

    @pl.when(i == 0)
    def _():
        zero_ref[...] = jnp.zeros_like(zero_ref)

        def fill(t, wait):
            @pl.when(fill_ref[t] == 1)
            def _():
                cp = pltpu.make_async_copy(zero_ref, o_ref.at[pl.ds(t * tm, tm)], zsem)
                cp.wait() if wait else cp.start()

        lax.fori_loop(0, fill_ref.shape[0], lambda t, c: (fill(t, False), c)[1], 0)
        lax.fori_loop(0, fill_ref.shape[0], lambda t, c: (fill(t, True), c)[1], 0)

    @pl.when(i >= 2)
    def _():
        slot_wait(slot)

    stage_ref[slot] = x_ref[...]

    def start(r, carry):
        for k in range(2):
            dst = pos_ref[k * ntok + i * tile + r]
            pltpu.make_async_copy(stage_ref.at[slot, r], o_ref.at[dst], sems.at[slot]).start(priority=k)
        return carry

    lax.fori_loop(0, tile, start, 0, unroll=4)

    @pl.when(i == n - 1)
    def _():
        slot_wait(slot)

        @pl.when(n >= 2)
        def _():
            slot_wait(1 - slot)


def scatter_rows(h, pos, tile_fill, tm, tile):
    t = h.shape[0]
    rows = tile_fill.shape[0] * tm
    return pl.pallas_call(
        functools.partial(_scatter_kernel, tile=tile, tm=tm),
        grid_spec=pltpu.PrefetchScalarGridSpec(
            num_scalar_prefetch=2,
            grid=(t // tile,),
            in_specs=[pl.BlockSpec((tile,) + h.shape[1:], lambda i, p, f: (i, 0, 0))],
            out_specs=pl.BlockSpec(memory_space=pl.ANY),
            scratch_shapes=[pltpu.VMEM((2, tile) + h.shape[1:], h.dtype), pltpu.VMEM((tm,) + h.shape[1:], h.dtype),
                            pltpu.SemaphoreType.DMA((2,)), pltpu.SemaphoreType.DMA(())],
        ),
        out_shape=jax.ShapeDtypeStruct((rows,) + h.shape[1:], h.dtype),
        compiler_params=_cparams("arbitrary"),
        name="moe_scatter",
    )(pos, tile_fill, h)


def _ffn_a_kernel(tid_ref, te_ref, tf_ref, nv_ref, x_ref, w1_ref, w3_ref, o_ref, w1b, w3b):
    i = pl.program_id(1)

    @pl.when(i < nv_ref[0])
    def _():
        @pl.when(tf_ref[i] == 1)
        def _():
            w1b[...] = w1_ref[0].astype(BF16)
            w3b[...] = w3_ref[0].astype(BF16)

        x = x_ref[...].reshape(x_ref.shape[0], -1)
        a = jnp.dot(x, w1b[...], preferred_element_type=F32)
        b = jnp.dot(x, w3b[...], preferred_element_type=F32)
        o_ref[...] = (a * _sigmoid(a) * b).astype(o_ref.dtype)

    @pl.when(i >= nv_ref[0])
    def _():
        o_ref[...] = jnp.zeros_like(o_ref)


def ffn_stage_a(xs, w1, w3, layer, plan, tm):
    _, _, _, tid, te, tf, nv = plan
    rows = xs.shape[0]
    d = xs.shape[1] * xs.shape[2]
    ff = w1.shape[3]
    tfc = _pick(ff, (512, 256, 128))
    wspec = pl.BlockSpec((None, 1, d, tfc), lambda f, i, tid, te, tf, nv: (layer, te[i], 0, f))
    return pl.pallas_call(
        _ffn_a_kernel,
        grid_spec=pltpu.PrefetchScalarGridSpec(
            num_scalar_prefetch=4,
            grid=(ff // tfc, rows // tm),
            in_specs=[pl.BlockSpec((tm,) + xs.shape[1:], lambda f, i, tid, te, tf, nv: (tid[i], 0, 0)), wspec, wspec],
            out_specs=pl.BlockSpec((tm, tfc), lambda f, i, tid, te, tf, nv: (i, f)),
            scratch_shapes=[pltpu.VMEM((d, tfc), BF16), pltpu.VMEM((d, tfc), BF16)],
        ),
        out_shape=jax.ShapeDtypeStruct((rows, ff), BF16),
        compiler_params=_cparams("arbitrary", "arbitrary"),
        name="moe_ffn_a",
    )(tid, te, tf, nv, xs, w1, w3)


def _ffn_b_kernel(tid_ref, te_ref, tf_ref, nv_ref, h_ref, w2_ref, o_ref, w2b):
    i = pl.program_id(1)

    @pl.when(i < nv_ref[0])
    def _():
        @pl.when(tf_ref[i] == 1)
        def _():
            w2b[...] = w2_ref[0].astype(BF16)

        y = jnp.dot(h_ref[...], w2b[...], preferred_element_type=F32)
        o_ref[...] = y.reshape(o_ref.shape).astype(o_ref.dtype)

    @pl.when(i >= nv_ref[0])
    def _():
        o_ref[...] = jnp.zeros_like(o_ref)


def ffn_stage_b(hs, w2, layer, plan, tm):
    _, _, _, tid, te, tf, nv = plan
    rows, ff = hs.shape
    d = w2.shape[3]
    tn = _pick(d, (2048, 1024, 512, 256, 128))
    return pl.pallas_call(
        _ffn_b_kernel,
        grid_spec=pltpu.PrefetchScalarGridSpec(
            num_scalar_prefetch=4,
            grid=(d // tn, rows // tm),
            in_specs=[
                pl.BlockSpec((tm, ff), lambda n, i, tid, te, tf, nv: (tid[i], 0)),
                pl.BlockSpec((None, 1, ff, tn), lambda n, i, tid, te, tf, nv: (layer, te[i], 0, n)),
            ],
            out_specs=pl.BlockSpec((tm, tn // LANES, LANES), lambda n, i, tid, te, tf, nv: (i, n, 0)),
            scratch_shapes=[pltpu.VMEM((ff, tn), BF16)],
        ),
        out_shape=jax.ShapeDtypeStruct((rows, d // LANES, LANES), BF16),
        compiler_params=_cparams("arbitrary", "arbitrary"),
        name="moe_ffn_b",
    )(tid, te, tf, nv, hs, w2)


def moe_sparse(h, logits_t, router_b, w1, w3, w2, layer, tile):
    gates, mask = router(logits_t, router_b)
    plan = dispatch_plan(gates, mask, MOE_TILE)
    xs = scatter_rows(h, plan[2], plan[0], MOE_TILE, tile)
    hs = ffn_stage_a(xs, w1, w3, layer, plan, MOE_TILE)
    ys = ffn_stage_b(hs, w2, layer, plan, MOE_TILE)
    return ys, plan[2], plan[1]


def rope_tables(n, lc, hd):
    half = hd // 2
    t = jnp.arange(n)
    inv = jnp.power(ROPE_BASE, -jnp.arange(0, half, 2, dtype=F32) / half)
    ang_r = (t // GRID_W).astype(F32)[:, None] * inv
    ang_c = (t % GRID_W).astype(F32)[:, None] * inv
    cos = jnp.concatenate([jnp.cos(ang_r)] * 2 + [jnp.cos(ang_c)] * 2, axis=1)
    sin = jnp.concatenate([-jnp.sin(ang_r), jnp.sin(ang_r), -jnp.sin(ang_c), jnp.sin(ang_c)], axis=1)
    cos = jnp.concatenate([jnp.ones((lc, hd), F32), cos], axis=0)
    sin = jnp.concatenate([jnp.zeros((lc, hd), F32), sin], axis=0)
    scale = hd ** -0.5
    ones, zeros = jnp.ones_like(cos), jnp.zeros_like(sin)
    return jnp.stack([cos * scale, cos, ones]), jnp.stack([sin * scale, sin, zeros])


def kernel(x, c, ctx, c_ctx, ada_w, ada_b, ln_g, ln_b, mlstm_w_in, mlstm_b_gates, mlstm_norm_w, mlstm_w_out,
           attn_w_in, attn_b_in, attn_sink, attn_w_out, router_w, router_b, moe_w1, moe_w3, moe_w2):
    bsz, n, d = x.shape
    lc = ctx.shape[1]
    depth = ada_w.shape[0]
    assert depth == 2 and bsz <= 4 and n % lc == 0 and lc % ML_CHUNK == 0
    alpha = (2 * depth) ** 0.25
    tile = lc
    per = (n + lc) // tile
    per_lat = n // tile

    cc = jnp.concatenate([c, c_ctx[None, :], jnp.zeros((8 - bsz - 1, d), F32)], axis=0)
    ada = adaln(cc, ada_w, ada_b)
    ada0 = ada[0].reshape(8, 1, 6 * d)
    ada1 = ada[1].reshape(8, 1, 6 * d)
    rwt = router_w.T

    heads = mlstm_b_gates.shape[1] // 4
    dv = d // heads
    dqk = dv // 2
    nmain = 2 * heads * (dqk + dv)
    w_in_t = jnp.swapaxes(mlstm_w_in, 1, 2)
    h0, gates_pre = mod0(x, ctx, ada0, w_in_t[0, nmain:, :], mlstm_b_gates[0][None, :])
    qkvo = matmul(h0, w_in_t, n_cols=nmain, w_t=True)
    ncc, nct = lc // ML_CHUNK, (n + lc) // ML_CHUNK
    hg = mlstm_bidir(qkvo, gates_pre, mlstm_norm_w[0], bsz, heads, dqk, dv, ncc, nct)
    y = matmul(hg, mlstm_w_out, out_dtype=F32)
    x1, hffn, lg = ln_block(x, y, ada0, 2, ln_g[0, 0], ln_b[0, 0], alpha=alpha, nb=bsz, per=per, tile=tile,
                            ctx_rows=True, ctx=ctx, nxt=(ada0, 3, 4), h_dtype=BF16, h_rows3d=True, router_wt=rwt)
    ys, pos, pos_w = moe_sparse(hffn, lg, router_b, moe_w1, moe_w3, moe_w2, 0, tile)
    x2, h1 = ln_block(x1, ys, ada0, 5, ln_g[0, 1], ln_b[0, 1], alpha=alpha, nb=bsz, per=per, tile=tile,
                      ctx_rows=True, x_per=per, nxt=(ada1, 0, 1), h_dtype=BF16, pos=pos, pos_w=pos_w)

    heads_a = attn_sink.shape[1]
    hd = d // heads_a
    kvh = (attn_w_in.shape[2] // hd - heads_a) // 2
    cos_t, sin_t = rope_tables(n, lc, hd)
    q_end, k_end = heads_a * hd, (heads_a + kvh) * hd
    type_of = lambda col: jnp.where(col < q_end, 0, jnp.where(col < k_end, 1, 2))
    tm1 = _pick(n + lc, (1152, 768, 384, 256, 128))
    qkv = matmul(h1, attn_w_in, bias=attn_b_in[0], rope=(cos_t, sin_t, n + lc, type_of), tm=tm1,
                 tn=_pick(kvh * hd, (512, 256, 128)))
    att = window_attention(qkv.reshape(bsz, n + lc, -1), attn_sink[0], lc, n, heads_a, kvh, hd)
    y = matmul(att.reshape(bsz * n, heads_a * hd), attn_w_out, out_dtype=F32)
    x3, hffn, lg = ln_block(x2, y, ada1, 2, ln_g[1, 0], ln_b[1, 0], alpha=alpha, nb=bsz, per=per_lat, tile=tile,
                            ctx_rows=False, x_per=per, x_off=1, nxt=(ada1, 3, 4), h_dtype=BF16, h_rows3d=True,
                            router_wt=rwt)
    ys, pos, pos_w = moe_sparse(hffn, lg, router_b, moe_w1, moe_w3, moe_w2, 1, tile)
    (x4,) = ln_block(x3, ys, ada1, 5, ln_g[1, 1], ln_b[1, 1], alpha=alpha, nb=bsz, per=per_lat, tile=tile,
                     ctx_rows=False, x_per=per_lat, pos=pos, pos_w=pos_w)
    return x4.reshape(bsz, n, d)
```

```python
import functools

import jax
import jax.numpy as jnp
from jax import lax
from jax.experimental import pallas as pl
from jax.experimental.pallas import tpu as pltpu

F32 = jnp.float32
BF16 = jnp.bfloat16
NEG_INF = float("-inf")

GRID_W = 64
ML_CHUNK = 128
AT_WINDOW = 128
AT_BLOCK = 128
ROPE_BASE = 10000.0
N_GROUPS = 4
TOP_K = 2
LN_EPS = 1e-5
RMS_EPS = 1e-6

LANES = 128
VMEM_LIMIT = 56 * 1024 * 1024
MOE_TILE = 256
HIGHEST = lax.Precision.HIGHEST
NT_DIMS = (((1,), (1,)), ((), ()))


def _cparams(*sem):
    return pltpu.CompilerParams(dimension_semantics=sem, vmem_limit_bytes=VMEM_LIMIT)


def _pick(n, candidates):
    for c in candidates:
        if n % c == 0:
            return c
    return n


def _sigmoid(x):
    return 1.0 / (1.0 + jnp.exp(-x))


def _log_sigmoid(x):
    return jnp.minimum(x, 0.0) - jnp.log1p(jnp.exp(-jnp.abs(x)))


def _adaln_block(c_ref, w_ref, b_ref):
    c = c_ref[...]
    a = (c * _sigmoid(c)).astype(BF16)
    return jnp.dot(a, w_ref[...].astype(BF16), preferred_element_type=F32) + b_ref[...]


def _adaln_kernel(c_ref, w_ref, b_ref, o_ref):
    o_ref[...] = _adaln_block(c_ref, w_ref, b_ref)


def adaln(cc, ada_w, ada_b, layer):
    depth, d, n6 = ada_w.shape
    tn = _pick(n6, (512, 256, 128))
    return pl.pallas_call(
        _adaln_kernel,
        grid=(n6 // tn,),
        in_specs=[
            pl.BlockSpec((8, d), lambda j: (0, 0)),
            pl.BlockSpec((None, d, tn), lambda j: (layer, 0, j)),
            pl.BlockSpec((None, 1, tn), lambda j: (layer, 0, j)),
        ],
        out_specs=pl.BlockSpec((8, tn), lambda j: (0, j)),
        out_shape=jax.ShapeDtypeStruct((8, n6), F32),
        compiler_params=_cparams("parallel"),
        name="adaln",
    )(cc, ada_w, ada_b.reshape(depth, 1, n6))


def _mod0_kernel(x_ref, c_ref, sh_ref, sc_ref, wg_ref, bg_ref, h_ref, g_ref):
    j = pl.program_id(1)
    t = jnp.where(j == 0, c_ref[0], x_ref[0])
    h = t * (1.0 + sc_ref[0]) + sh_ref[0]
    h_ref[...] = h.astype(BF16)
    g_ref[...] = lax.dot_general(h, wg_ref[...], NT_DIMS, preferred_element_type=F32,
                                 precision=HIGHEST) + bg_ref[...]


def mod0(x, ctx, ada3, wg, bg):
    bsz, n, d = x.shape
    lc = ctx.shape[1]
    per = (n + lc) // lc
    ng = wg.shape[0]
    prow = lambda b, j: jnp.where(j == 0, bsz, b)
    return pl.pallas_call(
        _mod0_kernel,
        grid=(bsz, per),
        in_specs=[
            pl.BlockSpec((1, lc, d), lambda b, j: (b, jnp.maximum(j - 1, 0), 0)),
            pl.BlockSpec((1, lc, d), lambda b, j: (b, 0, 0)),
            pl.BlockSpec((1, 1, d), lambda b, j: (prow(b, j), 0, 0)),
            pl.BlockSpec((1, 1, d), lambda b, j: (prow(b, j), 0, 1)),
            pl.BlockSpec((ng, d), lambda b, j: (0, 0)),
            pl.BlockSpec((1, ng), lambda b, j: (0, 0)),
        ],
        out_specs=[
            pl.BlockSpec((lc, d), lambda b, j: (b * per + j, 0)),
            pl.BlockSpec((lc, ng), lambda b, j: (b * per + j, 0)),
        ],
        out_shape=[
            jax.ShapeDtypeStruct((bsz * (n + lc), d), BF16),
            jax.ShapeDtypeStruct((bsz * (n + lc), ng), F32),
        ],
        compiler_params=_cparams("parallel", "arbitrary"),
        name="mod0",
    )(x, ctx, ada3, ada3, wg, bg)


def _mm_kernel(*refs, has_bias, has_rope, tn, w_t):
    x_ref, w_ref = refs[0], refs[1]
    pos = 2
    if w_t:
        acc = lax.dot_general(x_ref[...], w_ref[...].astype(BF16), NT_DIMS, preferred_element_type=F32)
    else:
        acc = jnp.dot(x_ref[...], w_ref[...].astype(BF16), preferred_element_type=F32)
    if has_bias:
        acc = acc + refs[pos][...]
        pos += 1
    o_ref = refs[-1]
    if has_rope:
        ca = refs[pos][0]
        sb = refs[pos + 1][0]
        lane = lax.broadcasted_iota(jnp.int32, ca.shape, 1)
        first = (lane % 64) < 32
        for h in range(tn // LANES):
            y = acc[:, h * LANES:(h + 1) * LANES]
            sw = jnp.where(first, pltpu.roll(y, 96, 1), pltpu.roll(y, 32, 1))
            o_ref[:, h * LANES:(h + 1) * LANES] = (y * ca + sw * sb).astype(o_ref.dtype)
    else:
        o_ref[...] = acc.astype(o_ref.dtype)


def matmul(x, w, bias=None, rope=None, out_dtype=BF16, tm=None, tn=None, n_cols=None, w_t=False):
    m, k = x.shape
    n = n_cols or w.shape[1 if w_t else 2]
    tm = tm or _pick(m, (1152, 1024, 768, 512, 256, 128))
    tn = tn or _pick(n, (512, 256, 128))
    in_specs = [
        pl.BlockSpec((tm, k), lambda i, j: (i, 0)),
        pl.BlockSpec((None, tn, k), lambda i, j: (0, j, 0)) if w_t else
        pl.BlockSpec((None, k, tn), lambda i, j: (0, 0, j)),
    ]
    args = [x, w]
    if bias is not None:
        in_specs.append(pl.BlockSpec((1, tn), lambda i, j: (0, j)))
        args.append(bias.reshape(1, n))
    if rope is not None:
        ca, sb, period, type_of = rope
        nper = period // tm
        spec = pl.BlockSpec((1, tm, LANES), lambda i, j: (type_of(j * tn), i % nper, 0))
        in_specs += [spec, spec]
        args += [ca, sb]
    return pl.pallas_call(
        functools.partial(_mm_kernel, has_bias=bias is not None, has_rope=rope is not None, tn=tn, w_t=w_t),
        grid=(m // tm, n // tn),
        in_specs=in_specs,
        out_specs=pl.BlockSpec((tm, tn), lambda i, j: (i, j)),
        out_shape=jax.ShapeDtypeStruct((m, n), out_dtype),
        compiler_params=_cparams("parallel", "arbitrary"),
        name="matmul",
    )(*args)


def _mlstm_chunks(chains, scale):
    L = ML_CHUNK
    r = lax.broadcasted_iota(jnp.int32, (L, L), 0)
    c = lax.broadcasted_iota(jnp.int32, (L, L), 1)
    gate = []
    for q, k, v, ig_row, f_row, ig_col, f_col, st, n_row, m_prev, d in chains:
        reverse = d == 1
        lf_row = _log_sigmoid(f_row)
        lf_col = _log_sigmoid(f_col)
        keep = (c >= r) if reverse else (c <= r)
        keep_t = (r >= c) if reverse else (r <= c)
        b_col = jnp.sum(jnp.where(keep, lf_row, 0.0), axis=1, keepdims=True)
        b_row = jnp.sum(jnp.where(keep_t, lf_col, 0.0), axis=0, keepdims=True)
        logw = jnp.where(keep, b_col - b_row + ig_row, NEG_INF)
        inter = b_col + m_prev
        m_t = jnp.maximum(inter, jnp.max(logw, axis=1, keepdims=True))
        w_inter = jnp.exp(inter - m_t)
        p = jnp.exp(logw - m_t)
        b_end = b_col[0:1, :] if reverse else b_col[L - 1:L, :]
        m_new = jnp.maximum(b_end + m_prev, jnp.max(b_end - b_row + ig_row, axis=1, keepdims=True))
        decay = jnp.exp(b_end + m_prev - m_new)
        w_src = jnp.exp(b_end - b_col + ig_col - m_new)
        gate.append((m_t, w_inter, p, m_new, decay, w_src))

    qk = [lax.dot_general(ch[0], ch[1], NT_DIMS, preferred_element_type=F32) for ch in chains]
    qs = [jnp.dot(ch[0], ch[7].astype(BF16), preferred_element_type=F32) for ch in chains]
    s = [a * (g[2] * scale) for a, g in zip(qk, gate)]
    sv = [jnp.dot(a.astype(BF16), ch[2], preferred_element_type=F32) for a, ch in zip(s, chains)]
    kw = [ch[1].astype(F32) * g[5] for ch, g in zip(chains, gate)]
    kv = [jnp.dot(a.T.astype(BF16), ch[2], preferred_element_type=F32) for a, ch in zip(kw, chains)]

    out = []
    for ch, g, s_i, qs_i, sv_i, kw_i, kv_i in zip(chains, gate, s, qs, sv, kw, kv):
        q, n_row, st = ch[0], ch[8], ch[7]
        m_t, w_inter, _, m_new, decay, _ = g
        num = (w_inter * scale) * qs_i + sv_i
        qn = jnp.sum(q.astype(F32) * n_row, axis=1, keepdims=True) * scale
        den = w_inter * qn + jnp.sum(s_i, axis=1, keepdims=True)
        h = num * (1.0 / jnp.maximum(jnp.abs(den), jnp.exp(-m_t)))
        out.append((h, decay * st + kv_i, decay * n_row + jnp.sum(kw_i, axis=0, keepdims=True), m_new))
    return out


def _mlstm_kernel(q_ref, k_ref, v_ref, o_ref, gr_ref, gc_ref, nw_ref, cc_ref, aw_ref, ab_ref, out_ref, ada_ref,
                  acc_ref, st_ref, n_ref, m_ref, *, scale, ncc, nct, hpb):
    L = ML_CHUNK
    ada_ref[...] = _adaln_block(cc_ref, aw_ref, ab_ref)
    st_ref[...] = jnp.zeros_like(st_ref)
    n_ref[...] = jnp.zeros_like(n_ref)
    m_ref[...] = jnp.zeros_like(m_ref)
    acc_ref[...] = jnp.zeros_like(acc_ref)

    dqk, dv = st_ref.shape[1], st_ref.shape[2]

    def step(j, carry):
        where, chains = [], []
        for hh in range(hpb):
            for d, cj in ((0, j), (1, jnp.where(j < ncc, ncc - 1 - j, nct - 1 - (j - ncc)))):
                rows = pl.ds(pl.multiple_of(cj * L, L), L)
                slot = 2 * hh + d
                gcol = gc_ref[0, hh, cj]
                where.append((rows, hh, slot, acc_ref[rows, hh * dv:(hh + 1) * dv]))
                chains.append((q_ref[rows, hh * dqk:(hh + 1) * dqk], k_ref[rows, hh * dqk:(hh + 1) * dqk],
                               v_ref[rows, hh * dv:(hh + 1) * dv],
                               gr_ref[0, 2 * d, hh, cj], gr_ref[0, 2 * d + 1, hh, cj],
                               gcol[:, 2 * d:2 * d + 1], gcol[:, 2 * d + 1:2 * d + 2],
                               st_ref[slot], n_ref[slot], m_ref[slot], d))
        for (rows, hh, slot, acc), (h, st_new, n_new, m_new) in zip(where, _mlstm_chunks(chains, scale)):
            acc_ref[rows, hh * dv:(hh + 1) * dv] = acc + h
            st_ref[slot] = st_new
            n_ref[slot] = n_new
            m_ref[slot] = m_new
        return carry

    lax.fori_loop(0, nct, step, 0)

    def finish(cj, carry):
        rows = pl.ds(pl.multiple_of(cj * L, L), L)
        for hh in range(hpb):
            cols = slice(hh * dv, (hh + 1) * dv)
            hs = acc_ref[rows, cols]
            hn = hs * lax.rsqrt(jnp.mean(hs * hs, axis=1, keepdims=True) + RMS_EPS)
            out_ref[rows, cols] = (hn * nw_ref[:, cols]
                                   * _sigmoid(o_ref[rows, cols].astype(F32))).astype(out_ref.dtype)
        return carry

    lax.fori_loop(0, nct, finish, 0)


def mlstm_bidir(qkvo, gates_pre, norm_w, bsz, heads, dqk, dv, ncc, nct, cc, ada_w, ada_b, ada_layer):
    t = qkvo.shape[0]
    L = ML_CHUNK
    n2 = nct * L
    hpb = 1
    depth, d, n6 = ada_w.shape
    nsteps = bsz * heads // hpb
    ta = n6 // nsteps
    assert n6 % nsteps == 0 and ta % LANES == 0
    assert ncc % 2 == 0 and (nct - ncc) % 2 == 0
    g4 = gates_pre.reshape(bsz, nct, L, 4, heads)
    g_rows = jnp.transpose(g4, (0, 3, 4, 1, 2)).reshape(bsz, 4, heads, nct, 1, L)
    g_cols = jnp.transpose(g4, (0, 4, 1, 2, 3))
    koff = heads // hpb
    voff = 2 * heads * dqk // (dv * hpb)
    ooff = voff + heads // hpb
    return pl.pallas_call(
        functools.partial(_mlstm_kernel, scale=dqk ** -0.5, ncc=ncc, nct=nct, hpb=hpb),
        grid=(bsz, heads // hpb),
        in_specs=[
            pl.BlockSpec((n2, hpb * dqk), lambda b, h: (b, h)),
            pl.BlockSpec((n2, hpb * dqk), lambda b, h: (b, koff + h)),
            pl.BlockSpec((n2, hpb * dv), lambda b, h: (b, voff + h)),
            pl.BlockSpec((n2, hpb * dv), lambda b, h: (b, ooff + h)),
            pl.BlockSpec((1, 4, hpb, nct, 1, L), lambda b, h: (b, 0, h, 0, 0, 0)),
            pl.BlockSpec((1, hpb, nct, L, 4), lambda b, h: (b, h, 0, 0, 0)),
            pl.BlockSpec((1, hpb * dv), lambda b, h: (0, h)),
            pl.BlockSpec((8, d), lambda b, h: (0, 0)),
            pl.BlockSpec((None, d, ta), lambda b, h: (ada_layer, 0, b * (heads // hpb) + h)),
            pl.BlockSpec((None, 1, ta), lambda b, h: (ada_layer, 0, b * (heads // hpb) + h)),
        ],
        out_specs=[pl.BlockSpec((n2, hpb * dv), lambda b, h: (b, h)),
                   pl.BlockSpec((8, ta), lambda b, h: (0, b * (heads // hpb) + h))],
        out_shape=[jax.ShapeDtypeStruct((t, heads * dv), BF16), jax.ShapeDtypeStruct((8, n6), F32)],
        scratch_shapes=[pltpu.VMEM((n2, hpb * dv), F32), pltpu.VMEM((2 * hpb, dqk, dv), F32),
                        pltpu.VMEM((2 * hpb, 1, dqk), F32), pltpu.VMEM((2 * hpb, 1, 1), F32)],
        compiler_params=_cparams("parallel", "parallel"),
        name="mlstm",
    )(qkvo, qkvo, qkvo, qkvo, g_rows, g_cols, norm_w.reshape(1, heads * dv), cc, ada_w,
      ada_b.reshape(depth, 1, n6))


def _attn_kernel(sink_ref, q_ref, k_ref, v_ref, o_ref, bias_ref, *, lc, nb, group, hd):
    kv = pl.program_id(1)
    blk = AT_BLOCK
    rows = group * blk
    kc = k_ref[0, 0:lc, :]
    vc = v_ref[0, 0:lc, :]
    rid = lax.broadcasted_iota(jnp.int32, (rows, 1), 0)
    sink_col = jnp.zeros((rows, 1), F32)
    for g in range(group):
        sink_col = jnp.where(rid // blk == g, sink_ref[kv * group + g], sink_col)
    qrow = lax.broadcasted_iota(jnp.int32, (rows, 3 * blk), 0) % blk
    kcol = lax.broadcasted_iota(jnp.int32, (rows, 3 * blk), 1)
    for back in range(3):
        rel = kcol - back * blk - qrow
        bias_ref[back] = jnp.where(jnp.abs(rel) <= AT_WINDOW, 0.0, NEG_INF)

    def body(j, carry):
        ws = jnp.clip(j - 1, 0, nb - 3)
        qs = pl.multiple_of(lc + j * blk, blk)
        ks = pl.multiple_of(lc + ws * blk, blk)
        q4 = q_ref[0, pl.ds(qs, blk), :]
        qq = jnp.concatenate([q4[:, g * hd:(g + 1) * hd] for g in range(group)], axis=0)
        kw = k_ref[0, pl.ds(ks, 3 * blk), :]
        vw = v_ref[0, pl.ds(ks, 3 * blk), :]
        s_loc = lax.dot_general(qq, kw, NT_DIMS, preferred_element_type=F32)
        s_ctx = lax.dot_general(qq, kc, NT_DIMS, preferred_element_type=F32)
        s_loc = s_loc + bias_ref[j - ws]
        m = jnp.maximum(jnp.maximum(jnp.max(s_loc, axis=1, keepdims=True),
                                    jnp.max(s_ctx, axis=1, keepdims=True)), sink_col)
        p_loc = jnp.exp(s_loc - m)
        p_ctx = jnp.exp(s_ctx - m)
        den = (jnp.sum(p_loc, axis=1, keepdims=True) + jnp.sum(p_ctx, axis=1, keepdims=True)
               + jnp.exp(sink_col - m))
        o = (jnp.dot(p_loc.astype(BF16), vw, preferred_element_type=F32)
             + jnp.dot(p_ctx.astype(BF16), vc, preferred_element_type=F32)) / den
        os = pl.multiple_of(j * blk, blk)
        for g in range(group):
            o_ref[0, pl.ds(os, blk), g * hd:(g + 1) * hd] = o[g * blk:(g + 1) * blk].astype(o_ref.dtype)
        return carry

    lax.fori_loop(0, nb, body, 0, unroll=2)


def window_attention(qkv3, sink, lc, n, heads, kv_heads, hd):
    bsz = qkv3.shape[0]
    group = heads // kv_heads
    nb = n // AT_BLOCK
    assert nb >= 3
    koff = heads
    voff = heads + kv_heads
    return pl.pallas_call(
        functools.partial(_attn_kernel, lc=lc, nb=nb, group=group, hd=hd),
        grid=(bsz, kv_heads),
        in_specs=[
            pl.BlockSpec(memory_space=pltpu.SMEM),
            pl.BlockSpec((1, lc + n, group * hd), lambda b, g: (b, 0, g)),
            pl.BlockSpec((1, lc + n, hd), lambda b, g: (b, 0, koff + g)),
            pl.BlockSpec((1, lc + n, hd), lambda b, g: (b, 0, voff + g)),
        ],
        out_specs=pl.BlockSpec((1, n, group * hd), lambda b, g: (b, 0, g)),
        out_shape=jax.ShapeDtypeStruct((bsz, n, heads * hd), BF16),
        scratch_shapes=[pltpu.VMEM((3, group * AT_BLOCK, 3 * AT_BLOCK), F32)],
        compiler_params=_cparams("parallel", "parallel"),
        name="window_attention",
    )(sink, qkv3, qkv3, qkv3)


def _ln_kernel(*refs, dual, gather, tile, has_h, has_router, alpha):
    refs = list(refs)
    if gather:
        pos_ref = refs.pop(0)
    x_ref = refs.pop(0)
    c_ref = refs.pop(0) if dual else None
    y_ref = refs.pop(0)
    gw_ref = refs.pop(0) if gather else None
    gate_ref, lng_ref, lnb_ref = refs.pop(0), refs.pop(0), refs.pop(0)
    if has_h:
        sh_ref, sc_ref = refs.pop(0), refs.pop(0)
    if has_router:
        rw_ref = refs.pop(0)
    xo_ref = refs.pop(0)
    if has_h:
        ho_ref = refs.pop(0)
    if has_router:
        lg_ref = refs.pop(0)

    if gather:
        buf_ref, sems = refs.pop(0), refs.pop(0)
        lin = pl.program_id(0) * pl.num_programs(1) + pl.program_id(1)
        ntiles = pl.num_programs(0) * pl.num_programs(1)
        ntok = pos_ref.shape[0] // 2

        def issue(t):
            slot = t % 2

            def start(r, carry):
                for k in range(2):
                    src = pos_ref[k * ntok + t * tile + r]
                    pltpu.make_async_copy(y_ref.at[src], buf_ref.at[slot, k, r], sems.at[slot]).start(priority=k)
                return carry

            lax.fori_loop(0, tile, start, 0, unroll=4)

        @pl.when(lin == 0)
        def _():
            issue(lin)

        @pl.when(lin + 1 < ntiles)
        def _():
            issue(lin + 1)

        slot = lin % 2
        for k in range(2):
            pltpu.make_async_copy(y_ref.at[pl.ds(0, tile)], buf_ref.at[slot, k], sems.at[slot]).wait()
        gw = gw_ref[...]
        d = x_ref.shape[-1]
        y = (gw[:, 0:1] * buf_ref[slot, 0].reshape(tile, d).astype(F32)
             + gw[:, 1:2] * buf_ref[slot, 1].reshape(tile, d).astype(F32))
    else:
        y = y_ref[...].astype(F32)

    if dual:
        x = jnp.where(pl.program_id(1) == 0, c_ref[0], x_ref[0])
    else:
        x = x_ref[...]
    z = alpha * x + gate_ref[0] * y
    mu = jnp.mean(z, axis=1, keepdims=True)
    zc = z - mu
    var = jnp.mean(zc * zc, axis=1, keepdims=True)
    xn = zc * lax.rsqrt(var + LN_EPS) * lng_ref[...] + lnb_ref[...]
    xo_ref[...] = xn
    if has_h:
        h = xn * (1.0 + sc_ref[0]) + sh_ref[0]
        ho_ref[...] = h.reshape(ho_ref.shape).astype(ho_ref.dtype)
        if has_router:
            lg_ref[...] = lax.dot_general(rw_ref[...], h, NT_DIMS, preferred_element_type=F32,
                                          precision=HIGHEST)


def ln_block(x, y, ada3, gate_col, lng, lnb, *, alpha, nb, per, tile, ctx_rows, ctx=None, x_per=None,
             x_off=0, nxt=None, h_dtype=None, h_rows3d=False, router_wt=None, pos=None, pos_w=None):
    d = x.shape[-1]
    dual = ctx is not None
    gather = pos is not None
    rows = nb * per * tile
    prow = (lambda b, j: jnp.where(j == 0, nb, b)) if ctx_rows else (lambda b, j: b)

    def im(f):
        return lambda b, j, *_: f(b, j)

    in_specs, args = [], []
    if dual:
        in_specs.append(pl.BlockSpec((1, tile, d), im(lambda b, j: (b, jnp.maximum(j - 1, 0), 0))))
        in_specs.append(pl.BlockSpec((1, tile, d), im(lambda b, j: (b, 0, 0))))
        args += [x, ctx]
    else:
        in_specs.append(pl.BlockSpec((tile, d), im(lambda b, j: (b * x_per + x_off + j, 0))))
        args.append(x)
    if gather:
        in_specs.append(pl.BlockSpec(memory_space=pl.ANY))
        in_specs.append(pl.BlockSpec((tile, 2), im(lambda b, j: (b * per + j, 0))))
        args += [y, pos_w]
    else:
        in_specs.append(pl.BlockSpec((tile, d), im(lambda b, j: (b * per + j, 0))))
        args.append(y)
    in_specs.append(pl.BlockSpec((1, 1, d), im(lambda b, j: (prow(b, j), 0, gate_col))))
    in_specs.append(pl.BlockSpec((1, d), im(lambda b, j: (0, 0))))
    in_specs.append(pl.BlockSpec((1, d), im(lambda b, j: (0, 0))))
    args += [ada3, lng.reshape(1, d), lnb.reshape(1, d)]
    if nxt is not None:
        ada_n, shc, scc = nxt
        in_specs.append(pl.BlockSpec((1, 1, d), im(lambda b, j: (prow(b, j), 0, shc))))
        in_specs.append(pl.BlockSpec((1, 1, d), im(lambda b, j: (prow(b, j), 0, scc))))
        args += [ada_n, ada_n]
    if router_wt is not None:
        ne = router_wt.shape[0]
        in_specs.append(pl.BlockSpec((ne, d), im(lambda b, j: (0, 0))))
        args.append(router_wt)

    out_specs = [pl.BlockSpec((tile, d), im(lambda b, j: (b * per + j, 0)))]
    out_shape = [jax.ShapeDtypeStruct((rows, d), F32)]
    if nxt is not None and h_rows3d:
        out_specs.append(pl.BlockSpec((tile, d // LANES, LANES), im(lambda b, j: (b * per + j, 0, 0))))
        out_shape.append(jax.ShapeDtypeStruct((rows, d // LANES, LANES), h_dtype))
    elif nxt is not None:
        out_specs.append(pl.BlockSpec((tile, d), im(lambda b, j: (b * per + j, 0))))
        out_shape.append(jax.ShapeDtypeStruct((rows, d), h_dtype))
    if router_wt is not None:
        out_specs.append(pl.BlockSpec((ne, tile), im(lambda b, j: (0, b * per + j))))
        out_shape.append(jax.ShapeDtypeStruct((ne, rows), F32))

    scratch = [pltpu.VMEM((2, 2, tile) + y.shape[1:], y.dtype), pltpu.SemaphoreType.DMA((2,))] if gather else []
    kern = functools.partial(_ln_kernel, dual=dual, gather=gather, tile=tile, has_h=nxt is not None,
                             has_router=router_wt is not None, alpha=alpha)
    call = pl.pallas_call(
        kern,
        grid_spec=pltpu.PrefetchScalarGridSpec(
            num_scalar_prefetch=1 if gather else 0,
            grid=(nb, per),
            in_specs=in_specs,
            out_specs=out_specs,
            scratch_shapes=scratch,
        ),
        out_shape=out_shape,
        compiler_params=_cparams("arbitrary", "arbitrary"),
        name="ln_gather" if gather else "ln_block",
    )
    return call(*(([pos] if gather else []) + args))


def _router_kernel(b_ref, l_ref, g_ref, m_ref, *, ne):
    epg = ne // N_GROUPS
    logits = l_ref[...]
    s = [_sigmoid(logits[e:e + 1, :]) for e in range(ne)]
    sel = [s[e] + b_ref[e] for e in range(ne)]
    gscore = []
    for g in range(N_GROUPS):
        v = sel[g * epg:(g + 1) * epg]
        best = None
        for i in range(epg):
            for j in range(i + 1, epg):
                pair = v[i] + v[j]
                best = pair if best is None else jnp.maximum(best, pair)
        gscore.append(best)
    chosen = []
    for g in range(N_GROUPS):
        ok = None
        for o in range(N_GROUPS):
            if o == g:
                continue
            t = (gscore[g] > gscore[o]) if o < g else (gscore[g] >= gscore[o])
            ok = t if ok is None else jnp.logical_and(ok, t)
        chosen.append(ok)
    mask = []
    for g in range(N_GROUPS):
        v = sel[g * epg:(g + 1) * epg]
        for i in range(epg):
            rank = jnp.zeros_like(v[i])
            for j in range(epg):
                if j == i:
                    continue
                ahead = (v[j] >= v[i]) if j < i else (v[j] > v[i])
                rank = rank + jnp.where(ahead, 1.0, 0.0)
            mask.append(jnp.logical_and(chosen[g], rank < TOP_K))
    wsum = jnp.zeros_like(s[0])
    for e in range(ne):
        wsum = wsum + jnp.where(mask[e], s[e], 0.0)
    for e in range(ne):
        g_ref[e:e + 1, :] = jnp.where(mask[e], s[e] / wsum, 0.0)
        m_ref[e:e + 1, :] = jnp.where(mask[e], 1, 0).astype(jnp.int32)


def router(logits_t, router_b):
    ne, t = logits_t.shape
    tl = _pick(t, (1024, 512, 256, 128))
    spec = pl.BlockSpec((ne, tl), lambda i: (0, i))
    return pl.pallas_call(
        functools.partial(_router_kernel, ne=ne),
        grid=(t // tl,),
        in_specs=[pl.BlockSpec(memory_space=pltpu.SMEM), spec],
        out_specs=[spec, spec],
        out_shape=[jax.ShapeDtypeStruct((ne, t), F32), jax.ShapeDtypeStruct((ne, t), jnp.int32)],
        compiler_params=_cparams("parallel"),
        name="router",
    )(router_b, logits_t)


def dispatch_plan(gates, mask, tm):
    ne, t = mask.shape
    n_tiles = (TOP_K * t) // tm + ne
    rows = n_tiles * tm
    sel = mask > 0
    counts = jnp.sum(mask, axis=1)
    ptiles = (counts + tm - 1) // tm
    tile_end = jnp.cumsum(ptiles)
    off = (tile_end - ptiles) * tm
    dest = off[:, None] + jnp.cumsum(mask, axis=1) - mask
    pos_lo = jnp.min(jnp.where(sel, dest, rows), axis=0)
    pos_hi = jnp.max(jnp.where(sel, dest, -1), axis=0)
    g_lo = jnp.sum(jnp.where(sel & (dest == pos_lo[None, :]), gates, 0.0), axis=0)
    g_hi = jnp.sum(jnp.where(sel & (dest == pos_hi[None, :]), gates, 0.0), axis=0)
    pos = jnp.concatenate([pos_lo, pos_hi]).astype(jnp.int32)
    n_valid = tile_end[-1]
    tiles = jnp.arange(n_tiles, dtype=jnp.int32)
    tid = jnp.minimum(tiles, n_valid - 1)
    tile_expert = jnp.minimum(jnp.sum(tile_end[None, :] <= tid[:, None], axis=1), ne - 1).astype(jnp.int32)
    prev = jnp.concatenate([jnp.full((1,), -1, jnp.int32), tile_expert[:-1]])
    tile_first = (tile_expert != prev).astype(jnp.int32)
    last = jnp.any((tiles[:, None] == tile_end[None, :] - 1) & (ptiles[None, :] > 0), axis=1)
    tile_fill = (last | (tiles >= n_valid)).astype(jnp.int32)
    nv = jnp.reshape(n_valid, (1,)).astype(jnp.int32)
    return tile_fill, jnp.stack([g_lo, g_hi], axis=1), pos, tid, tile_expert, tile_first, nv


def _scatter_kernel(pos_ref, fill_ref, x_ref, o_ref, stage_ref, zero_ref, sems, zsem, *, tile, tm):
    i = pl.program_id(0)
    n = pl.num_programs(0)
    ntok = pos_ref.shape[0] // 2
    slot = i % 2

    def slot_wait(s):
        for _ in range(2):
            pltpu.make_async_copy(stage_ref.at[s], o_ref.at[pl.ds(0, tile)], sems.at[s]).wait()

    @pl.when(i == 0)
    def _():
        zero_ref[...] = jnp.zeros_like(zero_ref)

        def fill(t, wait):
            @pl.when(fill_ref[t] == 1)
            def _():
                cp = pltpu.make_async_copy(zero_ref, o_ref.at[pl.ds(t * tm, tm)], zsem)
                cp.wait() if wait else cp.start()

        lax.fori_loop(0, fill_ref.shape[0], lambda t, c: (fill(t, False), c)[1], 0)
        lax.fori_loop(0, fill_ref.shape[0], lambda t, c: (fill(t, True), c)[1], 0)

    @pl.when(i >= 2)
    def _():
        slot_wait(slot)

    stage_ref[slot] = x_ref[...]

    def start(r, carry):
        for k in range(2):
            dst = pos_ref[k * ntok + i * tile + r]
            pltpu.make_async_copy(stage_ref.at[slot, r], o_ref.at[dst], sems.at[slot]).start(priority=k)
        return carry

    lax.fori_loop(0, tile, start, 0, unroll=4)

    @pl.when(i == n - 1)
    def _():
        slot_wait(slot)

        @pl.when(n >= 2)
        def _():
            slot_wait(1 - slot)


def scatter_rows(h, pos, tile_fill, tm, tile):
    t = h.shape[0]
    rows = tile_fill.shape[0] * tm
    return pl.pallas_call(
        functools.partial(_scatter_kernel, tile=tile, tm=tm),
        grid_spec=pltpu.PrefetchScalarGridSpec(
            num_scalar_prefetch=2,
            grid=(t // tile,),
            in_specs=[pl.BlockSpec((tile,) + h.shape[1:], lambda i, p, f: (i, 0, 0))],
            out_specs=pl.BlockSpec(memory_space=pl.ANY),
            scratch_shapes=[pltpu.VMEM((2, tile) + h.shape[1:], h.dtype), pltpu.VMEM((tm,) + h.shape[1:], h.dtype),
                            pltpu.SemaphoreType.DMA((2,)), pltpu.SemaphoreType.DMA(())],
        ),
        out_shape=jax.ShapeDtypeStruct((rows,) + h.shape[1:], h.dtype),
        compiler_params=_cparams("arbitrary"),
        name="moe_scatter",
    )(pos, tile_fill, h)


def _ffn_a_kernel(tid_ref, te_ref, tf_ref, nv_ref, x_ref, w1_ref, w3_ref, o_ref, w1b, w3b):
    i = pl.program_id(1)

    @pl.when(i < nv_ref[0])
    def _():
        @pl.when(tf_ref[i] == 1)
        def _():
            w1b[...] = w1_ref[0].astype(BF16)
            w3b[...] = w3_ref[0].astype(BF16)

        x = x_ref[...].reshape(x_ref.shape[0], -1)
        a = jnp.dot(x, w1b[...], preferred_element_type=F32)
        b = jnp.dot(x, w3b[...], preferred_element_type=F32)
        o_ref[...] = (a * _sigmoid(a) * b).astype(o_ref.dtype)

    @pl.when(i >= nv_ref[0])
    def _():
        o_ref[...] = jnp.zeros_like(o_ref)


def ffn_stage_a(xs, w1, w3, layer, plan, tm):
    _, _, _, tid, te, tf, nv = plan
    rows = xs.shape[0]
    d = xs.shape[1] * xs.shape[2]
    ff = w1.shape[3]
    tfc = _pick(ff, (512, 256, 128))
    wspec = pl.BlockSpec((None, 1, d, tfc), lambda f, i, tid, te, tf, nv: (layer, te[i], 0, f))
    return pl.pallas_call(
        _ffn_a_kernel,
        grid_spec=pltpu.PrefetchScalarGridSpec(
            num_scalar_prefetch=4,
            grid=(ff // tfc, rows // tm),
            in_specs=[pl.BlockSpec((tm,) + xs.shape[1:], lambda f, i, tid, te, tf, nv: (tid[i], 0, 0)), wspec, wspec],
            out_specs=pl.BlockSpec((tm, tfc), lambda f, i, tid, te, tf, nv: (i, f)),
            scratch_shapes=[pltpu.VMEM((d, tfc), BF16), pltpu.VMEM((d, tfc), BF16)],
        ),
        out_shape=jax.ShapeDtypeStruct((rows, ff), BF16),
        compiler_params=_cparams("arbitrary", "arbitrary"),
        name="moe_ffn_a",
    )(tid, te, tf, nv, xs, w1, w3)


def _ffn_b_kernel(tid_ref, te_ref, tf_ref, nv_ref, h_ref, w2_ref, o_ref, w2b):
    i = pl.program_id(1)

    @pl.when(i < nv_ref[0])
    def _():
        @pl.when(tf_ref[i] == 1)
        def _():
            w2b[...] = w2_ref[0].astype(BF16)

        y = jnp.dot(h_ref[...], w2b[...], preferred_element_type=F32)
        o_ref[...] = y.reshape(o_ref.shape).astype(o_ref.dtype)

    @pl.when(i >= nv_ref[0])
    def _():
        o_ref[...] = jnp.zeros_like(o_ref)


def ffn_stage_b(hs, w2, layer, plan, tm):
    _, _, _, tid, te, tf, nv = plan
    rows, ff = hs.shape
    d = w2.shape[3]
    tn = _pick(d, (2048, 1024, 512, 256, 128))
    return pl.pallas_call(
        _ffn_b_kernel,
        grid_spec=pltpu.PrefetchScalarGridSpec(
            num_scalar_prefetch=4,
            grid=(d // tn, rows // tm),
            in_specs=[
                pl.BlockSpec((tm, ff), lambda n, i, tid, te, tf, nv: (tid[i], 0)),
                pl.BlockSpec((None, 1, ff, tn), lambda n, i, tid, te, tf, nv: (layer, te[i], 0, n)),
            ],
            out_specs=pl.BlockSpec((tm, tn // LANES, LANES), lambda n, i, tid, te, tf, nv: (i, n, 0)),
            scratch_shapes=[pltpu.VMEM((ff, tn), BF16)],
        ),
        out_shape=jax.ShapeDtypeStruct((rows, d // LANES, LANES), BF16),
        compiler_params=_cparams("arbitrary", "arbitrary"),
        name="moe_ffn_b",
    )(tid, te, tf, nv, hs, w2)


def moe_sparse(h, logits_t, router_b, w1, w3, w2, layer, tile):
    gates, mask = router(logits_t, router_b)
    plan = dispatch_plan(gates, mask, MOE_TILE)
    xs = scatter_rows(h, plan[2], plan[0], MOE_TILE, tile)
    hs = ffn_stage_a(xs, w1, w3, layer, plan, MOE_TILE)
    ys = ffn_stage_b(hs, w2, layer, plan, MOE_TILE)
    return ys, plan[2], plan[1]


def rope_tables(n, lc, hd):
    half = hd // 2
    t = jnp.arange(n)
    inv = jnp.power(ROPE_BASE, -jnp.arange(0, half, 2, dtype=F32) / half)
    ang_r = (t // GRID_W).astype(F32)[:, None] * inv
    ang_c = (t % GRID_W).astype(F32)[:, None] * inv
    cos = jnp.concatenate([jnp.cos(ang_r)] * 2 + [jnp.cos(ang_c)] * 2, axis=1)
    sin = jnp.concatenate([-jnp.sin(ang_r), jnp.sin(ang_r), -jnp.sin(ang_c), jnp.sin(ang_c)], axis=1)
    cos = jnp.concatenate([jnp.ones((lc, hd), F32), cos], axis=0)
    sin = jnp.concatenate([jnp.zeros((lc, hd), F32), sin], axis=0)
    scale = hd ** -0.5
    ones, zeros = jnp.ones_like(cos), jnp.zeros_like(sin)
    return jnp.stack([cos * scale, cos, ones]), jnp.stack([sin * scale, sin, zeros])


def kernel(x, c, ctx, c_ctx, ada_w, ada_b, ln_g, ln_b, mlstm_w_in, mlstm_b_gates, mlstm_norm_w, mlstm_w_out,
           attn_w_in, attn_b_in, attn_sink, attn_w_out, router_w, router_b, moe_w1, moe_w3, moe_w2):
    bsz, n, d = x.shape
    lc = ctx.shape[1]
    depth = ada_w.shape[0]
    assert depth == 2 and bsz <= 4 and n % lc == 0 and lc % ML_CHUNK == 0
    alpha = (2 * depth) ** 0.25
    tile = lc
    per = (n + lc) // tile
    per_lat = n // tile

    cc = jnp.concatenate([c, c_ctx[None, :], jnp.zeros((8 - bsz - 1, d), F32)], axis=0)
    ada0 = adaln(cc, ada_w, ada_b, 0).reshape(8, 1, 6 * d)
    rwt = router_w.T

    heads = mlstm_b_gates.shape[1] // 4
    dv = d // heads
    dqk = dv // 2
    nmain = 2 * heads * (dqk + dv)
    w_in_t = jnp.swapaxes(mlstm_w_in, 1, 2)
    h0, gates_pre = mod0(x, ctx, ada0, w_in_t[0, nmain:, :], mlstm_b_gates[0][None, :])
    qkvo = matmul(h0, w_in_t, n_cols=nmain, w_t=True)
    ncc, nct = lc // ML_CHUNK, (n + lc) // ML_CHUNK
    hg, ada1 = mlstm_bidir(qkvo, gates_pre, mlstm_norm_w[0], bsz, heads, dqk, dv, ncc, nct, cc, ada_w, ada_b, 1)
    ada1 = ada1.reshape(8, 1, 6 * d)
    y = matmul(hg, mlstm_w_out, out_dtype=F32)
    x1, hffn, lg = ln_block(x, y, ada0, 2, ln_g[0, 0], ln_b[0, 0], alpha=alpha, nb=bsz, per=per, tile=tile,
                            ctx_rows=True, ctx=ctx, nxt=(ada0, 3, 4), h_dtype=BF16, h_rows3d=True, router_wt=rwt)
    ys, pos, pos_w = moe_sparse(hffn, lg, router_b, moe_w1, moe_w3, moe_w2, 0, tile)
    x2, h1 = ln_block(x1, ys, ada0, 5, ln_g[0, 1], ln_b[0, 1], alpha=alpha, nb=bsz, per=per, tile=tile,
                      ctx_rows=True, x_per=per, nxt=(ada1, 0, 1), h_dtype=BF16, pos=pos, pos_w=pos_w)

    heads_a = attn_sink.shape[1]
    hd = d // heads_a
    kvh = (attn_w_in.shape[2] // hd - heads_a) // 2
    cos_t, sin_t = rope_tables(n, lc, hd)
    q_end, k_end = heads_a * hd, (heads_a + kvh) * hd
    type_of = lambda col: jnp.where(col < q_end, 0, jnp.where(col < k_end, 1, 2))
    tm1 = _pick(n + lc, (1152, 768, 384, 256, 128))
    qkv = matmul(h1, attn_w_in, bias=attn_b_in[0], rope=(cos_t, sin_t, n + lc, type_of), tm=tm1,
                 tn=_pick(kvh * hd, (512, 256, 128)))
    att = window_attention(qkv.reshape(bsz, n + lc, -1), attn_sink[0], lc, n, heads_a, kvh, hd)
    y = matmul(att.reshape(bsz * n, heads_a * hd), attn_w_out, out_dtype=F32)
    x3, hffn, lg = ln_block(x2, y, ada1, 2, ln_g[1, 0], ln_b[1, 0], alpha=alpha, nb=bsz, per=per_lat, tile=tile,
                            ctx_rows=False, x_per=per, x_off=1, nxt=(ada1, 3, 4), h_dtype=BF16, h_rows3d=True,
                            router_wt=rwt)
    ys, pos, pos_w = moe_sparse(hffn, lg, router_b, moe_w1, moe_w3, moe_w2, 1, tile)
    (x4,) = ln_block(x3, ys, ada1, 5, ln_g[1, 1], ln_b[1, 1], alpha=alpha, nb=bsz, per=per_lat, tile=tile,
                     ctx_rows=False, x_per=per_lat, pos=pos, pos_w=pos_w)
    return x4.reshape(bsz, n, d)
```

```python
import functools

import jax
import jax.numpy as jnp
from jax import lax
from jax.experimental import pallas as pl
from jax.experimental.pallas import tpu as pltpu

F32 = jnp.float32
BF16 = jnp.bfloat16
NEG_INF = float("-inf")

GRID_W = 64
ML_CHUNK = 128
AT_WINDOW = 128
AT_BLOCK = 128
ROPE_BASE = 10000.0
N_GROUPS = 4
TOP_K = 2
LN_EPS = 1e-5
RMS_EPS = 1e-6

LANES = 128
VMEM_LIMIT = 56 * 1024 * 1024
MOE_TILE = 512
HIGHEST = lax.Precision.HIGHEST
NT_DIMS = (((1,), (1,)), ((), ()))


def _cparams(*sem):
    return pltpu.CompilerParams(dimension_semantics=sem, vmem_limit_bytes=VMEM_LIMIT)


def _pick(n, candidates):
    for c in candidates:
        if n % c == 0:
            return c
    return n


def _sigmoid(x):
    return 1.0 / (1.0 + jnp.exp(-x))


def _log_sigmoid(x):
    return jnp.minimum(x, 0.0) - jnp.log1p(jnp.exp(-jnp.abs(x)))


def _adaln_block(c_ref, w_ref, b_ref):
    c = c_ref[...]
    a = (c * _sigmoid(c)).astype(BF16)
    return jnp.dot(a, w_ref[...].astype(BF16), preferred_element_type=F32) + b_ref[...]


def _adaln_kernel(c_ref, w_ref, b_ref, o_ref):
    o_ref[...] = _adaln_block(c_ref, w_ref, b_ref)


def adaln(cc, ada_w, ada_b, layer):
    depth, d, n6 = ada_w.shape
    tn = _pick(n6, (512, 256, 128))
    return pl.pallas_call(
        _adaln_kernel,
        grid=(n6 // tn,),
        in_specs=[
            pl.BlockSpec((8, d), lambda j: (0, 0)),
            pl.BlockSpec((None, d, tn), lambda j: (layer, 0, j)),
            pl.BlockSpec((None, 1, tn), lambda j: (layer, 0, j)),
        ],
        out_specs=pl.BlockSpec((8, tn), lambda j: (0, j)),
        out_shape=jax.ShapeDtypeStruct((8, n6), F32),
        compiler_params=_cparams("parallel"),
        name="adaln",
    )(cc, ada_w, ada_b.reshape(depth, 1, n6))


def _mod0_kernel(x_ref, c_ref, sh_ref, sc_ref, wg_ref, bg_ref, h_ref, g_ref):
    j = pl.program_id(1)
    t = jnp.where(j == 0, c_ref[0], x_ref[0])
    h = t * (1.0 + sc_ref[0]) + sh_ref[0]
    h_ref[...] = h.astype(BF16)
    g_ref[...] = lax.dot_general(h, wg_ref[...], NT_DIMS, preferred_element_type=F32,
                                 precision=HIGHEST) + bg_ref[...]


def mod0(x, ctx, ada3, wg, bg):
    bsz, n, d = x.shape
    lc = ctx.shape[1]
    per = (n + lc) // lc
    ng = wg.shape[0]
    prow = lambda b, j: jnp.where(j == 0, bsz, b)
    return pl.pallas_call(
        _mod0_kernel,
        grid=(bsz, per),
        in_specs=[
            pl.BlockSpec((1, lc, d), lambda b, j: (b, jnp.maximum(j - 1, 0), 0)),
            pl.BlockSpec((1, lc, d), lambda b, j: (b, 0, 0)),
            pl.BlockSpec((1, 1, d), lambda b, j: (prow(b, j), 0, 0)),
            pl.BlockSpec((1, 1, d), lambda b, j: (prow(b, j), 0, 1)),
            pl.BlockSpec((ng, d), lambda b, j: (0, 0)),
            pl.BlockSpec((1, ng), lambda b, j: (0, 0)),
        ],
        out_specs=[
            pl.BlockSpec((lc, d), lambda b, j: (b * per + j, 0)),
            pl.BlockSpec((lc, ng), lambda b, j: (b * per + j, 0)),
        ],
        out_shape=[
            jax.ShapeDtypeStruct((bsz * (n + lc), d), BF16),
            jax.ShapeDtypeStruct((bsz * (n + lc), ng), F32),
        ],
        compiler_params=_cparams("parallel", "arbitrary"),
        name="mod0",
    )(x, ctx, ada3, ada3, wg, bg)


def _mm_kernel(*refs, has_bias, has_rope, tn, w_t):
    x_ref, w_ref = refs[0], refs[1]
    pos = 2
    if w_t:
        acc = lax.dot_general(x_ref[...], w_ref[...].astype(BF16), NT_DIMS, preferred_element_type=F32)
    else:
        acc = jnp.dot(x_ref[...], w_ref[...].astype(BF16), preferred_element_type=F32)
    if has_bias:
        acc = acc + refs[pos][...]
        pos += 1
    o_ref = refs[-1]
    if has_rope:
        ca = refs[pos][0]
        sb = refs[pos + 1][0]
        lane = lax.broadcasted_iota(jnp.int32, ca.shape, 1)
        first = (lane % 64) < 32
        for h in range(tn // LANES):
            y = acc[:, h * LANES:(h + 1) * LANES]
            sw = jnp.where(first, pltpu.roll(y, 96, 1), pltpu.roll(y, 32, 1))
            o_ref[:, h * LANES:(h + 1) * LANES] = (y * ca + sw * sb).astype(o_ref.dtype)
    else:
        o_ref[...] = acc.astype(o_ref.dtype)


def matmul(x, w, bias=None, rope=None, out_dtype=BF16, tm=None, tn=None, n_cols=None, w_t=False):
    m, k = x.shape
    n = n_cols or w.shape[1 if w_t else 2]
    tm = tm or _pick(m, (1152, 1024, 768, 512, 256, 128))
    tn = tn or _pick(n, (512, 256, 128))
    in_specs = [
        pl.BlockSpec((tm, k), lambda i, j: (i, 0)),
        pl.BlockSpec((None, tn, k), lambda i, j: (0, j, 0)) if w_t else
        pl.BlockSpec((None, k, tn), lambda i, j: (0, 0, j)),
    ]
    args = [x, w]
    if bias is not None:
        in_specs.append(pl.BlockSpec((1, tn), lambda i, j: (0, j)))
        args.append(bias.reshape(1, n))
    if rope is not None:
        ca, sb, period, type_of = rope
        nper = period // tm
        spec = pl.BlockSpec((1, tm, LANES), lambda i, j: (type_of(j * tn), i % nper, 0))
        in_specs += [spec, spec]
        args += [ca, sb]
    return pl.pallas_call(
        functools.partial(_mm_kernel, has_bias=bias is not None, has_rope=rope is not None, tn=tn, w_t=w_t),
        grid=(m // tm, n // tn),
        in_specs=in_specs,
        out_specs=pl.BlockSpec((tm, tn), lambda i, j: (i, j)),
        out_shape=jax.ShapeDtypeStruct((m, n), out_dtype),
        compiler_params=_cparams("parallel", "arbitrary"),
        name="matmul",
    )(*args)


def _mlstm_chunks(chains, scale):
    L = ML_CHUNK
    r = lax.broadcasted_iota(jnp.int32, (L, L), 0)
    c = lax.broadcasted_iota(jnp.int32, (L, L), 1)
    gate = []
    for q, k, v, ig_row, f_row, ig_col, f_col, st, n_row, m_prev, d in chains:
        reverse = d == 1
        lf_row = _log_sigmoid(f_row)
        lf_col = _log_sigmoid(f_col)
        keep = (c >= r) if reverse else (c <= r)
        keep_t = (r >= c) if reverse else (r <= c)
        b_col = jnp.sum(jnp.where(keep, lf_row, 0.0), axis=1, keepdims=True)
        b_row = jnp.sum(jnp.where(keep_t, lf_col, 0.0), axis=0, keepdims=True)
        logw = jnp.where(keep, b_col - b_row + ig_row, NEG_INF)
        inter = b_col + m_prev
        m_t = jnp.maximum(inter, jnp.max(logw, axis=1, keepdims=True))
        w_inter = jnp.exp(inter - m_t)
        p = jnp.exp(logw - m_t)
        b_end = b_col[0:1, :] if reverse else b_col[L - 1:L, :]
        m_new = jnp.maximum(b_end + m_prev, jnp.max(b_end - b_row + ig_row, axis=1, keepdims=True))
        decay = jnp.exp(b_end + m_prev - m_new)
        w_src = jnp.exp(b_end - b_col + ig_col - m_new)
        gate.append((m_t, w_inter, p, m_new, decay, w_src))

    qk = [lax.dot_general(ch[0], ch[1], NT_DIMS, preferred_element_type=F32) for ch in chains]
    qs = [jnp.dot(ch[0], ch[7].astype(BF16), preferred_element_type=F32) for ch in chains]
    s = [a * (g[2] * scale) for a, g in zip(qk, gate)]
    sv = [jnp.dot(a.astype(BF16), ch[2], preferred_element_type=F32) for a, ch in zip(s, chains)]
    kw = [ch[1].astype(F32) * g[5] for ch, g in zip(chains, gate)]
    kv = [jnp.dot(a.T.astype(BF16), ch[2], preferred_element_type=F32) for a, ch in zip(kw, chains)]

    out = []
    for ch, g, s_i, qs_i, sv_i, kw_i, kv_i in zip(chains, gate, s, qs, sv, kw, kv):
        q, n_row, st = ch[0], ch[8], ch[7]
        m_t, w_inter, _, m_new, decay, _ = g
        num = (w_inter * scale) * qs_i + sv_i
        qn = jnp.sum(q.astype(F32) * n_row, axis=1, keepdims=True) * scale
        den = w_inter * qn + jnp.sum(s_i, axis=1, keepdims=True)
        h = num * (1.0 / jnp.maximum(jnp.abs(den), jnp.exp(-m_t)))
        out.append((h, decay * st + kv_i, decay * n_row + jnp.sum(kw_i, axis=0, keepdims=True), m_new))
    return out


def _mlstm_kernel(q_ref, k_ref, v_ref, o_ref, gr_ref, gc_ref, nw_ref, cc_ref, aw_ref, ab_ref, out_ref, ada_ref,
                  acc_ref, st_ref, n_ref, m_ref, *, scale, ncc, nct, hpb):
    L = ML_CHUNK
    ada_ref[...] = _adaln_block(cc_ref, aw_ref, ab_ref)
    st_ref[...] = jnp.zeros_like(st_ref)
    n_ref[...] = jnp.zeros_like(n_ref)
    m_ref[...] = jnp.zeros_like(m_ref)
    acc_ref[...] = jnp.zeros_like(acc_ref)

    dqk, dv = st_ref.shape[1], st_ref.shape[2]

    def step(j, carry):
        where, chains = [], []
        for hh in range(hpb):
            for d, cj in ((0, j), (1, jnp.where(j < ncc, ncc - 1 - j, nct - 1 - (j - ncc)))):
                rows = pl.ds(pl.multiple_of(cj * L, L), L)
                slot = 2 * hh + d
                gcol = gc_ref[0, hh, cj]
                where.append((rows, hh, slot, acc_ref[rows, hh * dv:(hh + 1) * dv]))
                chains.append((q_ref[rows, hh * dqk:(hh + 1) * dqk], k_ref[rows, hh * dqk:(hh + 1) * dqk],
                               v_ref[rows, hh * dv:(hh + 1) * dv],
                               gr_ref[0, 2 * d, hh, cj], gr_ref[0, 2 * d + 1, hh, cj],
                               gcol[:, 2 * d:2 * d + 1], gcol[:, 2 * d + 1:2 * d + 2],
                               st_ref[slot], n_ref[slot], m_ref[slot], d))
        for (rows, hh, slot, acc), (h, st_new, n_new, m_new) in zip(where, _mlstm_chunks(chains, scale)):
            acc_ref[rows, hh * dv:(hh + 1) * dv] = acc + h
            st_ref[slot] = st_new
            n_ref[slot] = n_new
            m_ref[slot] = m_new
        return carry

    lax.fori_loop(0, nct, step, 0, unroll=2)

    def finish(cj, carry):
        rows = pl.ds(pl.multiple_of(cj * L, L), L)
        for hh in range(hpb):
            cols = slice(hh * dv, (hh + 1) * dv)
            hs = acc_ref[rows, cols]
            hn = hs * lax.rsqrt(jnp.mean(hs * hs, axis=1, keepdims=True) + RMS_EPS)
            out_ref[rows, cols] = (hn * nw_ref[:, cols]
                                   * _sigmoid(o_ref[rows, cols].astype(F32))).astype(out_ref.dtype)
        return carry

    lax.fori_loop(0, nct, finish, 0)


def mlstm_bidir(qkvo, gates_pre, norm_w, bsz, heads, dqk, dv, ncc, nct, cc, ada_w, ada_b, ada_layer):
    t = qkvo.shape[0]
    L = ML_CHUNK
    n2 = nct * L
    hpb = 1
    depth, d, n6 = ada_w.shape
    nsteps = bsz * heads // hpb
    ta = n6 // nsteps
    assert n6 % nsteps == 0 and ta % LANES == 0
    assert ncc % 2 == 0 and (nct - ncc) % 2 == 0
    g4 = gates_pre.reshape(bsz, nct, L, 4, heads)
    g_rows = jnp.transpose(g4, (0, 3, 4, 1, 2)).reshape(bsz, 4, heads, nct, 1, L)
    g_cols = jnp.transpose(g4, (0, 4, 1, 2, 3))
    koff = heads // hpb
    voff = 2 * heads * dqk // (dv * hpb)
    ooff = voff + heads // hpb
    return pl.pallas_call(
        functools.partial(_mlstm_kernel, scale=dqk ** -0.5, ncc=ncc, nct=nct, hpb=hpb),
        grid=(bsz, heads // hpb),
        in_specs=[
            pl.BlockSpec((n2, hpb * dqk), lambda b, h: (b, h)),
            pl.BlockSpec((n2, hpb * dqk), lambda b, h: (b, koff + h)),
            pl.BlockSpec((n2, hpb * dv), lambda b, h: (b, voff + h)),
            pl.BlockSpec((n2, hpb * dv), lambda b, h: (b, ooff + h)),
            pl.BlockSpec((1, 4, hpb, nct, 1, L), lambda b, h: (b, 0, h, 0, 0, 0)),
            pl.BlockSpec((1, hpb, nct, L, 4), lambda b, h: (b, h, 0, 0, 0)),
            pl.BlockSpec((1, hpb * dv), lambda b, h: (0, h)),
            pl.BlockSpec((8, d), lambda b, h: (0, 0)),
            pl.BlockSpec((None, d, ta), lambda b, h: (ada_layer, 0, b * (heads // hpb) + h)),
            pl.BlockSpec((None, 1, ta), lambda b, h: (ada_layer, 0, b * (heads // hpb) + h)),
        ],
        out_specs=[pl.BlockSpec((n2, hpb * dv), lambda b, h: (b, h)),
                   pl.BlockSpec((8, ta), lambda b, h: (0, b * (heads // hpb) + h))],
        out_shape=[jax.ShapeDtypeStruct((t, heads * dv), BF16), jax.ShapeDtypeStruct((8, n6), F32)],
        scratch_shapes=[pltpu.VMEM((n2, hpb * dv), F32), pltpu.VMEM((2 * hpb, dqk, dv), F32),
                        pltpu.VMEM((2 * hpb, 1, dqk), F32), pltpu.VMEM((2 * hpb, 1, 1), F32)],
        compiler_params=_cparams("parallel", "parallel"),
        name="mlstm",
    )(qkvo, qkvo, qkvo, qkvo, g_rows, g_cols, norm_w.reshape(1, heads * dv), cc, ada_w,
      ada_b.reshape(depth, 1, n6))


def _attn_kernel(sink_ref, q_ref, k_ref, v_ref, o_ref, bias_ref, *, lc, nb, group, hd):
    kv = pl.program_id(1)
    blk = AT_BLOCK
    rows = group * blk
    kc = k_ref[0, 0:lc, :]
    vc = v_ref[0, 0:lc, :]
    rid = lax.broadcasted_iota(jnp.int32, (rows, 1), 0)
    sink_col = jnp.zeros((rows, 1), F32)
    for g in range(group):
        sink_col = jnp.where(rid // blk == g, sink_ref[kv * group + g], sink_col)
    qrow = lax.broadcasted_iota(jnp.int32, (rows, 3 * blk), 0) % blk
    kcol = lax.broadcasted_iota(jnp.int32, (rows, 3 * blk), 1)
    for back in range(3):
        rel = kcol - back * blk - qrow
        bias_ref[back] = jnp.where(jnp.abs(rel) <= AT_WINDOW, 0.0, NEG_INF)

    def body(j, carry):
        ws = jnp.clip(j - 1, 0, nb - 3)
        qs = pl.multiple_of(lc + j * blk, blk)
        ks = pl.multiple_of(lc + ws * blk, blk)
        q4 = q_ref[0, pl.ds(qs, blk), :]
        qq = jnp.concatenate([q4[:, g * hd:(g + 1) * hd] for g in range(group)], axis=0)
        kw = k_ref[0, pl.ds(ks, 3 * blk), :]
        vw = v_ref[0, pl.ds(ks, 3 * blk), :]
        s_loc = lax.dot_general(qq, kw, NT_DIMS, preferred_element_type=F32)
        s_ctx = lax.dot_general(qq, kc, NT_DIMS, preferred_element_type=F32)
        s_loc = s_loc + bias_ref[j - ws]
        m = jnp.maximum(jnp.maximum(jnp.max(s_loc, axis=1, keepdims=True),
                                    jnp.max(s_ctx, axis=1, keepdims=True)), sink_col)
        p_loc = jnp.exp(s_loc - m)
        p_ctx = jnp.exp(s_ctx - m)
        den = (jnp.sum(p_loc, axis=1, keepdims=True) + jnp.sum(p_ctx, axis=1, keepdims=True)
               + jnp.exp(sink_col - m))
        o = (jnp.dot(p_loc.astype(BF16), vw, preferred_element_type=F32)
             + jnp.dot(p_ctx.astype(BF16), vc, preferred_element_type=F32)) / den
        os = pl.multiple_of(j * blk, blk)
        for g in range(group):
            o_ref[0, pl.ds(os, blk), g * hd:(g + 1) * hd] = o[g * blk:(g + 1) * blk].astype(o_ref.dtype)
        return carry

    lax.fori_loop(0, nb, body, 0, unroll=4 if nb % 4 == 0 else 2)


def window_attention(qkv3, sink, lc, n, heads, kv_heads, hd):
    bsz = qkv3.shape[0]
    group = heads // kv_heads
    nb = n // AT_BLOCK
    assert nb >= 3
    koff = heads
    voff = heads + kv_heads
    return pl.pallas_call(
        functools.partial(_attn_kernel, lc=lc, nb=nb, group=group, hd=hd),
        grid=(bsz, kv_heads),
        in_specs=[
            pl.BlockSpec(memory_space=pltpu.SMEM),
            pl.BlockSpec((1, lc + n, group * hd), lambda b, g: (b, 0, g)),
            pl.BlockSpec((1, lc + n, hd), lambda b, g: (b, 0, koff + g)),
            pl.BlockSpec((1, lc + n, hd), lambda b, g: (b, 0, voff + g)),
        ],
        out_specs=pl.BlockSpec((1, n, group * hd), lambda b, g: (b, 0, g)),
        out_shape=jax.ShapeDtypeStruct((bsz, n, heads * hd), BF16),
        scratch_shapes=[pltpu.VMEM((3, group * AT_BLOCK, 3 * AT_BLOCK), F32)],
        compiler_params=_cparams("parallel", "parallel"),
        name="window_attention",
    )(sink, qkv3, qkv3, qkv3)


def _ln_kernel(*refs, dual, gather, tile, has_h, has_router, alpha):
    refs = list(refs)
    if gather:
        pos_ref = refs.pop(0)
    x_ref = refs.pop(0)
    c_ref = refs.pop(0) if dual else None
    y_ref = refs.pop(0)
    gw_ref = refs.pop(0) if gather else None
    gate_ref, lng_ref, lnb_ref = refs.pop(0), refs.pop(0), refs.pop(0)
    if has_h:
        sh_ref, sc_ref = refs.pop(0), refs.pop(0)
    if has_router:
        rw_ref = refs.pop(0)
    xo_ref = refs.pop(0)
    if has_h:
        ho_ref = refs.pop(0)
    if has_router:
        lg_ref = refs.pop(0)

    if gather:
        buf_ref, sems = refs.pop(0), refs.pop(0)
        lin = pl.program_id(0) * pl.num_programs(1) + pl.program_id(1)
        ntiles = pl.num_programs(0) * pl.num_programs(1)
        ntok = pos_ref.shape[0] // 2

        def issue(t):
            slot = t % 2

            def start(r, carry):
                for k in range(2):
                    src = pos_ref[k * ntok + t * tile + r]
                    pltpu.make_async_copy(y_ref.at[src], buf_ref.at[slot, k, r], sems.at[slot]).start(priority=k)
                return carry

            lax.fori_loop(0, tile, start, 0, unroll=4)

        @pl.when(lin == 0)
        def _():
            issue(lin)

        @pl.when(lin + 1 < ntiles)
        def _():
            issue(lin + 1)

        slot = lin % 2
        for k in range(2):
            pltpu.make_async_copy(y_ref.at[pl.ds(0, tile)], buf_ref.at[slot, k], sems.at[slot]).wait()
        gw = gw_ref[...]
        d = x_ref.shape[-1]
        y = (gw[:, 0:1] * buf_ref[slot, 0].reshape(tile, d).astype(F32)
             + gw[:, 1:2] * buf_ref[slot, 1].reshape(tile, d).astype(F32))
    else:
        y = y_ref[...].astype(F32)

    if dual:
        x = jnp.where(pl.program_id(1) == 0, c_ref[0], x_ref[0])
    else:
        x = x_ref[...]
    z = alpha * x + gate_ref[0] * y
    mu = jnp.mean(z, axis=1, keepdims=True)
    zc = z - mu
    var = jnp.mean(zc * zc, axis=1, keepdims=True)
    xn = zc * lax.rsqrt(var + LN_EPS) * lng_ref[...] + lnb_ref[...]
    xo_ref[...] = xn
    if has_h:
        h = xn * (1.0 + sc_ref[0]) + sh_ref[0]
        ho_ref[...] = h.reshape(ho_ref.shape).astype(ho_ref.dtype)
        if has_router:
            lg_ref[...] = lax.dot_general(rw_ref[...], h, NT_DIMS, preferred_element_type=F32,
                                          precision=HIGHEST)


def ln_block(x, y, ada3, gate_col, lng, lnb, *, alpha, nb, per, tile, ctx_rows, ctx=None, x_per=None,
             x_off=0, nxt=None, h_dtype=None, h_rows3d=False, router_wt=None, pos=None, pos_w=None):
    d = x.shape[-1]
    dual = ctx is not None
    gather = pos is not None
    rows = nb * per * tile
    prow = (lambda b, j: jnp.where(j == 0, nb, b)) if ctx_rows else (lambda b, j: b)

    def im(f):
        return lambda b, j, *_: f(b, j)

    in_specs, args = [], []
    if dual:
        in_specs.append(pl.BlockSpec((1, tile, d), im(lambda b, j: (b, jnp.maximum(j - 1, 0), 0))))
        in_specs.append(pl.BlockSpec((1, tile, d), im(lambda b, j: (b, 0, 0))))
        args += [x, ctx]
    else:
        in_specs.append(pl.BlockSpec((tile, d), im(lambda b, j: (b * x_per + x_off + j, 0))))
        args.append(x)
    if gather:
        in_specs.append(pl.BlockSpec(memory_space=pl.ANY))
        in_specs.append(pl.BlockSpec((tile, 2), im(lambda b, j: (b * per + j, 0))))
        args += [y, pos_w]
    else:
        in_specs.append(pl.BlockSpec((tile, d), im(lambda b, j: (b * per + j, 0))))
        args.append(y)
    in_specs.append(pl.BlockSpec((1, 1, d), im(lambda b, j: (prow(b, j), 0, gate_col))))
    in_specs.append(pl.BlockSpec((1, d), im(lambda b, j: (0, 0))))
    in_specs.append(pl.BlockSpec((1, d), im(lambda b, j: (0, 0))))
    args += [ada3, lng.reshape(1, d), lnb.reshape(1, d)]
    if nxt is not None:
        ada_n, shc, scc = nxt
        in_specs.append(pl.BlockSpec((1, 1, d), im(lambda b, j: (prow(b, j), 0, shc))))
        in_specs.append(pl.BlockSpec((1, 1, d), im(lambda b, j: (prow(b, j), 0, scc))))
        args += [ada_n, ada_n]
    if router_wt is not None:
        ne = router_wt.shape[0]
        in_specs.append(pl.BlockSpec((ne, d), im(lambda b, j: (0, 0))))
        args.append(router_wt)

    out_specs = [pl.BlockSpec((tile, d), im(lambda b, j: (b * per + j, 0)))]
    out_shape = [jax.ShapeDtypeStruct((rows, d), F32)]
    if nxt is not None and h_rows3d:
        out_specs.append(pl.BlockSpec((tile, d // LANES, LANES), im(lambda b, j: (b * per + j, 0, 0))))
        out_shape.append(jax.ShapeDtypeStruct((rows, d // LANES, LANES), h_dtype))
    elif nxt is not None:
        out_specs.append(pl.BlockSpec((tile, d), im(lambda b, j: (b * per + j, 0))))
        out_shape.append(jax.ShapeDtypeStruct((rows, d), h_dtype))
    if router_wt is not None:
        out_specs.append(pl.BlockSpec((ne, tile), im(lambda b, j: (0, b * per + j))))
        out_shape.append(jax.ShapeDtypeStruct((ne, rows), F32))

    scratch = [pltpu.VMEM((2, 2, tile) + y.shape[1:], y.dtype), pltpu.SemaphoreType.DMA((2,))] if gather else []
    kern = functools.partial(_ln_kernel, dual=dual, gather=gather, tile=tile, has_h=nxt is not None,
                             has_router=router_wt is not None, alpha=alpha)
    call = pl.pallas_call(
        kern,
        grid_spec=pltpu.PrefetchScalarGridSpec(
            num_scalar_prefetch=1 if gather else 0,
            grid=(nb, per),
            in_specs=in_specs,
            out_specs=out_specs,
            scratch_shapes=scratch,
        ),
        out_shape=out_shape,
        compiler_params=_cparams("arbitrary", "arbitrary"),
        name="ln_gather" if gather else "ln_block",
    )
    return call(*(([pos] if gather else []) + args))


def _router_kernel(b_ref, l_ref, g_ref, m_ref, *, ne):
    epg = ne // N_GROUPS
    logits = l_ref[...]
    s = [_sigmoid(logits[e:e + 1, :]) for e in range(ne)]
    sel = [s[e] + b_ref[e] for e in range(ne)]
    gscore = []
    for g in range(N_GROUPS):
        v = sel[g * epg:(g + 1) * epg]
        best = None
        for i in range(epg):
            for j in range(i + 1, epg):
                pair = v[i] + v[j]
                best = pair if best is None else jnp.maximum(best, pair)
        gscore.append(best)
    chosen = []
    for g in range(N_GROUPS):
        ok = None
        for o in range(N_GROUPS):
            if o == g:
                continue
            t = (gscore[g] > gscore[o]) if o < g else (gscore[g] >= gscore[o])
            ok = t if ok is None else jnp.logical_and(ok, t)
        chosen.append(ok)
    mask = []
    for g in range(N_GROUPS):
        v = sel[g * epg:(g + 1) * epg]
        for i in range(epg):
            rank = jnp.zeros_like(v[i])
            for j in range(epg):
                if j == i:
                    continue
                ahead = (v[j] >= v[i]) if j < i else (v[j] > v[i])
                rank = rank + jnp.where(ahead, 1.0, 0.0)
            mask.append(jnp.logical_and(chosen[g], rank < TOP_K))
    wsum = jnp.zeros_like(s[0])
    for e in range(ne):
        wsum = wsum + jnp.where(mask[e], s[e], 0.0)
    for e in range(ne):
        g_ref[e:e + 1, :] = jnp.where(mask[e], s[e] / wsum, 0.0)
        m_ref[e:e + 1, :] = jnp.where(mask[e], 1, 0).astype(jnp.int32)


def router(logits_t, router_b):
    ne, t = logits_t.shape
    tl = _pick(t, (1024, 512, 256, 128))
    spec = pl.BlockSpec((ne, tl), lambda i: (0, i))
    return pl.pallas_call(
        functools.partial(_router_kernel, ne=ne),
        grid=(t // tl,),
        in_specs=[pl.BlockSpec(memory_space=pltpu.SMEM), spec],
        out_specs=[spec, spec],
        out_shape=[jax.ShapeDtypeStruct((ne, t), F32), jax.ShapeDtypeStruct((ne, t), jnp.int32)],
        compiler_params=_cparams("parallel"),
        name="router",
    )(router_b, logits_t)


def dispatch_plan(gates, mask, tm):
    ne, t = mask.shape
    n_tiles = (TOP_K * t) // tm + ne
    rows = n_tiles * tm
    sel = mask > 0
    counts = jnp.sum(mask, axis=1)
    ptiles = (counts + tm - 1) // tm
    tile_end = jnp.cumsum(ptiles)
    off = (tile_end - ptiles) * tm
    dest = off[:, None] + jnp.cumsum(mask, axis=1) - mask
    pos_lo = jnp.min(jnp.where(sel, dest, rows), axis=0)
    pos_hi = jnp.max(jnp.where(sel, dest, -1), axis=0)
    g_lo = jnp.sum(jnp.where(sel & (dest == pos_lo[None, :]), gates, 0.0), axis=0)
    g_hi = jnp.sum(jnp.where(sel & (dest == pos_hi[None, :]), gates, 0.0), axis=0)
    pos = jnp.concatenate([pos_lo, pos_hi]).astype(jnp.int32)
    n_valid = tile_end[-1]
    tiles = jnp.arange(n_tiles, dtype=jnp.int32)
    tid = jnp.minimum(tiles, n_valid - 1)
    tile_expert = jnp.minimum(jnp.sum(tile_end[None, :] <= tid[:, None], axis=1), ne - 1).astype(jnp.int32)
    prev = jnp.concatenate([jnp.full((1,), -1, jnp.int32), tile_expert[:-1]])
    tile_first = (tile_expert != prev).astype(jnp.int32)
    last = jnp.any((tiles[:, None] == tile_end[None, :] - 1) & (ptiles[None, :] > 0), axis=1)
    tile_fill = (last | (tiles >= n_valid)).astype(jnp.int32)
    nv = jnp.reshape(n_valid, (1,)).astype(jnp.int32)
    return tile_fill, jnp.stack([g_lo, g_hi], axis=1), pos, tid, tile_expert, tile_first, nv


def _scatter_kernel(pos_ref, fill_ref, x_ref, o_ref, stage_ref, zero_ref, sems, zsem, *, tile, tm):
    i = pl.program_id(0)
    n = pl.num_programs(0)
    ntok = pos_ref.shape[0] // 2
    slot = i % 2

    def slot_wait(s):
        for _ in range(2):
            pltpu.make_async_copy(stage_ref.at[s], o_ref.at[pl.ds(0, tile)], sems.at[s]).wait()

    @pl.when(i == 0)
    def _():
        zero_ref[...] = jnp.zeros_like(zero_ref)

        def fill(t, wait):
            @pl.when(fill_ref[t] == 1)
            def _():
                cp = pltpu.make_async_copy(zero_ref, o_ref.at[pl.ds(t * tm, tm)], zsem)
                cp.wait() if wait else cp.start()

        lax.fori_loop(0, fill_ref.shape[0], lambda t, c: (fill(t, False), c)[1], 0)
        lax.fori_loop(0, fill_ref.shape[0], lambda t, c: (fill(t, True), c)[1], 0)

    @pl.when(i >= 2)
    def _():
        slot_wait(slot)

    stage_ref[slot] = x_ref[...]

    def start(r, carry):
        for k in range(2):
            dst = pos_ref[k * ntok + i * tile + r]
            pltpu.make_async_copy(stage_ref.at[slot, r], o_ref.at[dst], sems.at[slot]).start(priority=k)
        return carry

    lax.fori_loop(0, tile, start, 0, unroll=4)

    @pl.when(i == n - 1)
    def _():
        slot_wait(slot)

        @pl.when(n >= 2)
        def _():
            slot_wait(1 - slot)


def scatter_rows(h, pos, tile_fill, tm, tile):
    t = h.shape[0]
    rows = tile_fill.shape[0] * tm
    return pl.pallas_call(
        functools.partial(_scatter_kernel, tile=tile, tm=tm),
        grid_spec=pltpu.PrefetchScalarGridSpec(
            num_scalar_prefetch=2,
            grid=(t // tile,),
            in_specs=[pl.BlockSpec((tile,) + h.shape[1:], lambda i, p, f: (i, 0, 0))],
            out_specs=pl.BlockSpec(memory_space=pl.ANY),
            scratch_shapes=[pltpu.VMEM((2, tile) + h.shape[1:], h.dtype), pltpu.VMEM((tm,) + h.shape[1:], h.dtype),
                            pltpu.SemaphoreType.DMA((2,)), pltpu.SemaphoreType.DMA(())],
        ),
        out_shape=jax.ShapeDtypeStruct((rows,) + h.shape[1:], h.dtype),
        compiler_params=_cparams("arbitrary"),
        name="moe_scatter",
    )(pos, tile_fill, h)


def _ffn_a_kernel(tid_ref, te_ref, tf_ref, nv_ref, x_ref, w1_ref, w3_ref, o_ref, w1b, w3b):
    i = pl.program_id(1)

    @pl.when(i < nv_ref[0])
    def _():
        @pl.when(tf_ref[i] == 1)
        def _():
            w1b[...] = w1_ref[0].astype(BF16)
            w3b[...] = w3_ref[0].astype(BF16)

        x = x_ref[...].reshape(x_ref.shape[0], -1)
        a = jnp.dot(x, w1b[...], preferred_element_type=F32)
        b = jnp.dot(x, w3b[...], preferred_element_type=F32)
        o_ref[...] = (a * _sigmoid(a) * b).astype(o_ref.dtype)

    @pl.when(i >= nv_ref[0])
    def _():
        o_ref[...] = jnp.zeros_like(o_ref)


def ffn_stage_a(xs, w1, w3, layer, plan, tm):
    _, _, _, tid, te, tf, nv = plan
    rows = xs.shape[0]
    d = xs.shape[1] * xs.shape[2]
    ff = w1.shape[3]
    tfc = _pick(ff, (512, 256, 128))
    wspec = pl.BlockSpec((None, 1, d, tfc), lambda f, i, tid, te, tf, nv: (layer, te[i], 0, f))
    return pl.pallas_call(
        _ffn_a_kernel,
        grid_spec=pltpu.PrefetchScalarGridSpec(
            num_scalar_prefetch=4,
            grid=(ff // tfc, rows // tm),
            in_specs=[pl.BlockSpec((tm,) + xs.shape[1:], lambda f, i, tid, te, tf, nv: (tid[i], 0, 0)), wspec, wspec],
            out_specs=pl.BlockSpec((tm, tfc), lambda f, i, tid, te, tf, nv: (i, f)),
            scratch_shapes=[pltpu.VMEM((d, tfc), BF16), pltpu.VMEM((d, tfc), BF16)],
        ),
        out_shape=jax.ShapeDtypeStruct((rows, ff), BF16),
        compiler_params=_cparams("arbitrary", "arbitrary"),
        name="moe_ffn_a",
    )(tid, te, tf, nv, xs, w1, w3)


def _ffn_b_kernel(tid_ref, te_ref, tf_ref, nv_ref, h_ref, w2_ref, o_ref, w2b):
    i = pl.program_id(1)

    @pl.when(i < nv_ref[0])
    def _():
        @pl.when(tf_ref[i] == 1)
        def _():
            w2b[...] = w2_ref[0].astype(BF16)

        y = jnp.dot(h_ref[...], w2b[...], preferred_element_type=F32)
        o_ref[...] = y.reshape(o_ref.shape).astype(o_ref.dtype)

    @pl.when(i >= nv_ref[0])
    def _():
        o_ref[...] = jnp.zeros_like(o_ref)


def ffn_stage_b(hs, w2, layer, plan, tm):
    _, _, _, tid, te, tf, nv = plan
    rows, ff = hs.shape
    d = w2.shape[3]
    tn = _pick(d, (2048, 1024, 512, 256, 128))
    return pl.pallas_call(
        _ffn_b_kernel,
        grid_spec=pltpu.PrefetchScalarGridSpec(
            num_scalar_prefetch=4,
            grid=(d // tn, rows // tm),
            in_specs=[
                pl.BlockSpec((tm, ff), lambda n, i, tid, te, tf, nv: (tid[i], 0)),
                pl.BlockSpec((None, 1, ff, tn), lambda n, i, tid, te, tf, nv: (layer, te[i], 0, n)),
            ],
            out_specs=pl.BlockSpec((tm, tn // LANES, LANES), lambda n, i, tid, te, tf, nv: (i, n, 0)),
            scratch_shapes=[pltpu.VMEM((ff, tn), BF16)],
        ),
        out_shape=jax.ShapeDtypeStruct((rows, d // LANES, LANES), BF16),
        compiler_params=_cparams("arbitrary", "arbitrary"),
        name="moe_ffn_b",
    )(tid, te, tf, nv, hs, w2)


def moe_sparse(h, logits_t, router_b, w1, w3, w2, layer, tile):
    gates, mask = router(logits_t, router_b)
    plan = dispatch_plan(gates, mask, MOE_TILE)
    xs = scatter_rows(h, plan[2], plan[0], MOE_TILE, tile)
    hs = ffn_stage_a(xs, w1, w3, layer, plan, MOE_TILE)
    ys = ffn_stage_b(hs, w2, layer, plan, MOE_TILE)
    return ys, plan[2], plan[1]


def rope_tables(n, lc, hd):
    half = hd // 2
    t = jnp.arange(n)
    inv = jnp.power(ROPE_BASE, -jnp.arange(0, half, 2, dtype=F32) / half)
    ang_r = (t // GRID_W).astype(F32)[:, None] * inv
    ang_c = (t % GRID_W).astype(F32)[:, None] * inv
    cos = jnp.concatenate([jnp.cos(ang_r)] * 2 + [jnp.cos(ang_c)] * 2, axis=1)
    sin = jnp.concatenate([-jnp.sin(ang_r), jnp.sin(ang_r), -jnp.sin(ang_c), jnp.sin(ang_c)], axis=1)
    cos = jnp.concatenate([jnp.ones((lc, hd), F32), cos], axis=0)
    sin = jnp.concatenate([jnp.zeros((lc, hd), F32), sin], axis=0)
    scale = hd ** -0.5
    ones, zeros = jnp.ones_like(cos), jnp.zeros_like(sin)
    return jnp.stack([cos * scale, cos, ones]), jnp.stack([sin * scale, sin, zeros])


def kernel(x, c, ctx, c_ctx, ada_w, ada_b, ln_g, ln_b, mlstm_w_in, mlstm_b_gates, mlstm_norm_w, mlstm_w_out,
           attn_w_in, attn_b_in, attn_sink, attn_w_out, router_w, router_b, moe_w1, moe_w3, moe_w2):
    bsz, n, d = x.shape
    lc = ctx.shape[1]
    depth = ada_w.shape[0]
    assert depth == 2 and bsz <= 4 and n % lc == 0 and lc % ML_CHUNK == 0
    alpha = (2 * depth) ** 0.25
    tile = lc
    per = (n + lc) // tile
    per_lat = n // tile

    cc = jnp.concatenate([c, c_ctx[None, :], jnp.zeros((8 - bsz - 1, d), F32)], axis=0)
    ada0 = adaln(cc, ada_w, ada_b, 0).reshape(8, 1, 6 * d)
    rwt = router_w.T

    heads = mlstm_b_gates.shape[1] // 4
    dv = d // heads
    dqk = dv // 2
    nmain = 2 * heads * (dqk + dv)
    w_in_t = jnp.swapaxes(mlstm_w_in, 1, 2)
    h0, gates_pre = mod0(x, ctx, ada0, w_in_t[0, nmain:, :], mlstm_b_gates[0][None, :])
    qkvo = matmul(h0, w_in_t, n_cols=nmain, w_t=True)
    ncc, nct = lc // ML_CHUNK, (n + lc) // ML_CHUNK
    hg, ada1 = mlstm_bidir(qkvo, gates_pre, mlstm_norm_w[0], bsz, heads, dqk, dv, ncc, nct, cc, ada_w, ada_b, 1)
    ada1 = ada1.reshape(8, 1, 6 * d)
    y = matmul(hg, mlstm_w_out, out_dtype=F32)
    x1, hffn, lg = ln_block(x, y, ada0, 2, ln_g[0, 0], ln_b[0, 0], alpha=alpha, nb=bsz, per=per, tile=tile,
                            ctx_rows=True, ctx=ctx, nxt=(ada0, 3, 4), h_dtype=BF16, h_rows3d=True, router_wt=rwt)
    ys, pos, pos_w = moe_sparse(hffn, lg, router_b, moe_w1, moe_w3, moe_w2, 0, tile)
    x2, h1 = ln_block(x1, ys, ada0, 5, ln_g[0, 1], ln_b[0, 1], alpha=alpha, nb=bsz, per=per, tile=tile,
                      ctx_rows=True, x_per=per, nxt=(ada1, 0, 1), h_dtype=BF16, pos=pos, pos_w=pos_w)

    heads_a = attn_sink.shape[1]
    hd = d // heads_a
    kvh = (attn_w_in.shape[2] // hd - heads_a) // 2
    cos_t, sin_t = rope_tables(n, lc, hd)
    q_end, k_end = heads_a * hd, (heads_a + kvh) * hd
    type_of = lambda col: jnp.where(col < q_end, 0, jnp.where(col < k_end, 1, 2))
    tm1 = _pick(n + lc, (1152, 768, 384, 256, 128))
    qkv = matmul(h1, attn_w_in, bias=attn_b_in[0], rope=(cos_t, sin_t, n + lc, type_of), tm=tm1,
                 tn=_pick(kvh * hd, (512, 256, 128)))
    att = window_attention(qkv.reshape(bsz, n + lc, -1), attn_sink[0], lc, n, heads_a, kvh, hd)
    y = matmul(att.reshape(bsz * n, heads_a * hd), attn_w_out, out_dtype=F32)
    x3, hffn, lg = ln_block(x2, y, ada1, 2, ln_g[1, 0], ln_b[1, 0], alpha=alpha, nb=bsz, per=per_lat, tile=tile,
                            ctx_rows=False, x_per=per, x_off=1, nxt=(ada1, 3, 4), h_dtype=BF16, h_rows3d=True,
                            router_wt=rwt)
    ys, pos, pos_w = moe_sparse(hffn, lg, router_b, moe_w1, moe_w3, moe_w2, 1, tile)
    (x4,) = ln_block(x3, ys, ada1, 5, ln_g[1, 1], ln_b[1, 1], alpha=alpha, nb=bsz, per=per_lat, tile=tile,
                     ctx_rows=False, x_per=per_lat, pos=pos, pos_w=pos_w)
    return x4.reshape(bsz, n, d)
```

```python
import functools

import jax
import jax.numpy as jnp
from jax import lax
from jax.experimental import pallas as pl
from jax.experimental.pallas import tpu as pltpu

F32 = jnp.float32
BF16 = jnp.bfloat16
NEG_INF = float("-inf")

GRID_W = 64
ML_CHUNK = 128
AT_WINDOW = 128
AT_BLOCK = 128
ROPE_BASE = 10000.0
N_GROUPS = 4
TOP_K = 2
LN_EPS = 1e-5
RMS_EPS = 1e-6

LANES = 128
VMEM_LIMIT = 56 * 1024 * 1024
MOE_TILE = 512
HIGHEST = lax.Precision.HIGHEST
NT_DIMS = (((1,), (1,)), ((), ()))


def _cparams(*sem):
    return pltpu.CompilerParams(dimension_semantics=sem, vmem_limit_bytes=VMEM_LIMIT)


def _pick(n, candidates):
    for c in candidates:
        if n % c == 0:
            return c
    return n


def _sigmoid(x):
    return 1.0 / (1.0 + jnp.exp(-x))


def _log_sigmoid(x):
    return jnp.minimum(x, 0.0) - jnp.log1p(jnp.exp(-jnp.abs(x)))


def _adaln_block(c_ref, w_ref, b_ref):
    c = c_ref[...]
    a = (c * _sigmoid(c)).astype(BF16)
    return jnp.dot(a, w_ref[...].astype(BF16), preferred_element_type=F32) + b_ref[...]


def _adaln_kernel(c_ref, w_ref, b_ref, o_ref):
    o_ref[...] = _adaln_block(c_ref, w_ref, b_ref)


def adaln(cc, ada_w, ada_b, layer):
    depth, d, n6 = ada_w.shape
    tn = _pick(n6, (512, 256, 128))
    return pl.pallas_call(
        _adaln_kernel,
        grid=(n6 // tn,),
        in_specs=[
            pl.BlockSpec((8, d), lambda j: (0, 0)),
            pl.BlockSpec((None, d, tn), lambda j: (layer, 0, j)),
            pl.BlockSpec((None, 1, tn), lambda j: (layer, 0, j)),
        ],
        out_specs=pl.BlockSpec((8, tn), lambda j: (0, j)),
        out_shape=jax.ShapeDtypeStruct((8, n6), F32),
        compiler_params=_cparams("parallel"),
        name="adaln",
    )(cc, ada_w, ada_b.reshape(depth, 1, n6))


def _mod0_kernel(x_ref, c_ref, sh_ref, sc_ref, wg_ref, bg_ref, h_ref, g_ref):
    j = pl.program_id(1)
    t = jnp.where(j == 0, c_ref[0], x_ref[0])
    h = t * (1.0 + sc_ref[0]) + sh_ref[0]
    h_ref[...] = h.astype(BF16)
    g_ref[...] = lax.dot_general(h, wg_ref[...], NT_DIMS, preferred_element_type=F32,
                                 precision=HIGHEST) + bg_ref[...]


def mod0(x, ctx, ada3, wg, bg):
    bsz, n, d = x.shape
    lc = ctx.shape[1]
    per = (n + lc) // lc
    ng = wg.shape[0]
    prow = lambda b, j: jnp.where(j == 0, bsz, b)
    return pl.pallas_call(
        _mod0_kernel,
        grid=(bsz, per),
        in_specs=[
            pl.BlockSpec((1, lc, d), lambda b, j: (b, jnp.maximum(j - 1, 0), 0)),
            pl.BlockSpec((1, lc, d), lambda b, j: (b, 0, 0)),
            pl.BlockSpec((1, 1, d), lambda b, j: (prow(b, j), 0, 0)),
            pl.BlockSpec((1, 1, d), lambda b, j: (prow(b, j), 0, 1)),
            pl.BlockSpec((ng, d), lambda b, j: (0, 0)),
            pl.BlockSpec((1, ng), lambda b, j: (0, 0)),
        ],
        out_specs=[
            pl.BlockSpec((lc, d), lambda b, j: (b * per + j, 0)),
            pl.BlockSpec((lc, ng), lambda b, j: (b * per + j, 0)),
        ],
        out_shape=[
            jax.ShapeDtypeStruct((bsz * (n + lc), d), BF16),
            jax.ShapeDtypeStruct((bsz * (n + lc), ng), F32),
        ],
        compiler_params=_cparams("parallel", "arbitrary"),
        name="mod0",
    )(x, ctx, ada3, ada3, wg, bg)


def _mm_kernel(*refs, has_bias, has_rope, tn, w_t):
    x_ref, w_ref = refs[0], refs[1]
    pos = 2
    if w_t:
        acc = lax.dot_general(x_ref[...], w_ref[...].astype(BF16), NT_DIMS, preferred_element_type=F32)
    else:
        acc = jnp.dot(x_ref[...], w_ref[...].astype(BF16), preferred_element_type=F32)
    if has_bias:
        acc = acc + refs[pos][...]
        pos += 1
    o_ref = refs[-1]
    if has_rope:
        ca = refs[pos][0]
        sb = refs[pos + 1][0]
        lane = lax.broadcasted_iota(jnp.int32, ca.shape, 1)
        first = (lane % 64) < 32
        for h in range(tn // LANES):
            y = acc[:, h * LANES:(h + 1) * LANES]
            sw = jnp.where(first, pltpu.roll(y, 96, 1), pltpu.roll(y, 32, 1))
            o_ref[:, h * LANES:(h + 1) * LANES] = (y * ca + sw * sb).astype(o_ref.dtype)
    else:
        o_ref[...] = acc.astype(o_ref.dtype)


def matmul(x, w, bias=None, rope=None, out_dtype=BF16, tm=None, tn=None, n_cols=None, w_t=False):
    m, k = x.shape
    n = n_cols or w.shape[1 if w_t else 2]
    tm = tm or _pick(m, (1152, 1024, 768, 512, 256, 128))
    tn = tn or _pick(n, (512, 256, 128))
    in_specs = [
        pl.BlockSpec((tm, k), lambda i, j: (i, 0)),
        pl.BlockSpec((None, tn, k), lambda i, j: (0, j, 0)) if w_t else
        pl.BlockSpec((None, k, tn), lambda i, j: (0, 0, j)),
    ]
    args = [x, w]
    if bias is not None:
        in_specs.append(pl.BlockSpec((1, tn), lambda i, j: (0, j)))
        args.append(bias.reshape(1, n))
    if rope is not None:
        ca, sb, period, type_of = rope
        nper = period // tm
        spec = pl.BlockSpec((1, tm, LANES), lambda i, j: (type_of(j * tn), i % nper, 0))
        in_specs += [spec, spec]
        args += [ca, sb]
    return pl.pallas_call(
        functools.partial(_mm_kernel, has_bias=bias is not None, has_rope=rope is not None, tn=tn, w_t=w_t),
        grid=(m // tm, n // tn),
        in_specs=in_specs,
        out_specs=pl.BlockSpec((tm, tn), lambda i, j: (i, j)),
        out_shape=jax.ShapeDtypeStruct((m, n), out_dtype),
        compiler_params=_cparams("parallel", "arbitrary"),
        name="matmul",
    )(*args)


def _mlstm_chunks(chains, scale):
    L = ML_CHUNK
    r = lax.broadcasted_iota(jnp.int32, (L, L), 0)
    c = lax.broadcasted_iota(jnp.int32, (L, L), 1)
    gate = []
    for q, k, v, ig_row, f_row, ig_col, f_col, st, n_row, m_prev, d in chains:
        reverse = d == 1
        lf_row = _log_sigmoid(f_row)
        lf_col = _log_sigmoid(f_col)
        keep = (c >= r) if reverse else (c <= r)
        keep_t = (r >= c) if reverse else (r <= c)
        b_col = jnp.sum(jnp.where(keep, lf_row, 0.0), axis=1, keepdims=True)
        b_row = jnp.sum(jnp.where(keep_t, lf_col, 0.0), axis=0, keepdims=True)
        logw = jnp.where(keep, b_col - b_row + ig_row, NEG_INF)
        inter = b_col + m_prev
        m_t = jnp.maximum(inter, jnp.max(logw, axis=1, keepdims=True))
        w_inter = jnp.exp(inter - m_t)
        p = jnp.exp(logw - m_t)
        b_end = b_col[0:1, :] if reverse else b_col[L - 1:L, :]
        m_new = jnp.maximum(b_end + m_prev, jnp.max(b_end - b_row + ig_row, axis=1, keepdims=True))
        decay = jnp.exp(b_end + m_prev - m_new)
        w_src = jnp.exp(b_end - b_col + ig_col - m_new)
        gate.append((m_t, w_inter, p, m_new, decay, w_src))

    qk = [lax.dot_general(ch[0], ch[1], NT_DIMS, preferred_element_type=F32) for ch in chains]
    qs = [jnp.dot(ch[0], ch[7].astype(BF16), preferred_element_type=F32) for ch in chains]
    s = [a * (g[2] * scale) for a, g in zip(qk, gate)]
    sv = [jnp.dot(a.astype(BF16), ch[2], preferred_element_type=F32) for a, ch in zip(s, chains)]
    kw = [ch[1].astype(F32) * g[5] for ch, g in zip(chains, gate)]
    kv = [jnp.dot(a.T.astype(BF16), ch[2], preferred_element_type=F32) for a, ch in zip(kw, chains)]

    out = []
    for ch, g, s_i, qs_i, sv_i, kw_i, kv_i in zip(chains, gate, s, qs, sv, kw, kv):
        q, n_row, st = ch[0], ch[8], ch[7]
        m_t, w_inter, _, m_new, decay, _ = g
        num = (w_inter * scale) * qs_i + sv_i
        qn = jnp.sum(q.astype(F32) * n_row, axis=1, keepdims=True) * scale
        den = w_inter * qn + jnp.sum(s_i, axis=1, keepdims=True)
        h = num * (1.0 / jnp.maximum(jnp.abs(den), jnp.exp(-m_t)))
        out.append((h, decay * st + kv_i, decay * n_row + jnp.sum(kw_i, axis=0, keepdims=True), m_new))
    return out


def _mlstm_kernel(q_ref, k_ref, v_ref, o_ref, gr_ref, gc_ref, nw_ref, cc_ref, aw_ref, ab_ref, out_ref, ada_ref,
                  acc_ref, st_ref, n_ref, m_ref, *, scale, ncc, nct, hpb):
    L = ML_CHUNK
    ada_ref[...] = _adaln_block(cc_ref, aw_ref, ab_ref)
    st_ref[...] = jnp.zeros_like(st_ref)
    n_ref[...] = jnp.zeros_like(n_ref)
    m_ref[...] = jnp.zeros_like(m_ref)
    acc_ref[...] = jnp.zeros_like(acc_ref)

    dqk, dv = st_ref.shape[1], st_ref.shape[2]

    def step(j, carry):
        where, chains = [], []
        for hh in range(hpb):
            for d, cj in ((0, j), (1, jnp.where(j < ncc, ncc - 1 - j, nct - 1 - (j - ncc)))):
                rows = pl.ds(pl.multiple_of(cj * L, L), L)
                slot = 2 * hh + d
                gcol = gc_ref[0, hh, cj]
                where.append((rows, hh, slot, acc_ref[rows, hh * dv:(hh + 1) * dv]))
                chains.append((q_ref[rows, hh * dqk:(hh + 1) * dqk], k_ref[rows, hh * dqk:(hh + 1) * dqk],
                               v_ref[rows, hh * dv:(hh + 1) * dv],
                               gr_ref[0, 2 * d, hh, cj], gr_ref[0, 2 * d + 1, hh, cj],
                               gcol[:, 2 * d:2 * d + 1], gcol[:, 2 * d + 1:2 * d + 2],
                               st_ref[slot], n_ref[slot], m_ref[slot], d))
        for (rows, hh, slot, acc), (h, st_new, n_new, m_new) in zip(where, _mlstm_chunks(chains, scale)):
            acc_ref[rows, hh * dv:(hh + 1) * dv] = acc + h
            st_ref[slot] = st_new
            n_ref[slot] = n_new
            m_ref[slot] = m_new
        return carry

    lax.fori_loop(0, nct, step, 0, unroll=2)

    def finish(cj, carry):
        rows = pl.ds(pl.multiple_of(cj * L, L), L)
        for hh in range(hpb):
            cols = slice(hh * dv, (hh + 1) * dv)
            hs = acc_ref[rows, cols]
            hn = hs * lax.rsqrt(jnp.mean(hs * hs, axis=1, keepdims=True) + RMS_EPS)
            out_ref[rows, cols] = (hn * nw_ref[:, cols]
                                   * _sigmoid(o_ref[rows, cols].astype(F32))).astype(out_ref.dtype)
        return carry

    lax.fori_loop(0, nct, finish, 0)


def mlstm_bidir(qkvo, gates_pre, norm_w, bsz, heads, dqk, dv, ncc, nct, cc, ada_w, ada_b, ada_layer):
    t = qkvo.shape[0]
    L = ML_CHUNK
    n2 = nct * L
    hpb = 1
    depth, d, n6 = ada_w.shape
    nsteps = bsz * heads // hpb
    ta = n6 // nsteps
    assert n6 % nsteps == 0 and ta % LANES == 0
    assert ncc % 2 == 0 and (nct - ncc) % 2 == 0
    g4 = gates_pre.reshape(bsz, nct, L, 4, heads)
    g_rows = jnp.transpose(g4, (0, 3, 4, 1, 2)).reshape(bsz, 4, heads, nct, 1, L)
    g_cols = jnp.transpose(g4, (0, 4, 1, 2, 3))
    koff = heads // hpb
    voff = 2 * heads * dqk // (dv * hpb)
    ooff = voff + heads // hpb
    return pl.pallas_call(
        functools.partial(_mlstm_kernel, scale=dqk ** -0.5, ncc=ncc, nct=nct, hpb=hpb),
        grid=(bsz, heads // hpb),
        in_specs=[
            pl.BlockSpec((n2, hpb * dqk), lambda b, h: (b, h)),
            pl.BlockSpec((n2, hpb * dqk), lambda b, h: (b, koff + h)),
            pl.BlockSpec((n2, hpb * dv), lambda b, h: (b, voff + h)),
            pl.BlockSpec((n2, hpb * dv), lambda b, h: (b, ooff + h)),
            pl.BlockSpec((1, 4, hpb, nct, 1, L), lambda b, h: (b, 0, h, 0, 0, 0)),
            pl.BlockSpec((1, hpb, nct, L, 4), lambda b, h: (b, h, 0, 0, 0)),
            pl.BlockSpec((1, hpb * dv), lambda b, h: (0, h)),
            pl.BlockSpec((8, d), lambda b, h: (0, 0)),
            pl.BlockSpec((None, d, ta), lambda b, h: (ada_layer, 0, b * (heads // hpb) + h)),
            pl.BlockSpec((None, 1, ta), lambda b, h: (ada_layer, 0, b * (heads // hpb) + h)),
        ],
        out_specs=[pl.BlockSpec((n2, hpb * dv), lambda b, h: (b, h)),
                   pl.BlockSpec((8, ta), lambda b, h: (0, b * (heads // hpb) + h))],
        out_shape=[jax.ShapeDtypeStruct((t, heads * dv), BF16), jax.ShapeDtypeStruct((8, n6), F32)],
        scratch_shapes=[pltpu.VMEM((n2, hpb * dv), F32), pltpu.VMEM((2 * hpb, dqk, dv), F32),
                        pltpu.VMEM((2 * hpb, 1, dqk), F32), pltpu.VMEM((2 * hpb, 1, 1), F32)],
        compiler_params=_cparams("parallel", "parallel"),
        name="mlstm",
    )(qkvo, qkvo, qkvo, qkvo, g_rows, g_cols, norm_w.reshape(1, heads * dv), cc, ada_w,
      ada_b.reshape(depth, 1, n6))


def _attn_kernel(sink_ref, q_ref, k_ref, v_ref, o_ref, bias_ref, *, lc, nb, group, hd):
    kv = pl.program_id(1)
    blk = AT_BLOCK
    rows = group * blk
    kc = k_ref[0, 0:lc, :]
    vc = v_ref[0, 0:lc, :]
    rid = lax.broadcasted_iota(jnp.int32, (rows, 1), 0)
    sink_col = jnp.zeros((rows, 1), F32)
    for g in range(group):
        sink_col = jnp.where(rid // blk == g, sink_ref[kv * group + g], sink_col)
    qrow = lax.broadcasted_iota(jnp.int32, (rows, 3 * blk), 0) % blk
    kcol = lax.broadcasted_iota(jnp.int32, (rows, 3 * blk), 1)
    for back in range(3):
        rel = kcol - back * blk - qrow
        bias_ref[back] = jnp.where(jnp.abs(rel) <= AT_WINDOW, 0.0, NEG_INF)

    def body(j, carry):
        ws = jnp.clip(j - 1, 0, nb - 3)
        qs = pl.multiple_of(lc + j * blk, blk)
        ks = pl.multiple_of(lc + ws * blk, blk)
        q4 = q_ref[0, pl.ds(qs, blk), :]
        qq = jnp.concatenate([q4[:, g * hd:(g + 1) * hd] for g in range(group)], axis=0)
        kw = k_ref[0, pl.ds(ks, 3 * blk), :]
        vw = v_ref[0, pl.ds(ks, 3 * blk), :]
        s_loc = lax.dot_general(qq, kw, NT_DIMS, preferred_element_type=F32)
        s_ctx = lax.dot_general(qq, kc, NT_DIMS, preferred_element_type=F32)
        s_loc = s_loc + bias_ref[j - ws]
        m = jnp.maximum(jnp.maximum(jnp.max(s_loc, axis=1, keepdims=True),
                                    jnp.max(s_ctx, axis=1, keepdims=True)), sink_col)
        p_loc = jnp.exp(s_loc - m)
        p_ctx = jnp.exp(s_ctx - m)
        den = (jnp.sum(p_loc, axis=1, keepdims=True) + jnp.sum(p_ctx, axis=1, keepdims=True)
               + jnp.exp(sink_col - m))
        o = (jnp.dot(p_loc.astype(BF16), vw, preferred_element_type=F32)
             + jnp.dot(p_ctx.astype(BF16), vc, preferred_element_type=F32)) / den
        os = pl.multiple_of(j * blk, blk)
        for g in range(group):
            o_ref[0, pl.ds(os, blk), g * hd:(g + 1) * hd] = o[g * blk:(g + 1) * blk].astype(o_ref.dtype)
        return carry

    lax.fori_loop(0, nb, body, 0, unroll=4 if nb % 4 == 0 else 2)


def window_attention(qkv3, sink, lc, n, heads, kv_heads, hd):
    bsz = qkv3.shape[0]
    group = heads // kv_heads
    nb = n // AT_BLOCK
    assert nb >= 3
    koff = heads
    voff = heads + kv_heads
    return pl.pallas_call(
        functools.partial(_attn_kernel, lc=lc, nb=nb, group=group, hd=hd),
        grid=(bsz, kv_heads),
        in_specs=[
            pl.BlockSpec(memory_space=pltpu.SMEM),
            pl.BlockSpec((1, lc + n, group * hd), lambda b, g: (b, 0, g)),
            pl.BlockSpec((1, lc + n, hd), lambda b, g: (b, 0, koff + g)),
            pl.BlockSpec((1, lc + n, hd), lambda b, g: (b, 0, voff + g)),
        ],
        out_specs=pl.BlockSpec((1, n, group * hd), lambda b, g: (b, 0, g)),
        out_shape=jax.ShapeDtypeStruct((bsz, n, heads * hd), BF16),
        scratch_shapes=[pltpu.VMEM((3, group * AT_BLOCK, 3 * AT_BLOCK), F32)],
        compiler_params=_cparams("parallel", "parallel"),
        name="window_attention",
    )(sink, qkv3, qkv3, qkv3)


def _ln_kernel(*refs, dual, gather, tile, has_h, has_router, alpha):
    refs = list(refs)
    if gather:
        pos_ref = refs.pop(0)
    x_ref = refs.pop(0)
    c_ref = refs.pop(0) if dual else None
    y_ref = refs.pop(0)
    gw_ref = refs.pop(0) if gather else None
    gate_ref, lng_ref, lnb_ref = refs.pop(0), refs.pop(0), refs.pop(0)
    if has_h:
        sh_ref, sc_ref = refs.pop(0), refs.pop(0)
    if has_router:
        rw_ref = refs.pop(0)
    xo_ref = refs.pop(0)
    if has_h:
        ho_ref = refs.pop(0)
    if has_router:
        lg_ref = refs.pop(0)

    if gather:
        buf_ref, sems = refs.pop(0), refs.pop(0)
        lin = pl.program_id(0) * pl.num_programs(1) + pl.program_id(1)
        ntiles = pl.num_programs(0) * pl.num_programs(1)
        ntok = pos_ref.shape[0] // 2

        def issue(t):
            slot = t % 2

            def start(r, carry):
                for k in range(2):
                    src = pos_ref[k * ntok + t * tile + r]
                    pltpu.make_async_copy(y_ref.at[src], buf_ref.at[slot, k, r], sems.at[slot]).start(priority=k)
                return carry

            lax.fori_loop(0, tile, start, 0, unroll=4)

        @pl.when(lin == 0)
        def _():
            issue(lin)

        @pl.when(lin + 1 < ntiles)
        def _():
            issue(lin + 1)

        slot = lin % 2
        for k in range(2):
            pltpu.make_async_copy(y_ref.at[pl.ds(0, tile)], buf_ref.at[slot, k], sems.at[slot]).wait()
        gw = gw_ref[...]
        d = x_ref.shape[-1]
        y = (gw[:, 0:1] * buf_ref[slot, 0].reshape(tile, d).astype(F32)
             + gw[:, 1:2] * buf_ref[slot, 1].reshape(tile, d).astype(F32))
    else:
        y = y_ref[...].astype(F32)

    if dual:
        x = jnp.where(pl.program_id(1) == 0, c_ref[0], x_ref[0])
    else:
        x = x_ref[...]
    z = alpha * x + gate_ref[0] * y
    mu = jnp.mean(z, axis=1, keepdims=True)
    zc = z - mu
    var = jnp.mean(zc * zc, axis=1, keepdims=True)
    xn = zc * lax.rsqrt(var + LN_EPS) * lng_ref[...] + lnb_ref[...]
    xo_ref[...] = xn
    if has_h:
        h = xn * (1.0 + sc_ref[0]) + sh_ref[0]
        ho_ref[...] = h.reshape(ho_ref.shape).astype(ho_ref.dtype)
        if has_router:
            lg_ref[...] = lax.dot_general(rw_ref[...], h, NT_DIMS, preferred_element_type=F32,
                                          precision=HIGHEST)


def ln_block(x, y, ada3, gate_col, lng, lnb, *, alpha, nb, per, tile, ctx_rows, ctx=None, x_per=None,
             x_off=0, nxt=None, h_dtype=None, h_rows3d=False, router_wt=None, pos=None, pos_w=None):
    d = x.shape[-1]
    dual = ctx is not None
    gather = pos is not None
    rows = nb * per * tile
    prow = (lambda b, j: jnp.where(j == 0, nb, b)) if ctx_rows else (lambda b, j: b)

    def im(f):
        return lambda b, j, *_: f(b, j)

    in_specs, args = [], []
    if dual:
        in_specs.append(pl.BlockSpec((1, tile, d), im(lambda b, j: (b, jnp.maximum(j - 1, 0), 0))))
        in_specs.append(pl.BlockSpec((1, tile, d), im(lambda b, j: (b, 0, 0))))
        args += [x, ctx]
    else:
        in_specs.append(pl.BlockSpec((tile, d), im(lambda b, j: (b * x_per + x_off + j, 0))))
        args.append(x)
    if gather:
        in_specs.append(pl.BlockSpec(memory_space=pl.ANY))
        in_specs.append(pl.BlockSpec((tile, 2), im(lambda b, j: (b * per + j, 0))))
        args += [y, pos_w]
    else:
        in_specs.append(pl.BlockSpec((tile, d), im(lambda b, j: (b * per + j, 0))))
        args.append(y)
    in_specs.append(pl.BlockSpec((1, 1, d), im(lambda b, j: (prow(b, j), 0, gate_col))))
    in_specs.append(pl.BlockSpec((1, d), im(lambda b, j: (0, 0))))
    in_specs.append(pl.BlockSpec((1, d), im(lambda b, j: (0, 0))))
    args += [ada3, lng.reshape(1, d), lnb.reshape(1, d)]
    if nxt is not None:
        ada_n, shc, scc = nxt
        in_specs.append(pl.BlockSpec((1, 1, d), im(lambda b, j: (prow(b, j), 0, shc))))
        in_specs.append(pl.BlockSpec((1, 1, d), im(lambda b, j: (prow(b, j), 0, scc))))
        args += [ada_n, ada_n]
    if router_wt is not None:
        ne = router_wt.shape[0]
        in_specs.append(pl.BlockSpec((ne, d), im(lambda b, j: (0, 0))))
        args.append(router_wt)

    out_specs = [pl.BlockSpec((tile, d), im(lambda b, j: (b * per + j, 0)))]
    out_shape = [jax.ShapeDtypeStruct((rows, d), F32)]
    if nxt is not None and h_rows3d:
        out_specs.append(pl.BlockSpec((tile, d // LANES, LANES), im(lambda b, j: (b * per + j, 0, 0))))
        out_shape.append(jax.ShapeDtypeStruct((rows, d // LANES, LANES), h_dtype))
    elif nxt is not None:
        out_specs.append(pl.BlockSpec((tile, d), im(lambda b, j: (b * per + j, 0))))
        out_shape.append(jax.ShapeDtypeStruct((rows, d), h_dtype))
    if router_wt is not None:
        out_specs.append(pl.BlockSpec((ne, tile), im(lambda b, j: (0, b * per + j))))
        out_shape.append(jax.ShapeDtypeStruct((ne, rows), F32))

    scratch = [pltpu.VMEM((2, 2, tile) + y.shape[1:], y.dtype), pltpu.SemaphoreType.DMA((2,))] if gather else []
    kern = functools.partial(_ln_kernel, dual=dual, gather=gather, tile=tile, has_h=nxt is not None,
                             has_router=router_wt is not None, alpha=alpha)
    call = pl.pallas_call(
        kern,
        grid_spec=pltpu.PrefetchScalarGridSpec(
            num_scalar_prefetch=1 if gather else 0,
            grid=(nb, per),
            in_specs=in_specs,
            out_specs=out_specs,
            scratch_shapes=scratch,
        ),
        out_shape=out_shape,
        compiler_params=_cparams("arbitrary", "arbitrary"),
        name="ln_gather" if gather else "ln_block",
    )
    return call(*(([pos] if gather else []) + args))


def _router_kernel(b_ref, l_ref, g_ref, m_ref, *, ne):
    epg = ne // N_GROUPS
    logits = l_ref[...]
    s = [_sigmoid(logits[e:e + 1, :]) for e in range(ne)]
    sel = [s[e] + b_ref[e] for e in range(ne)]
    gscore = []
    for g in range(N_GROUPS):
        v = sel[g * epg:(g + 1) * epg]
        best = None
        for i in range(epg):
            for j in range(i + 1, epg):
                pair = v[i] + v[j]
                best = pair if best is None else jnp.maximum(best, pair)
        gscore.append(best)
    chosen = []
    for g in range(N_GROUPS):
        ok = None
        for o in range(N_GROUPS):
            if o == g:
                continue
            t = (gscore[g] > gscore[o]) if o < g else (gscore[g] >= gscore[o])
            ok = t if ok is None else jnp.logical_and(ok, t)
        chosen.append(ok)
    mask = []
    for g in range(N_GROUPS):
        v = sel[g * epg:(g + 1) * epg]
        for i in range(epg):
            rank = jnp.zeros_like(v[i])
            for j in range(epg):
                if j == i:
                    continue
                ahead = (v[j] >= v[i]) if j < i else (v[j] > v[i])
                rank = rank + jnp.where(ahead, 1.0, 0.0)
            mask.append(jnp.logical_and(chosen[g], rank < TOP_K))
    wsum = jnp.zeros_like(s[0])
    for e in range(ne):
        wsum = wsum + jnp.where(mask[e], s[e], 0.0)
    for e in range(ne):
        g_ref[e:e + 1, :] = jnp.where(mask[e], s[e] / wsum, 0.0)
        m_ref[e:e + 1, :] = jnp.where(mask[e], 1, 0).astype(jnp.int32)


def router(logits_t, router_b):
    ne, t = logits_t.shape
    tl = _pick(t, (1024, 512, 256, 128))
    spec = pl.BlockSpec((ne, tl), lambda i: (0, i))
    return pl.pallas_call(
        functools.partial(_router_kernel, ne=ne),
        grid=(t // tl,),
        in_specs=[pl.BlockSpec(memory_space=pltpu.SMEM), spec],
        out_specs=[spec, spec],
        out_shape=[jax.ShapeDtypeStruct((ne, t), F32), jax.ShapeDtypeStruct((ne, t), jnp.int32)],
        compiler_params=_cparams("parallel"),
        name="router",
    )(router_b, logits_t)


def dispatch_plan(gates, mask, tm):
    ne, t = mask.shape
    n_tiles = (TOP_K * t) // tm + ne
    rows = n_tiles * tm
    sel = mask > 0
    counts = jnp.sum(mask, axis=1)
    ptiles = (counts + tm - 1) // tm
    tile_end = jnp.cumsum(ptiles)
    off = (tile_end - ptiles) * tm
    dest = off[:, None] + jnp.cumsum(mask, axis=1) - mask
    pos_lo = jnp.min(jnp.where(sel, dest, rows), axis=0)
    pos_hi = jnp.max(jnp.where(sel, dest, -1), axis=0)
    g_lo = jnp.sum(jnp.where(sel & (dest == pos_lo[None, :]), gates, 0.0), axis=0)
    g_hi = jnp.sum(jnp.where(sel & (dest == pos_hi[None, :]), gates, 0.0), axis=0)
    pos = jnp.concatenate([pos_lo, pos_hi]).astype(jnp.int32)
    n_valid = tile_end[-1]
    tiles = jnp.arange(n_tiles, dtype=jnp.int32)
    tid = jnp.minimum(tiles, n_valid - 1)
    tile_expert = jnp.minimum(jnp.sum(tile_end[None, :] <= tid[:, None], axis=1), ne - 1).astype(jnp.int32)
    prev = jnp.concatenate([jnp.full((1,), -1, jnp.int32), tile_expert[:-1]])
    tile_first = (tile_expert != prev).astype(jnp.int32)
    last = jnp.any((tiles[:, None] == tile_end[None, :] - 1) & (ptiles[None, :] > 0), axis=1)
    tile_fill = (last | (tiles >= n_valid)).astype(jnp.int32)
    nv = jnp.reshape(n_valid, (1,)).astype(jnp.int32)
    return tile_fill, jnp.stack([g_lo, g_hi], axis=1), pos, tid, tile_expert, tile_first, nv


def _scatter_kernel(pos_ref, fill_ref, x_ref, o_ref, stage_ref, zero_ref, sems, zsem, *, tile, tm):
    i = pl.program_id(0)
    n = pl.num_programs(0)
    ntok = pos_ref.shape[0] // 2
    slot = i % 2

    def slot_wait(s):
        for _ in range(2):
            pltpu.make_async_copy(stage_ref.at[s], o_ref.at[pl.ds(0, tile)], sems.at[s]).wait()

    @pl.when(i == 0)
    def _():
        zero_ref[...] = jnp.zeros_like(zero_ref)

        def fill(t, wait):
            @pl.when(fill_ref[t] == 1)
            def _():
                cp = pltpu.make_async_copy(zero_ref, o_ref.at[pl.ds(t * tm, tm)], zsem)
                cp.wait() if wait else cp.start()

        lax.fori_loop(0, fill_ref.shape[0], lambda t, c: (fill(t, False), c)[1], 0)
        lax.fori_loop(0, fill_ref.shape[0], lambda t, c: (fill(t, True), c)[1], 0)

    @pl.when(i >= 2)
    def _():
        slot_wait(slot)

    stage_ref[slot] = x_ref[...]

    def start(r, carry):
        for k in range(2):
            dst = pos_ref[k * ntok + i * tile + r]
            pltpu.make_async_copy(stage_ref.at[slot, r], o_ref.at[dst], sems.at[slot]).start(priority=k)
        return carry

    lax.fori_loop(0, tile, start, 0, unroll=4)

    @pl.when(i == n - 1)
    def _():
        slot_wait(slot)

        @pl.when(n >= 2)
        def _():
            slot_wait(1 - slot)


def scatter_rows(h, pos, tile_fill, tm, tile):
    t = h.shape[0]
    rows = tile_fill.shape[0] * tm
    return pl.pallas_call(
        functools.partial(_scatter_kernel, tile=tile, tm=tm),
        grid_spec=pltpu.PrefetchScalarGridSpec(
            num_scalar_prefetch=2,
            grid=(t // tile,),
            in_specs=[pl.BlockSpec((tile,) + h.shape[1:], lambda i, p, f: (i, 0, 0))],
            out_specs=pl.BlockSpec(memory_space=pl.ANY),
            scratch_shapes=[pltpu.VMEM((2, tile) + h.shape[1:], h.dtype), pltpu.VMEM((tm,) + h.shape[1:], h.dtype),
                            pltpu.SemaphoreType.DMA((2,)), pltpu.SemaphoreType.DMA(())],
        ),
        out_shape=jax.ShapeDtypeStruct((rows,) + h.shape[1:], h.dtype),
        compiler_params=_cparams("arbitrary"),
        name="moe_scatter",
    )(pos, tile_fill, h)


def _ffn_a_kernel(tid_ref, te_ref, tf_ref, nv_ref, x_ref, w1_ref, w3_ref, o_ref, w1b, w3b):
    i = pl.program_id(1)

    @pl.when(i < nv_ref[0])
    def _():
        @pl.when(tf_ref[i] == 1)
        def _():
            w1b[...] = w1_ref[0].astype(BF16)
            w3b[...] = w3_ref[0].astype(BF16)

        x = x_ref[...].reshape(x_ref.shape[0], -1)
        a = jnp.dot(x, w1b[...], preferred_element_type=F32)
        b = jnp.dot(x, w3b[...], preferred_element_type=F32)
        o_ref[...] = (a * _sigmoid(a) * b).astype(o_ref.dtype)

    @pl.when(i >= nv_ref[0])
    def _():
        o_ref[...] = jnp.zeros_like(o_ref)


def ffn_stage_a(xs, w1, w3, layer, plan, tm):
    _, _, _, tid, te, tf, nv = plan
    rows = xs.shape[0]
    d = xs.shape[1] * xs.shape[2]
    ff = w1.shape[3]
    tfc = _pick(ff, (512, 256, 128))
    wspec = pl.BlockSpec((None, 1, d, tfc), lambda f, i, tid, te, tf, nv: (layer, te[i], 0, f))
    return pl.pallas_call(
        _ffn_a_kernel,
        grid_spec=pltpu.PrefetchScalarGridSpec(
            num_scalar_prefetch=4,
            grid=(ff // tfc, rows // tm),
            in_specs=[pl.BlockSpec((tm,) + xs.shape[1:], lambda f, i, tid, te, tf, nv: (tid[i], 0, 0)), wspec, wspec],
            out_specs=pl.BlockSpec((tm, tfc), lambda f, i, tid, te, tf, nv: (i, f)),
            scratch_shapes=[pltpu.VMEM((d, tfc), BF16), pltpu.VMEM((d, tfc), BF16)],
        ),
        out_shape=jax.ShapeDtypeStruct((rows, ff), BF16),
        compiler_params=_cparams("arbitrary", "arbitrary"),
        name="moe_ffn_a",
    )(tid, te, tf, nv, xs, w1, w3)


def _ffn_b_kernel(tid_ref, te_ref, tf_ref, nv_ref, h_ref, w2_ref, o_ref, w2b):
    i = pl.program_id(1)

    @pl.when(i < nv_ref[0])
    def _():
        @pl.when(tf_ref[i] == 1)
        def _():
            w2b[...] = w2_ref[0].astype(BF16)

        y = jnp.dot(h_ref[...], w2b[...], preferred_element_type=F32)
        o_ref[...] = y.astype(o_ref.dtype).reshape(o_ref.shape)

    @pl.when(i >= nv_ref[0])
    def _():
        o_ref[...] = jnp.zeros_like(o_ref)


def ffn_stage_b(hs, w2, layer, plan, tm):
    _, _, _, tid, te, tf, nv = plan
    rows, ff = hs.shape
    d = w2.shape[3]
    tn = _pick(d, (2048, 1024, 512, 256, 128))
    return pl.pallas_call(
        _ffn_b_kernel,
        grid_spec=pltpu.PrefetchScalarGridSpec(
            num_scalar_prefetch=4,
            grid=(d // tn, rows // tm),
            in_specs=[
                pl.BlockSpec((tm, ff), lambda n, i, tid, te, tf, nv: (tid[i], 0)),
                pl.BlockSpec((None, 1, ff, tn), lambda n, i, tid, te, tf, nv: (layer, te[i], 0, n)),
            ],
            out_specs=pl.BlockSpec((tm, tn // LANES, LANES), lambda n, i, tid, te, tf, nv: (i, n, 0)),
            scratch_shapes=[pltpu.VMEM((ff, tn), BF16)],
        ),
        out_shape=jax.ShapeDtypeStruct((rows, d // LANES, LANES), BF16),
        compiler_params=_cparams("arbitrary", "arbitrary"),
        name="moe_ffn_b",
    )(tid, te, tf, nv, hs, w2)


def moe_sparse(h, logits_t, router_b, w1, w3, w2, layer, tile):
    gates, mask = router(logits_t, router_b)
    plan = dispatch_plan(gates, mask, MOE_TILE)
    xs = scatter_rows(h, plan[2], plan[0], MOE_TILE, tile)
    hs = ffn_stage_a(xs, w1, w3, layer, plan, MOE_TILE)
    ys = ffn_stage_b(hs, w2, layer, plan, MOE_TILE)
    return ys, plan[2], plan[1]


def rope_tables(n, lc, hd):
    half = hd // 2
    t = jnp.arange(n)
    inv = jnp.power(ROPE_BASE, -jnp.arange(0, half, 2, dtype=F32) / half)
    ang_r = (t // GRID_W).astype(F32)[:, None] * inv
    ang_c = (t % GRID_W).astype(F32)[:, None] * inv
    cos = jnp.concatenate([jnp.cos(ang_r)] * 2 + [jnp.cos(ang_c)] * 2, axis=1)
    sin = jnp.concatenate([-jnp.sin(ang_r), jnp.sin(ang_r), -jnp.sin(ang_c), jnp.sin(ang_c)], axis=1)
    cos = jnp.concatenate([jnp.ones((lc, hd), F32), cos], axis=0)
    sin = jnp.concatenate([jnp.zeros((lc, hd), F32), sin], axis=0)
    scale = hd ** -0.5
    ones, zeros = jnp.ones_like(cos), jnp.zeros_like(sin)
    return jnp.stack([cos * scale, cos, ones]), jnp.stack([sin * scale, sin, zeros])


def kernel(x, c, ctx, c_ctx, ada_w, ada_b, ln_g, ln_b, mlstm_w_in, mlstm_b_gates, mlstm_norm_w, mlstm_w_out,
           attn_w_in, attn_b_in, attn_sink, attn_w_out, router_w, router_b, moe_w1, moe_w3, moe_w2):
    bsz, n, d = x.shape
    lc = ctx.shape[1]
    depth = ada_w.shape[0]
    assert depth == 2 and bsz <= 4 and n % lc == 0 and lc % ML_CHUNK == 0
    alpha = (2 * depth) ** 0.25
    tile = lc
    per = (n + lc) // tile
    per_lat = n // tile

    cc = jnp.concatenate([c, c_ctx[None, :], jnp.zeros((8 - bsz - 1, d), F32)], axis=0)
    ada0 = adaln(cc, ada_w, ada_b, 0).reshape(8, 1, 6 * d)
    rwt = router_w.T

    heads = mlstm_b_gates.shape[1] // 4
    dv = d // heads
    dqk = dv // 2
    nmain = 2 * heads * (dqk + dv)
    w_in_t = jnp.swapaxes(mlstm_w_in, 1, 2)
    h0, gates_pre = mod0(x, ctx, ada0, w_in_t[0, nmain:, :], mlstm_b_gates[0][None, :])
    qkvo = matmul(h0, w_in_t, n_cols=nmain, w_t=True)
    ncc, nct = lc // ML_CHUNK, (n + lc) // ML_CHUNK
    hg, ada1 = mlstm_bidir(qkvo, gates_pre, mlstm_norm_w[0], bsz, heads, dqk, dv, ncc, nct, cc, ada_w, ada_b, 1)
    ada1 = ada1.reshape(8, 1, 6 * d)
    y = matmul(hg, mlstm_w_out, out_dtype=F32)
    x1, hffn, lg = ln_block(x, y, ada0, 2, ln_g[0, 0], ln_b[0, 0], alpha=alpha, nb=bsz, per=per, tile=tile,
                            ctx_rows=True, ctx=ctx, nxt=(ada0, 3, 4), h_dtype=BF16, h_rows3d=True, router_wt=rwt)
    ys, pos, pos_w = moe_sparse(hffn, lg, router_b, moe_w1, moe_w3, moe_w2, 0, tile)
    x2, h1 = ln_block(x1, ys, ada0, 5, ln_g[0, 1], ln_b[0, 1], alpha=alpha, nb=bsz, per=per, tile=tile,
                      ctx_rows=True, x_per=per, nxt=(ada1, 0, 1), h_dtype=BF16, pos=pos, pos_w=pos_w)

    heads_a = attn_sink.shape[1]
    hd = d // heads_a
    kvh = (attn_w_in.shape[2] // hd - heads_a) // 2
    cos_t, sin_t = rope_tables(n, lc, hd)
    q_end, k_end = heads_a * hd, (heads_a + kvh) * hd
    type_of = lambda col: jnp.where(col < q_end, 0, jnp.where(col < k_end, 1, 2))
    tm1 = _pick(n + lc, (1152, 768, 384, 256, 128))
    qkv = matmul(h1, attn_w_in, bias=attn_b_in[0], rope=(cos_t, sin_t, n + lc, type_of), tm=tm1,
                 tn=_pick(kvh * hd, (512, 256, 128)))
    att = window_attention(qkv.reshape(bsz, n + lc, -1), attn_sink[0], lc, n, heads_a, kvh, hd)
    y = matmul(att.reshape(bsz * n, heads_a * hd), attn_w_out, out_dtype=F32)
    x3, hffn, lg = ln_block(x2, y, ada1, 2, ln_g[1, 0], ln_b[1, 0], alpha=alpha, nb=bsz, per=per_lat, tile=tile,
                            ctx_rows=False, x_per=per, x_off=1, nxt=(ada1, 3, 4), h_dtype=BF16, h_rows3d=True,
                            router_wt=rwt)
    ys, pos, pos_w = moe_sparse(hffn, lg, router_b, moe_w1, moe_w3, moe_w2, 1, tile)
    (x4,) = ln_block(x3, ys, ada1, 5, ln_g[1, 1], ln_b[1, 1], alpha=alpha, nb=bsz, per=per_lat, tile=tile,
                     ctx_rows=False, x_per=per_lat, pos=pos, pos_w=pos_w)
    return x4.reshape(bsz, n, d)
```

```python
import functools

import jax
import jax.numpy as jnp
from jax import lax
from jax.experimental import pallas as pl
from jax.experimental.pallas import tpu as pltpu

F32 = jnp.float32
BF16 = jnp.bfloat16
NEG_INF = float("-inf")

GRID_W = 64
ML_CHUNK = 128
AT_WINDOW = 128
AT_BLOCK = 128
ROPE_BASE = 10000.0
N_GROUPS = 4
TOP_K = 2
LN_EPS = 1e-5
RMS_EPS = 1e-6

LANES = 128
VMEM_LIMIT = 56 * 1024 * 1024
MOE_TILE = 512
HIGHEST = lax.Precision.HIGHEST
NT_DIMS = (((1,), (1,)), ((), ()))


def _cparams(*sem):
    return pltpu.CompilerParams(dimension_semantics=sem, vmem_limit_bytes=VMEM_LIMIT)


def _pick(n, candidates):
    for c in candidates:
        if n % c == 0:
            return c
    return n


def _sigmoid(x):
    return 1.0 / (1.0 + jnp.exp(-x))


def _log_sigmoid(x):
    return jnp.minimum(x, 0.0) - jnp.log1p(jnp.exp(-jnp.abs(x)))


def _adaln_block(c_ref, w_ref, b_ref):
    c = c_ref[...]
    a = (c * _sigmoid(c)).astype(BF16)
    return jnp.dot(a, w_ref[...].astype(BF16), preferred_element_type=F32) + b_ref[...]


def _adaln_kernel(c_ref, w_ref, b_ref, o_ref):
    o_ref[...] = _adaln_block(c_ref, w_ref, b_ref)


def adaln(cc, ada_w, ada_b, layer):
    depth, d, n6 = ada_w.shape
    tn = _pick(n6, (512, 256, 128))
    return pl.pallas_call(
        _adaln_kernel,
        grid=(n6 // tn,),
        in_specs=[
            pl.BlockSpec((8, d), lambda j: (0, 0)),
            pl.BlockSpec((None, d, tn), lambda j: (layer, 0, j)),
            pl.BlockSpec((None, 1, tn), lambda j: (layer, 0, j)),
        ],
        out_specs=pl.BlockSpec((8, tn), lambda j: (0, j)),
        out_shape=jax.ShapeDtypeStruct((8, n6), F32),
        compiler_params=_cparams("parallel"),
        name="adaln",
    )(cc, ada_w, ada_b.reshape(depth, 1, n6))


def _mod0_kernel(x_ref, c_ref, sh_ref, sc_ref, wg_ref, bg_ref, h_ref, g_ref):
    j = pl.program_id(1)
    t = jnp.where(j == 0, c_ref[0], x_ref[0])
    h = t * (1.0 + sc_ref[0]) + sh_ref[0]
    h_ref[...] = h.astype(BF16)
    g_ref[...] = lax.dot_general(h.astype(BF16), wg_ref[...].astype(BF16), NT_DIMS,
                                 preferred_element_type=F32) + bg_ref[...]


def mod0(x, ctx, ada3, wg, bg):
    bsz, n, d = x.shape
    lc = ctx.shape[1]
    per = (n + lc) // lc
    ng = wg.shape[0]
    prow = lambda b, j: jnp.where(j == 0, bsz, b)
    return pl.pallas_call(
        _mod0_kernel,
        grid=(bsz, per),
        in_specs=[
            pl.BlockSpec((1, lc, d), lambda b, j: (b, jnp.maximum(j - 1, 0), 0)),
            pl.BlockSpec((1, lc, d), lambda b, j: (b, 0, 0)),
            pl.BlockSpec((1, 1, d), lambda b, j: (prow(b, j), 0, 0)),
            pl.BlockSpec((1, 1, d), lambda b, j: (prow(b, j), 0, 1)),
            pl.BlockSpec((ng, d), lambda b, j: (0, 0)),
            pl.BlockSpec((1, ng), lambda b, j: (0, 0)),
        ],
        out_specs=[
            pl.BlockSpec((lc, d), lambda b, j: (b * per + j, 0)),
            pl.BlockSpec((lc, ng), lambda b, j: (b * per + j, 0)),
        ],
        out_shape=[
            jax.ShapeDtypeStruct((bsz * (n + lc), d), BF16),
            jax.ShapeDtypeStruct((bsz * (n + lc), ng), F32),
        ],
        compiler_params=_cparams("parallel", "arbitrary"),
        name="mod0",
    )(x, ctx, ada3, ada3, wg, bg)


def _mm_kernel(*refs, has_bias, has_rope, tn, w_t):
    x_ref, w_ref = refs[0], refs[1]
    pos = 2
    if w_t:
        acc = lax.dot_general(x_ref[...], w_ref[...].astype(BF16), NT_DIMS, preferred_element_type=F32)
    else:
        acc = jnp.dot(x_ref[...], w_ref[...].astype(BF16), preferred_element_type=F32)
    if has_bias:
        acc = acc + refs[pos][...]
        pos += 1
    o_ref = refs[-1]
    if has_rope:
        ca = refs[pos][0]
        sb = refs[pos + 1][0]
        lane = lax.broadcasted_iota(jnp.int32, ca.shape, 1)
        first = (lane % 64) < 32
        for h in range(tn // LANES):
            y = acc[:, h * LANES:(h + 1) * LANES]
            sw = jnp.where(first, pltpu.roll(y, 96, 1), pltpu.roll(y, 32, 1))
            o_ref[:, h * LANES:(h + 1) * LANES] = (y * ca + sw * sb).astype(o_ref.dtype)
    else:
        o_ref[...] = acc.astype(o_ref.dtype)


def matmul(x, w, bias=None, rope=None, out_dtype=BF16, tm=None, tn=None, n_cols=None, w_t=False):
    m, k = x.shape
    n = n_cols or w.shape[1 if w_t else 2]
    tm = tm or _pick(m, (1152, 1024, 768, 512, 256, 128))
    tn = tn or _pick(n, (512, 256, 128))
    in_specs = [
        pl.BlockSpec((tm, k), lambda i, j: (i, 0)),
        pl.BlockSpec((None, tn, k), lambda i, j: (0, j, 0)) if w_t else
        pl.BlockSpec((None, k, tn), lambda i, j: (0, 0, j)),
    ]
    args = [x, w]
    if bias is not None:
        in_specs.append(pl.BlockSpec((1, tn), lambda i, j: (0, j)))
        args.append(bias.reshape(1, n))
    if rope is not None:
        ca, sb, period, type_of = rope
        nper = period // tm
        spec = pl.BlockSpec((1, tm, LANES), lambda i, j: (type_of(j * tn), i % nper, 0))
        in_specs += [spec, spec]
        args += [ca, sb]
    return pl.pallas_call(
        functools.partial(_mm_kernel, has_bias=bias is not None, has_rope=rope is not None, tn=tn, w_t=w_t),
        grid=(m // tm, n // tn),
        in_specs=in_specs,
        out_specs=pl.BlockSpec((tm, tn), lambda i, j: (i, j)),
        out_shape=jax.ShapeDtypeStruct((m, n), out_dtype),
        compiler_params=_cparams("parallel", "arbitrary"),
        name="matmul",
    )(*args)


def _mlstm_chunks(chains, scale):
    L = ML_CHUNK
    r = lax.broadcasted_iota(jnp.int32, (L, L), 0)
    c = lax.broadcasted_iota(jnp.int32, (L, L), 1)
    gate = []
    for q, k, v, ig_row, f_row, ig_col, f_col, st, n_row, m_prev, d in chains:
        reverse = d == 1
        lf_row = _log_sigmoid(f_row)
        lf_col = _log_sigmoid(f_col)
        keep = (c >= r) if reverse else (c <= r)
        keep_t = (r >= c) if reverse else (r <= c)
        b_col = jnp.sum(jnp.where(keep, lf_row, 0.0), axis=1, keepdims=True)
        b_row = jnp.sum(jnp.where(keep_t, lf_col, 0.0), axis=0, keepdims=True)
        logw = jnp.where(keep, b_col - b_row + ig_row, NEG_INF)
        inter = b_col + m_prev
        m_t = jnp.maximum(inter, jnp.max(logw, axis=1, keepdims=True))
        w_inter = jnp.exp(inter - m_t)
        p = jnp.exp(logw - m_t)
        b_end = b_col[0:1, :] if reverse else b_col[L - 1:L, :]
        m_new = jnp.maximum(b_end + m_prev, jnp.max(b_end - b_row + ig_row, axis=1, keepdims=True))
        decay = jnp.exp(b_end + m_prev - m_new)
        w_src = jnp.exp(b_end - b_col + ig_col - m_new)
        gate.append((m_t, w_inter, p, m_new, decay, w_src))

    qk = [lax.dot_general(ch[0], ch[1], NT_DIMS, preferred_element_type=F32) for ch in chains]
    qs = [jnp.dot(ch[0], ch[7].astype(BF16), preferred_element_type=F32) for ch in chains]
    s = [a * (g[2] * scale) for a, g in zip(qk, gate)]
    sv = [jnp.dot(a.astype(BF16), ch[2], preferred_element_type=F32) for a, ch in zip(s, chains)]
    kw = [ch[1].astype(F32) * g[5] for ch, g in zip(chains, gate)]
    kv = [jnp.dot(a.T.astype(BF16), ch[2], preferred_element_type=F32) for a, ch in zip(kw, chains)]

    out = []
    for ch, g, s_i, qs_i, sv_i, kw_i, kv_i in zip(chains, gate, s, qs, sv, kw, kv):
        q, n_row, st = ch[0], ch[8], ch[7]
        m_t, w_inter, _, m_new, decay, _ = g
        num = (w_inter * scale) * qs_i + sv_i
        qn = jnp.sum(q.astype(F32) * n_row, axis=1, keepdims=True) * scale
        den = w_inter * qn + jnp.sum(s_i, axis=1, keepdims=True)
        h = num * (1.0 / jnp.maximum(jnp.abs(den), jnp.exp(-m_t)))
        out.append((h, decay * st + kv_i, decay * n_row + jnp.sum(kw_i, axis=0, keepdims=True), m_new))
    return out


def _mlstm_kernel(q_ref, k_ref, v_ref, o_ref, gr_ref, gc_ref, nw_ref, cc_ref, aw_ref, ab_ref, out_ref, ada_ref,
                  acc_ref, st_ref, n_ref, m_ref, *, scale, ncc, nct, hpb):
    L = ML_CHUNK
    ada_ref[...] = _adaln_block(cc_ref, aw_ref, ab_ref)
    st_ref[...] = jnp.zeros_like(st_ref)
    n_ref[...] = jnp.zeros_like(n_ref)
    m_ref[...] = jnp.zeros_like(m_ref)
    acc_ref[...] = jnp.zeros_like(acc_ref)

    dqk, dv = st_ref.shape[1], st_ref.shape[2]

    def step(j, carry):
        where, chains = [], []
        for hh in range(hpb):
            for d, cj in ((0, j), (1, jnp.where(j < ncc, ncc - 1 - j, nct - 1 - (j - ncc)))):
                rows = pl.ds(pl.multiple_of(cj * L, L), L)
                slot = 2 * hh + d
                gcol = gc_ref[0, hh, cj]
                where.append((rows, hh, slot, acc_ref[rows, hh * dv:(hh + 1) * dv]))
                chains.append((q_ref[rows, hh * dqk:(hh + 1) * dqk], k_ref[rows, hh * dqk:(hh + 1) * dqk],
                               v_ref[rows, hh * dv:(hh + 1) * dv],
                               gr_ref[0, 2 * d, hh, cj], gr_ref[0, 2 * d + 1, hh, cj],
                               gcol[:, 2 * d:2 * d + 1], gcol[:, 2 * d + 1:2 * d + 2],
                               st_ref[slot], n_ref[slot], m_ref[slot], d))
        for (rows, hh, slot, acc), (h, st_new, n_new, m_new) in zip(where, _mlstm_chunks(chains, scale)):
            acc_ref[rows, hh * dv:(hh + 1) * dv] = acc + h
            st_ref[slot] = st_new
            n_ref[slot] = n_new
            m_ref[slot] = m_new
        return carry

    lax.fori_loop(0, nct, step, 0, unroll=2)

    def finish(cj, carry):
        rows = pl.ds(pl.multiple_of(cj * L, L), L)
        for hh in range(hpb):
            cols = slice(hh * dv, (hh + 1) * dv)
            hs = acc_ref[rows, cols]
            hn = hs * lax.rsqrt(jnp.mean(hs * hs, axis=1, keepdims=True) + RMS_EPS)
            out_ref[rows, cols] = (hn * nw_ref[:, cols]
                                   * _sigmoid(o_ref[rows, cols].astype(F32))).astype(out_ref.dtype)
        return carry

    lax.fori_loop(0, nct, finish, 0)


def mlstm_bidir(qkvo, gates_pre, norm_w, bsz, heads, dqk, dv, ncc, nct, cc, ada_w, ada_b, ada_layer):
    t = qkvo.shape[0]
    L = ML_CHUNK
    n2 = nct * L
    hpb = 1
    depth, d, n6 = ada_w.shape
    nsteps = bsz * heads // hpb
    ta = n6 // nsteps
    assert n6 % nsteps == 0 and ta % LANES == 0
    assert ncc % 2 == 0 and (nct - ncc) % 2 == 0
    g4 = gates_pre.reshape(bsz, nct, L, 4, heads)
    g_rows = jnp.transpose(g4, (0, 3, 4, 1, 2)).reshape(bsz, 4, heads, nct, 1, L)
    g_cols = jnp.transpose(g4, (0, 4, 1, 2, 3))
    koff = heads // hpb
    voff = 2 * heads * dqk // (dv * hpb)
    ooff = voff + heads // hpb
    return pl.pallas_call(
        functools.partial(_mlstm_kernel, scale=dqk ** -0.5, ncc=ncc, nct=nct, hpb=hpb),
        grid=(bsz, heads // hpb),
        in_specs=[
            pl.BlockSpec((n2, hpb * dqk), lambda b, h: (b, h)),
            pl.BlockSpec((n2, hpb * dqk), lambda b, h: (b, koff + h)),
            pl.BlockSpec((n2, hpb * dv), lambda b, h: (b, voff + h)),
            pl.BlockSpec((n2, hpb * dv), lambda b, h: (b, ooff + h)),
            pl.BlockSpec((1, 4, hpb, nct, 1, L), lambda b, h: (b, 0, h, 0, 0, 0)),
            pl.BlockSpec((1, hpb, nct, L, 4), lambda b, h: (b, h, 0, 0, 0)),
            pl.BlockSpec((1, hpb * dv), lambda b, h: (0, h)),
            pl.BlockSpec((8, d), lambda b, h: (0, 0)),
            pl.BlockSpec((None, d, ta), lambda b, h: (ada_layer, 0, b * (heads // hpb) + h)),
            pl.BlockSpec((None, 1, ta), lambda b, h: (ada_layer, 0, b * (heads // hpb) + h)),
        ],
        out_specs=[pl.BlockSpec((n2, hpb * dv), lambda b, h: (b, h)),
                   pl.BlockSpec((8, ta), lambda b, h: (0, b * (heads // hpb) + h))],
        out_shape=[jax.ShapeDtypeStruct((t, heads * dv), BF16), jax.ShapeDtypeStruct((8, n6), F32)],
        scratch_shapes=[pltpu.VMEM((n2, hpb * dv), F32), pltpu.VMEM((2 * hpb, dqk, dv), F32),
                        pltpu.VMEM((2 * hpb, 1, dqk), F32), pltpu.VMEM((2 * hpb, 1, 1), F32)],
        compiler_params=_cparams("parallel", "parallel"),
        name="mlstm",
    )(qkvo, qkvo, qkvo, qkvo, g_rows, g_cols, norm_w.reshape(1, heads * dv), cc, ada_w,
      ada_b.reshape(depth, 1, n6))


def _attn_kernel(sink_ref, q_ref, k_ref, v_ref, o_ref, bias_ref, *, lc, nb, group, hd):
    kv = pl.program_id(1)
    blk = AT_BLOCK
    rows = group * blk
    kc = k_ref[0, 0:lc, :]
    vc = v_ref[0, 0:lc, :]
    rid = lax.broadcasted_iota(jnp.int32, (rows, 1), 0)
    sink_col = jnp.zeros((rows, 1), F32)
    for g in range(group):
        sink_col = jnp.where(rid // blk == g, sink_ref[kv * group + g], sink_col)
    qrow = lax.broadcasted_iota(jnp.int32, (rows, 3 * blk), 0) % blk
    kcol = lax.broadcasted_iota(jnp.int32, (rows, 3 * blk), 1)
    for back in range(3):
        rel = kcol - back * blk - qrow
        bias_ref[back] = jnp.where(jnp.abs(rel) <= AT_WINDOW, 0.0, NEG_INF)

    def body(j, carry):
        ws = jnp.clip(j - 1, 0, nb - 3)
        qs = pl.multiple_of(lc + j * blk, blk)
        ks = pl.multiple_of(lc + ws * blk, blk)
        q4 = q_ref[0, pl.ds(qs, blk), :]
        qq = jnp.concatenate([q4[:, g * hd:(g + 1) * hd] for g in range(group)], axis=0)
        kw = k_ref[0, pl.ds(ks, 3 * blk), :]
        vw = v_ref[0, pl.ds(ks, 3 * blk), :]
        s_loc = lax.dot_general(qq, kw, NT_DIMS, preferred_element_type=F32)
        s_ctx = lax.dot_general(qq, kc, NT_DIMS, preferred_element_type=F32)
        s_loc = s_loc + bias_ref[j - ws]
        m = jnp.maximum(jnp.maximum(jnp.max(s_loc, axis=1, keepdims=True),
                                    jnp.max(s_ctx, axis=1, keepdims=True)), sink_col)
        p_loc = jnp.exp(s_loc - m)
        p_ctx = jnp.exp(s_ctx - m)
        den = (jnp.sum(p_loc, axis=1, keepdims=True) + jnp.sum(p_ctx, axis=1, keepdims=True)
               + jnp.exp(sink_col - m))
        o = (jnp.dot(p_loc.astype(BF16), vw, preferred_element_type=F32)
             + jnp.dot(p_ctx.astype(BF16), vc, preferred_element_type=F32)) / den
        os = pl.multiple_of(j * blk, blk)
        for g in range(group):
            o_ref[0, pl.ds(os, blk), g * hd:(g + 1) * hd] = o[g * blk:(g + 1) * blk].astype(o_ref.dtype)
        return carry

    lax.fori_loop(0, nb, body, 0, unroll=4 if nb % 4 == 0 else 2)


def window_attention(qkv3, sink, lc, n, heads, kv_heads, hd):
    bsz = qkv3.shape[0]
    group = heads // kv_heads
    nb = n // AT_BLOCK
    assert nb >= 3
    koff = heads
    voff = heads + kv_heads
    return pl.pallas_call(
        functools.partial(_attn_kernel, lc=lc, nb=nb, group=group, hd=hd),
        grid=(bsz, kv_heads),
        in_specs=[
            pl.BlockSpec(memory_space=pltpu.SMEM),
            pl.BlockSpec((1, lc + n, group * hd), lambda b, g: (b, 0, g)),
            pl.BlockSpec((1, lc + n, hd), lambda b, g: (b, 0, koff + g)),
            pl.BlockSpec((1, lc + n, hd), lambda b, g: (b, 0, voff + g)),
        ],
        out_specs=pl.BlockSpec((1, n, group * hd), lambda b, g: (b, 0, g)),
        out_shape=jax.ShapeDtypeStruct((bsz, n, heads * hd), BF16),
        scratch_shapes=[pltpu.VMEM((3, group * AT_BLOCK, 3 * AT_BLOCK), F32)],
        compiler_params=_cparams("parallel", "parallel"),
        name="window_attention",
    )(sink, qkv3, qkv3, qkv3)


def _ln_kernel(*refs, dual, gather, tile, has_h, has_router, alpha):
    refs = list(refs)
    if gather:
        pos_ref = refs.pop(0)
    x_ref = refs.pop(0)
    c_ref = refs.pop(0) if dual else None
    y_ref = refs.pop(0)
    gw_ref = refs.pop(0) if gather else None
    gate_ref, lng_ref, lnb_ref = refs.pop(0), refs.pop(0), refs.pop(0)
    if has_h:
        sh_ref, sc_ref = refs.pop(0), refs.pop(0)
    if has_router:
        rw_ref = refs.pop(0)
    xo_ref = refs.pop(0)
    if has_h:
        ho_ref = refs.pop(0)
    if has_router:
        lg_ref = refs.pop(0)

    if gather:
        buf_ref, sems = refs.pop(0), refs.pop(0)
        lin = pl.program_id(0) * pl.num_programs(1) + pl.program_id(1)
        ntiles = pl.num_programs(0) * pl.num_programs(1)
        ntok = pos_ref.shape[0] // 2

        def issue(t):
            slot = t % 2

            def start(r, carry):
                for k in range(2):
                    src = pos_ref[k * ntok + t * tile + r]
                    pltpu.make_async_copy(y_ref.at[src], buf_ref.at[slot, k, r], sems.at[slot]).start(priority=k)
                return carry

            lax.fori_loop(0, tile, start, 0, unroll=4)

        @pl.when(lin == 0)
        def _():
            issue(lin)

        @pl.when(lin + 1 < ntiles)
        def _():
            issue(lin + 1)

        slot = lin % 2
        for k in range(2):
            pltpu.make_async_copy(y_ref.at[pl.ds(0, tile)], buf_ref.at[slot, k], sems.at[slot]).wait()
        gw = gw_ref[...]
        d = x_ref.shape[-1]
        y = (gw[:, 0:1] * buf_ref[slot, 0].reshape(tile, d).astype(F32)
             + gw[:, 1:2] * buf_ref[slot, 1].reshape(tile, d).astype(F32))
    else:
        y = y_ref[...].astype(F32)

    if dual:
        x = jnp.where(pl.program_id(1) == 0, c_ref[0], x_ref[0])
    else:
        x = x_ref[...]
    z = alpha * x + gate_ref[0] * y
    mu = jnp.mean(z, axis=1, keepdims=True)
    zc = z - mu
    var = jnp.mean(zc * zc, axis=1, keepdims=True)
    xn = zc * lax.rsqrt(var + LN_EPS) * lng_ref[...] + lnb_ref[...]
    xo_ref[...] = xn
    if has_h:
        h = xn * (1.0 + sc_ref[0]) + sh_ref[0]
        ho_ref[...] = h.reshape(ho_ref.shape).astype(ho_ref.dtype)
        if has_router:
            lg_ref[...] = lax.dot_general(rw_ref[...], h, NT_DIMS, preferred_element_type=F32,
                                          precision=HIGHEST)


def ln_block(x, y, ada3, gate_col, lng, lnb, *, alpha, nb, per, tile, ctx_rows, ctx=None, x_per=None,
             x_off=0, nxt=None, h_dtype=None, h_rows3d=False, router_wt=None, pos=None, pos_w=None):
    d = x.shape[-1]
    dual = ctx is not None
    gather = pos is not None
    rows = nb * per * tile
    prow = (lambda b, j: jnp.where(j == 0, nb, b)) if ctx_rows else (lambda b, j: b)

    def im(f):
        return lambda b, j, *_: f(b, j)

    in_specs, args = [], []
    if dual:
        in_specs.append(pl.BlockSpec((1, tile, d), im(lambda b, j: (b, jnp.maximum(j - 1, 0), 0))))
        in_specs.append(pl.BlockSpec((1, tile, d), im(lambda b, j: (b, 0, 0))))
        args += [x, ctx]
    else:
        in_specs.append(pl.BlockSpec((tile, d), im(lambda b, j: (b * x_per + x_off + j, 0))))
        args.append(x)
    if gather:
        in_specs.append(pl.BlockSpec(memory_space=pl.ANY))
        in_specs.append(pl.BlockSpec((tile, 2), im(lambda b, j: (b * per + j, 0))))
        args += [y, pos_w]
    else:
        in_specs.append(pl.BlockSpec((tile, d), im(lambda b, j: (b * per + j, 0))))
        args.append(y)
    in_specs.append(pl.BlockSpec((1, 1, d), im(lambda b, j: (prow(b, j), 0, gate_col))))
    in_specs.append(pl.BlockSpec((1, d), im(lambda b, j: (0, 0))))
    in_specs.append(pl.BlockSpec((1, d), im(lambda b, j: (0, 0))))
    args += [ada3, lng.reshape(1, d), lnb.reshape(1, d)]
    if nxt is not None:
        ada_n, shc, scc = nxt
        in_specs.append(pl.BlockSpec((1, 1, d), im(lambda b, j: (prow(b, j), 0, shc))))
        in_specs.append(pl.BlockSpec((1, 1, d), im(lambda b, j: (prow(b, j), 0, scc))))
        args += [ada_n, ada_n]
    if router_wt is not None:
        ne = router_wt.shape[0]
        in_specs.append(pl.BlockSpec((ne, d), im(lambda b, j: (0, 0))))
        args.append(router_wt)

    out_specs = [pl.BlockSpec((tile, d), im(lambda b, j: (b * per + j, 0)))]
    out_shape = [jax.ShapeDtypeStruct((rows, d), F32)]
    if nxt is not None and h_rows3d:
        out_specs.append(pl.BlockSpec((tile, d // LANES, LANES), im(lambda b, j: (b * per + j, 0, 0))))
        out_shape.append(jax.ShapeDtypeStruct((rows, d // LANES, LANES), h_dtype))
    elif nxt is not None:
        out_specs.append(pl.BlockSpec((tile, d), im(lambda b, j: (b * per + j, 0))))
        out_shape.append(jax.ShapeDtypeStruct((rows, d), h_dtype))
    if router_wt is not None:
        out_specs.append(pl.BlockSpec((ne, tile), im(lambda b, j: (0, b * per + j))))
        out_shape.append(jax.ShapeDtypeStruct((ne, rows), F32))

    scratch = [pltpu.VMEM((2, 2, tile) + y.shape[1:], y.dtype), pltpu.SemaphoreType.DMA((2,))] if gather else []
    kern = functools.partial(_ln_kernel, dual=dual, gather=gather, tile=tile, has_h=nxt is not None,
                             has_router=router_wt is not None, alpha=alpha)
    call = pl.pallas_call(
        kern,
        grid_spec=pltpu.PrefetchScalarGridSpec(
            num_scalar_prefetch=1 if gather else 0,
            grid=(nb, per),
            in_specs=in_specs,
            out_specs=out_specs,
            scratch_shapes=scratch,
        ),
        out_shape=out_shape,
        compiler_params=_cparams("arbitrary", "arbitrary"),
        name="ln_gather" if gather else "ln_block",
    )
    return call(*(([pos] if gather else []) + args))


def _router_kernel(b_ref, l_ref, g_ref, m_ref, *, ne):
    epg = ne // N_GROUPS
    logits = l_ref[...]
    s = [_sigmoid(logits[e:e + 1, :]) for e in range(ne)]
    sel = [s[e] + b_ref[e] for e in range(ne)]
    gscore = []
    for g in range(N_GROUPS):
        v = sel[g * epg:(g + 1) * epg]
        best = None
        for i in range(epg):
            for j in range(i + 1, epg):
                pair = v[i] + v[j]
                best = pair if best is None else jnp.maximum(best, pair)
        gscore.append(best)
    chosen = []
    for g in range(N_GROUPS):
        ok = None
        for o in range(N_GROUPS):
            if o == g:
                continue
            t = (gscore[g] > gscore[o]) if o < g else (gscore[g] >= gscore[o])
            ok = t if ok is None else jnp.logical_and(ok, t)
        chosen.append(ok)
    mask = []
    for g in range(N_GROUPS):
        v = sel[g * epg:(g + 1) * epg]
        for i in range(epg):
            rank = jnp.zeros_like(v[i])
            for j in range(epg):
                if j == i:
                    continue
                ahead = (v[j] >= v[i]) if j < i else (v[j] > v[i])
                rank = rank + jnp.where(ahead, 1.0, 0.0)
            mask.append(jnp.logical_and(chosen[g], rank < TOP_K))
    wsum = jnp.zeros_like(s[0])
    for e in range(ne):
        wsum = wsum + jnp.where(mask[e], s[e], 0.0)
    for e in range(ne):
        g_ref[e:e + 1, :] = jnp.where(mask[e], s[e] / wsum, 0.0)
        m_ref[e:e + 1, :] = jnp.where(mask[e], 1, 0).astype(jnp.int32)


def router(logits_t, router_b):
    ne, t = logits_t.shape
    tl = _pick(t, (1024, 512, 256, 128))
    spec = pl.BlockSpec((ne, tl), lambda i: (0, i))
    return pl.pallas_call(
        functools.partial(_router_kernel, ne=ne),
        grid=(t // tl,),
        in_specs=[pl.BlockSpec(memory_space=pltpu.SMEM), spec],
        out_specs=[spec, spec],
        out_shape=[jax.ShapeDtypeStruct((ne, t), F32), jax.ShapeDtypeStruct((ne, t), jnp.int32)],
        compiler_params=_cparams("parallel"),
        name="router",
    )(router_b, logits_t)


def dispatch_plan(gates, mask, tm):
    ne, t = mask.shape
    n_tiles = (TOP_K * t) // tm + ne
    rows = n_tiles * tm
    sel = mask > 0
    counts = jnp.sum(mask, axis=1)
    ptiles = (counts + tm - 1) // tm
    tile_end = jnp.cumsum(ptiles)
    off = (tile_end - ptiles) * tm
    dest = off[:, None] + jnp.cumsum(mask, axis=1) - mask
    pos_lo = jnp.min(jnp.where(sel, dest, rows), axis=0)
    pos_hi = jnp.max(jnp.where(sel, dest, -1), axis=0)
    g_lo = jnp.sum(jnp.where(sel & (dest == pos_lo[None, :]), gates, 0.0), axis=0)
    g_hi = jnp.sum(jnp.where(sel & (dest == pos_hi[None, :]), gates, 0.0), axis=0)
    pos = jnp.concatenate([pos_lo, pos_hi]).astype(jnp.int32)
    n_valid = tile_end[-1]
    tiles = jnp.arange(n_tiles, dtype=jnp.int32)
    tid = jnp.minimum(tiles, n_valid - 1)
    tile_expert = jnp.minimum(jnp.sum(tile_end[None, :] <= tid[:, None], axis=1), ne - 1).astype(jnp.int32)
    prev = jnp.concatenate([jnp.full((1,), -1, jnp.int32), tile_expert[:-1]])
    tile_first = (tile_expert != prev).astype(jnp.int32)
    last = jnp.any((tiles[:, None] == tile_end[None, :] - 1) & (ptiles[None, :] > 0), axis=1)
    tile_fill = (last | (tiles >= n_valid)).astype(jnp.int32)
    nv = jnp.reshape(n_valid, (1,)).astype(jnp.int32)
    return tile_fill, jnp.stack([g_lo, g_hi], axis=1), pos, tid, tile_expert, tile_first, nv


def _scatter_kernel(pos_ref, fill_ref, x_ref, o_ref, stage_ref, zero_ref, sems, zsem, *, tile, tm):
    i = pl.program_id(0)
    n = pl.num_programs(0)
    ntok = pos_ref.shape[0] // 2
    slot = i % 2

    def slot_wait(s):
        for _ in range(2):
            pltpu.make_async_copy(stage_ref.at[s], o_ref.at[pl.ds(0, tile)], sems.at[s]).wait()

    @pl.when(i == 0)
    def _():
        zero_ref[...] = jnp.zeros_like(zero_ref)

        def fill(t, wait):
            @pl.when(fill_ref[t] == 1)
            def _():
                cp = pltpu.make_async_copy(zero_ref, o_ref.at[pl.ds(t * tm, tm)], zsem)
                cp.wait() if wait else cp.start()

        lax.fori_loop(0, fill_ref.shape[0], lambda t, c: (fill(t, False), c)[1], 0)
        lax.fori_loop(0, fill_ref.shape[0], lambda t, c: (fill(t, True), c)[1], 0)

    @pl.when(i >= 2)
    def _():
        slot_wait(slot)

    stage_ref[slot] = x_ref[...]

    def start(r, carry):
        for k in range(2):
            dst = pos_ref[k * ntok + i * tile + r]
            pltpu.make_async_copy(stage_ref.at[slot, r], o_ref.at[dst], sems.at[slot]).start(priority=k)
        return carry

    lax.fori_loop(0, tile, start, 0, unroll=4)

    @pl.when(i == n - 1)
    def _():
        slot_wait(slot)

        @pl.when(n >= 2)
        def _():
            slot_wait(1 - slot)


def scatter_rows(h, pos, tile_fill, tm, tile):
    t = h.shape[0]
    rows = tile_fill.shape[0] * tm
    return pl.pallas_call(
        functools.partial(_scatter_kernel, tile=tile, tm=tm),
        grid_spec=pltpu.PrefetchScalarGridSpec(
            num_scalar_prefetch=2,
            grid=(t // tile,),
            in_specs=[pl.BlockSpec((tile,) + h.shape[1:], lambda i, p, f: (i, 0, 0))],
            out_specs=pl.BlockSpec(memory_space=pl.ANY),
            scratch_shapes=[pltpu.VMEM((2, tile) + h.shape[1:], h.dtype), pltpu.VMEM((tm,) + h.shape[1:], h.dtype),
                            pltpu.SemaphoreType.DMA((2,)), pltpu.SemaphoreType.DMA(())],
        ),
        out_shape=jax.ShapeDtypeStruct((rows,) + h.shape[1:], h.dtype),
        compiler_params=_cparams("arbitrary"),
        name="moe_scatter",
    )(pos, tile_fill, h)


def _ffn_a_kernel(tid_ref, te_ref, tf_ref, nv_ref, x_ref, w1_ref, w3_ref, o_ref, w1b, w3b):
    i = pl.program_id(1)

    @pl.when(i < nv_ref[0])
    def _():
        @pl.when(tf_ref[i] == 1)
        def _():
            w1b[...] = w1_ref[0].astype(BF16)
            w3b[...] = w3_ref[0].astype(BF16)

        x = x_ref[...].reshape(x_ref.shape[0], -1)
        a = jnp.dot(x, w1b[...], preferred_element_type=F32)
        b = jnp.dot(x, w3b[...], preferred_element_type=F32)
        o_ref[...] = (a * _sigmoid(a) * b).astype(o_ref.dtype)

    @pl.when(i >= nv_ref[0])
    def _():
        o_ref[...] = jnp.zeros_like(o_ref)


def ffn_stage_a(xs, w1, w3, layer, plan, tm):
    _, _, _, tid, te, tf, nv = plan
    rows = xs.shape[0]
    d = xs.shape[1] * xs.shape[2]
    ff = w1.shape[3]
    tfc = _pick(ff, (512, 256, 128))
    wspec = pl.BlockSpec((None, 1, d, tfc), lambda f, i, tid, te, tf, nv: (layer, te[i], 0, f))
    return pl.pallas_call(
        _ffn_a_kernel,
        grid_spec=pltpu.PrefetchScalarGridSpec(
            num_scalar_prefetch=4,
            grid=(ff // tfc, rows // tm),
            in_specs=[pl.BlockSpec((tm,) + xs.shape[1:], lambda f, i, tid, te, tf, nv: (tid[i], 0, 0)), wspec, wspec],
            out_specs=pl.BlockSpec((tm, tfc), lambda f, i, tid, te, tf, nv: (i, f)),
            scratch_shapes=[pltpu.VMEM((d, tfc), BF16), pltpu.VMEM((d, tfc), BF16)],
        ),
        out_shape=jax.ShapeDtypeStruct((rows, ff), BF16),
        compiler_params=_cparams("arbitrary", "arbitrary"),
        name="moe_ffn_a",
    )(tid, te, tf, nv, xs, w1, w3)


def _ffn_b_kernel(tid_ref, te_ref, tf_ref, nv_ref, h_ref, w2_ref, o_ref, w2b):
    i = pl.program_id(1)

    @pl.when(i < nv_ref[0])
    def _():
        @pl.when(tf_ref[i] == 1)
        def _():
            w2b[...] = w2_ref[0].astype(BF16)

        y = jnp.dot(h_ref[...], w2b[...], preferred_element_type=F32)
        o_ref[...] = y.astype(o_ref.dtype).reshape(o_ref.shape)

    @pl.when(i >= nv_ref[0])
    def _():
        o_ref[...] = jnp.zeros_like(o_ref)


def ffn_stage_b(hs, w2, layer, plan, tm):
    _, _, _, tid, te, tf, nv = plan
    rows, ff = hs.shape
    d = w2.shape[3]
    tn = _pick(d, (2048, 1024, 512, 256, 128))
    return pl.pallas_call(
        _ffn_b_kernel,
        grid_spec=pltpu.PrefetchScalarGridSpec(
            num_scalar_prefetch=4,
            grid=(d // tn, rows // tm),
            in_specs=[
                pl.BlockSpec((tm, ff), lambda n, i, tid, te, tf, nv: (tid[i], 0)),
                pl.BlockSpec((None, 1, ff, tn), lambda n, i, tid, te, tf, nv: (layer, te[i], 0, n)),
            ],
            out_specs=pl.BlockSpec((tm, tn // LANES, LANES), lambda n, i, tid, te, tf, nv: (i, n, 0)),
            scratch_shapes=[pltpu.VMEM((ff, tn), BF16)],
        ),
        out_shape=jax.ShapeDtypeStruct((rows, d // LANES, LANES), BF16),
        compiler_params=_cparams("arbitrary", "arbitrary"),
        name="moe_ffn_b",
    )(tid, te, tf, nv, hs, w2)


def moe_sparse(h, logits_t, router_b, w1, w3, w2, layer, tile):
    gates, mask = router(logits_t, router_b)
    plan = dispatch_plan(gates, mask, MOE_TILE)
    xs = scatter_rows(h, plan[2], plan[0], MOE_TILE, tile)
    hs = ffn_stage_a(xs, w1, w3, layer, plan, MOE_TILE)
    ys = ffn_stage_b(hs, w2, layer, plan, MOE_TILE)
    return ys, plan[2], plan[1]


def rope_tables(n, lc, hd):
    half = hd // 2
    t = jnp.arange(n)
    inv = jnp.power(ROPE_BASE, -jnp.arange(0, half, 2, dtype=F32) / half)
    ang_r = (t // GRID_W).astype(F32)[:, None] * inv
    ang_c = (t % GRID_W).astype(F32)[:, None] * inv
    cos = jnp.concatenate([jnp.cos(ang_r)] * 2 + [jnp.cos(ang_c)] * 2, axis=1)
    sin = jnp.concatenate([-jnp.sin(ang_r), jnp.sin(ang_r), -jnp.sin(ang_c), jnp.sin(ang_c)], axis=1)
    cos = jnp.concatenate([jnp.ones((lc, hd), F32), cos], axis=0)
    sin = jnp.concatenate([jnp.zeros((lc, hd), F32), sin], axis=0)
    scale = hd ** -0.5
    ones, zeros = jnp.ones_like(cos), jnp.zeros_like(sin)
    return jnp.stack([cos * scale, cos, ones]), jnp.stack([sin * scale, sin, zeros])


def kernel(x, c, ctx, c_ctx, ada_w, ada_b, ln_g, ln_b, mlstm_w_in, mlstm_b_gates, mlstm_norm_w, mlstm_w_out,
           attn_w_in, attn_b_in, attn_sink, attn_w_out, router_w, router_b, moe_w1, moe_w3, moe_w2):
    bsz, n, d = x.shape
    lc = ctx.shape[1]
    depth = ada_w.shape[0]
    assert depth == 2 and bsz <= 4 and n % lc == 0 and lc % ML_CHUNK == 0
    alpha = (2 * depth) ** 0.25
    tile = lc
    per = (n + lc) // tile
    per_lat = n // tile

    cc = jnp.concatenate([c, c_ctx[None, :], jnp.zeros((8 - bsz - 1, d), F32)], axis=0)
    ada0 = adaln(cc, ada_w, ada_b, 0).reshape(8, 1, 6 * d)
    rwt = router_w.T

    heads = mlstm_b_gates.shape[1] // 4
    dv = d // heads
    dqk = dv // 2
    nmain = 2 * heads * (dqk + dv)
    w_in_t = jnp.swapaxes(mlstm_w_in, 1, 2)
    h0, gates_pre = mod0(x, ctx, ada0, w_in_t[0, nmain:, :], mlstm_b_gates[0][None, :])
    qkvo = matmul(h0, w_in_t, n_cols=nmain, w_t=True)
    ncc, nct = lc // ML_CHUNK, (n + lc) // ML_CHUNK
    hg, ada1 = mlstm_bidir(qkvo, gates_pre, mlstm_norm_w[0], bsz, heads, dqk, dv, ncc, nct, cc, ada_w, ada_b, 1)
    ada1 = ada1.reshape(8, 1, 6 * d)
    y = matmul(hg, mlstm_w_out, out_dtype=F32)
    x1, hffn, lg = ln_block(x, y, ada0, 2, ln_g[0, 0], ln_b[0, 0], alpha=alpha, nb=bsz, per=per, tile=tile,
                            ctx_rows=True, ctx=ctx, nxt=(ada0, 3, 4), h_dtype=BF16, h_rows3d=True, router_wt=rwt)
    ys, pos, pos_w = moe_sparse(hffn, lg, router_b, moe_w1, moe_w3, moe_w2, 0, tile)
    x2, h1 = ln_block(x1, ys, ada0, 5, ln_g[0, 1], ln_b[0, 1], alpha=alpha, nb=bsz, per=per, tile=tile,
                      ctx_rows=True, x_per=per, nxt=(ada1, 0, 1), h_dtype=BF16, pos=pos, pos_w=pos_w)

    heads_a = attn_sink.shape[1]
    hd = d // heads_a
    kvh = (attn_w_in.shape[2] // hd - heads_a) // 2
    cos_t, sin_t = rope_tables(n, lc, hd)
    q_end, k_end = heads_a * hd, (heads_a + kvh) * hd
    type_of = lambda col: jnp.where(col < q_end, 0, jnp.where(col < k_end, 1, 2))
    tm1 = _pick(n + lc, (1152, 768, 384, 256, 128))
    qkv = matmul(h1, attn_w_in, bias=attn_b_in[0], rope=(cos_t, sin_t, n + lc, type_of), tm=tm1,
                 tn=_pick(kvh * hd, (512, 256, 128)))
    att = window_attention(qkv.reshape(bsz, n + lc, -1), attn_sink[0], lc, n, heads_a, kvh, hd)
    y = matmul(att.reshape(bsz * n, heads_a * hd), attn_w_out, out_dtype=F32)
    x3, hffn, lg = ln_block(x2, y, ada1, 2, ln_g[1, 0], ln_b[1, 0], alpha=alpha, nb=bsz, per=per_lat, tile=tile,
                            ctx_rows=False, x_per=per, x_off=1, nxt=(ada1, 3, 4), h_dtype=BF16, h_rows3d=True,
                            router_wt=rwt)
    ys, pos, pos_w = moe_sparse(hffn, lg, router_b, moe_w1, moe_w3, moe_w2, 1, tile)
    (x4,) = ln_block(x3, ys, ada1, 5, ln_g[1, 1], ln_b[1, 1], alpha=alpha, nb=bsz, per=per_lat, tile=tile,
                     ctx_rows=False, x_per=per_lat, pos=pos, pos_w=pos_w)
    return x4.reshape(bsz, n, d)
```

```python
import functools

import jax
import jax.numpy as jnp
from jax import lax
from jax.experimental import pallas as pl
from jax.experimental.pallas import tpu as pltpu

F32 = jnp.float32
BF16 = jnp.bfloat16
NEG_INF = float("-inf")

GRID_W = 64
ML_CHUNK = 128
AT_WINDOW = 128
AT_BLOCK = 128
ROPE_BASE = 10000.0
N_GROUPS = 4
TOP_K = 2
LN_EPS = 1e-5
RMS_EPS = 1e-6

LANES = 128
VMEM_LIMIT = 56 * 1024 * 1024
MOE_TILE = 512
HIGHEST = lax.Precision.HIGHEST
NT_DIMS = (((1,), (1,)), ((), ()))


def _cparams(*sem):
    return pltpu.CompilerParams(dimension_semantics=sem, vmem_limit_bytes=VMEM_LIMIT)


def _pick(n, candidates):
    for c in candidates:
        if n % c == 0:
            return c
    return n


def _sigmoid(x):
    return 1.0 / (1.0 + jnp.exp(-x))


def _log_sigmoid(x):
    return jnp.minimum(x, 0.0) - jnp.log1p(jnp.exp(-jnp.abs(x)))


def _adaln_block(c_ref, w_ref, b_ref):
    c = c_ref[...]
    a = (c * _sigmoid(c)).astype(BF16)
    return jnp.dot(a, w_ref[...].astype(BF16), preferred_element_type=F32) + b_ref[...]


def _adaln_kernel(c_ref, w_ref, b_ref, o_ref):
    o_ref[...] = _adaln_block(c_ref, w_ref, b_ref)


def adaln(cc, ada_w, ada_b, layer):
    depth, d, n6 = ada_w.shape
    tn = _pick(n6, (512, 256, 128))
    return pl.pallas_call(
        _adaln_kernel,
        grid=(n6 // tn,),
        in_specs=[
            pl.BlockSpec((8, d), lambda j: (0, 0)),
            pl.BlockSpec((None, d, tn), lambda j: (layer, 0, j)),
            pl.BlockSpec((None, 1, tn), lambda j: (layer, 0, j)),
        ],
        out_specs=pl.BlockSpec((8, tn), lambda j: (0, j)),
        out_shape=jax.ShapeDtypeStruct((8, n6), F32),
        compiler_params=_cparams("parallel"),
        name="adaln",
    )(cc, ada_w, ada_b.reshape(depth, 1, n6))


def _mod0_kernel(x_ref, c_ref, sh_ref, sc_ref, wg_ref, bg_ref, h_ref, g_ref):
    j = pl.program_id(1)
    t = jnp.where(j == 0, c_ref[0], x_ref[0])
    h = t * (1.0 + sc_ref[0]) + sh_ref[0]
    h_ref[...] = h.astype(BF16)
    g_ref[...] = lax.dot_general(h.astype(BF16), wg_ref[...].astype(BF16), NT_DIMS,
                                 preferred_element_type=F32) + bg_ref[...]


def mod0(x, ctx, ada3, wg, bg):
    bsz, n, d = x.shape
    lc = ctx.shape[1]
    per = (n + lc) // lc
    ng = wg.shape[0]
    prow = lambda b, j: jnp.where(j == 0, bsz, b)
    return pl.pallas_call(
        _mod0_kernel,
        grid=(bsz, per),
        in_specs=[
            pl.BlockSpec((1, lc, d), lambda b, j: (b, jnp.maximum(j - 1, 0), 0)),
            pl.BlockSpec((1, lc, d), lambda b, j: (b, 0, 0)),
            pl.BlockSpec((1, 1, d), lambda b, j: (prow(b, j), 0, 0)),
            pl.BlockSpec((1, 1, d), lambda b, j: (prow(b, j), 0, 1)),
            pl.BlockSpec((ng, d), lambda b, j: (0, 0)),
            pl.BlockSpec((1, ng), lambda b, j: (0, 0)),
        ],
        out_specs=[
            pl.BlockSpec((lc, d), lambda b, j: (b * per + j, 0)),
            pl.BlockSpec((lc, ng), lambda b, j: (b * per + j, 0)),
        ],
        out_shape=[
            jax.ShapeDtypeStruct((bsz * (n + lc), d), BF16),
            jax.ShapeDtypeStruct((bsz * (n + lc), ng), F32),
        ],
        compiler_params=_cparams("parallel", "arbitrary"),
        name="mod0",
    )(x, ctx, ada3, ada3, wg, bg)


def _mm_kernel(*refs, has_bias, has_rope, tn, w_t):
    x_ref, w_ref = refs[0], refs[1]
    pos = 2
    if w_t:
        acc = lax.dot_general(x_ref[...], w_ref[...].astype(BF16), NT_DIMS, preferred_element_type=F32)
    else:
        acc = jnp.dot(x_ref[...], w_ref[...].astype(BF16), preferred_element_type=F32)
    if has_bias:
        acc = acc + refs[pos][...]
        pos += 1
    o_ref = refs[-1]
    if has_rope:
        ca = refs[pos][0]
        sb = refs[pos + 1][0]
        lane = lax.broadcasted_iota(jnp.int32, ca.shape, 1)
        first = (lane % 64) < 32
        for h in range(tn // LANES):
            y = acc[:, h * LANES:(h + 1) * LANES]
            sw = jnp.where(first, pltpu.roll(y, 96, 1), pltpu.roll(y, 32, 1))
            o_ref[:, h * LANES:(h + 1) * LANES] = (y * ca + sw * sb).astype(o_ref.dtype)
    else:
        o_ref[...] = acc.astype(o_ref.dtype)


def matmul(x, w, bias=None, rope=None, out_dtype=BF16, tm=None, tn=None, n_cols=None, w_t=False):
    m, k = x.shape
    n = n_cols or w.shape[1 if w_t else 2]
    tm = tm or _pick(m, (1152, 1024, 768, 512, 256, 128))
    tn = tn or _pick(n, (512, 256, 128))
    in_specs = [
        pl.BlockSpec((tm, k), lambda i, j: (i, 0)),
        pl.BlockSpec((None, tn, k), lambda i, j: (0, j, 0)) if w_t else
        pl.BlockSpec((None, k, tn), lambda i, j: (0, 0, j)),
    ]
    args = [x, w]
    if bias is not None:
        in_specs.append(pl.BlockSpec((1, tn), lambda i, j: (0, j)))
        args.append(bias.reshape(1, n))
    if rope is not None:
        ca, sb, period, type_of = rope
        nper = period // tm
        spec = pl.BlockSpec((1, tm, LANES), lambda i, j: (type_of(j * tn), i % nper, 0))
        in_specs += [spec, spec]
        args += [ca, sb]
    return pl.pallas_call(
        functools.partial(_mm_kernel, has_bias=bias is not None, has_rope=rope is not None, tn=tn, w_t=w_t),
        grid=(m // tm, n // tn),
        in_specs=in_specs,
        out_specs=pl.BlockSpec((tm, tn), lambda i, j: (i, j)),
        out_shape=jax.ShapeDtypeStruct((m, n), out_dtype),
        compiler_params=_cparams("parallel", "arbitrary"),
        name="matmul",
    )(*args)


def _mlstm_chunks(chains, scale):
    L = ML_CHUNK
    r = lax.broadcasted_iota(jnp.int32, (L, L), 0)
    c = lax.broadcasted_iota(jnp.int32, (L, L), 1)
    gate = []
    for q, k, v, ig_row, f_row, ig_col, f_col, st, n_row, m_prev, d in chains:
        reverse = d == 1
        lf_row = _log_sigmoid(f_row)
        lf_col = _log_sigmoid(f_col)
        keep = (c >= r) if reverse else (c <= r)
        keep_t = (r >= c) if reverse else (r <= c)
        b_col = jnp.sum(jnp.where(keep, lf_row, 0.0), axis=1, keepdims=True)
        b_row = jnp.sum(jnp.where(keep_t, lf_col, 0.0), axis=0, keepdims=True)
        logw = jnp.where(keep, b_col - b_row + ig_row, NEG_INF)
        inter = b_col + m_prev
        m_t = jnp.maximum(inter, jnp.max(logw, axis=1, keepdims=True))
        w_inter = jnp.exp(inter - m_t)
        p = jnp.exp(logw - m_t)
        b_end = b_col[0:1, :] if reverse else b_col[L - 1:L, :]
        m_new = jnp.maximum(b_end + m_prev, jnp.max(b_end - b_row + ig_row, axis=1, keepdims=True))
        decay = jnp.exp(b_end + m_prev - m_new)
        w_src = jnp.exp(b_end - b_col + ig_col - m_new)
        gate.append((m_t, w_inter, p, m_new, decay, w_src))

    qk = [lax.dot_general(ch[0], ch[1], NT_DIMS, preferred_element_type=F32) for ch in chains]
    qs = [jnp.dot(ch[0], ch[7].astype(BF16), preferred_element_type=F32) for ch in chains]
    s = [a * (g[2] * scale) for a, g in zip(qk, gate)]
    sv = [jnp.dot(a.astype(BF16), ch[2], preferred_element_type=F32) for a, ch in zip(s, chains)]
    kw = [ch[1].astype(F32) * g[5] for ch, g in zip(chains, gate)]
    kv = [jnp.dot(a.T.astype(BF16), ch[2], preferred_element_type=F32) for a, ch in zip(kw, chains)]

    out = []
    for ch, g, s_i, qs_i, sv_i, kw_i, kv_i in zip(chains, gate, s, qs, sv, kw, kv):
        q, n_row, st = ch[0], ch[8], ch[7]
        m_t, w_inter, _, m_new, decay, _ = g
        num = (w_inter * scale) * qs_i + sv_i
        qn = jnp.sum(q.astype(F32) * n_row, axis=1, keepdims=True) * scale
        den = w_inter * qn + jnp.sum(s_i, axis=1, keepdims=True)
        h = num * (1.0 / jnp.maximum(jnp.abs(den), jnp.exp(-m_t)))
        out.append((h, decay * st + kv_i, decay * n_row + jnp.sum(kw_i, axis=0, keepdims=True), m_new))
    return out


def _mlstm_kernel(q_ref, k_ref, v_ref, o_ref, gr_ref, gc_ref, nw_ref, cc_ref, aw_ref, ab_ref, out_ref, ada_ref,
                  acc_ref, st_ref, n_ref, m_ref, *, scale, ncc, nct, hpb):
    L = ML_CHUNK
    ada_ref[...] = _adaln_block(cc_ref, aw_ref, ab_ref)
    st_ref[...] = jnp.zeros_like(st_ref)
    n_ref[...] = jnp.zeros_like(n_ref)
    m_ref[...] = jnp.zeros_like(m_ref)
    acc_ref[...] = jnp.zeros_like(acc_ref)

    dqk, dv = st_ref.shape[1], st_ref.shape[2]

    def step(j, carry):
        where, chains = [], []
        for hh in range(hpb):
            for d, cj in ((0, j), (1, jnp.where(j < ncc, ncc - 1 - j, nct - 1 - (j - ncc)))):
                rows = pl.ds(pl.multiple_of(cj * L, L), L)
                slot = 2 * hh + d
                gcol = gc_ref[0, hh, cj]
                where.append((rows, hh, slot, acc_ref[rows, hh * dv:(hh + 1) * dv]))
                chains.append((q_ref[rows, hh * dqk:(hh + 1) * dqk], k_ref[rows, hh * dqk:(hh + 1) * dqk],
                               v_ref[rows, hh * dv:(hh + 1) * dv],
                               gr_ref[0, 2 * d, hh, cj], gr_ref[0, 2 * d + 1, hh, cj],
                               gcol[:, 2 * d:2 * d + 1], gcol[:, 2 * d + 1:2 * d + 2],
                               st_ref[slot], n_ref[slot], m_ref[slot], d))
        for (rows, hh, slot, acc), (h, st_new, n_new, m_new) in zip(where, _mlstm_chunks(chains, scale)):
            acc_ref[rows, hh * dv:(hh + 1) * dv] = acc + h
            st_ref[slot] = st_new
            n_ref[slot] = n_new
            m_ref[slot] = m_new
        return carry

    lax.fori_loop(0, nct, step, 0, unroll=2)

    def finish(cj, carry):
        rows = pl.ds(pl.multiple_of(cj * L, L), L)
        for hh in range(hpb):
            cols = slice(hh * dv, (hh + 1) * dv)
            hs = acc_ref[rows, cols]
            hn = hs * lax.rsqrt(jnp.mean(hs * hs, axis=1, keepdims=True) + RMS_EPS)
            out_ref[rows, cols] = (hn * nw_ref[:, cols]
                                   * _sigmoid(o_ref[rows, cols].astype(F32))).astype(out_ref.dtype)
        return carry

    lax.fori_loop(0, nct, finish, 0)


def mlstm_bidir(qkvo, gates_pre, norm_w, bsz, heads, dqk, dv, ncc, nct, cc, ada_w, ada_b, ada_layer):
    t = qkvo.shape[0]
    L = ML_CHUNK
    n2 = nct * L
    hpb = 1
    depth, d, n6 = ada_w.shape
    nsteps = bsz * heads // hpb
    ta = n6 // nsteps
    assert n6 % nsteps == 0 and ta % LANES == 0
    assert ncc % 2 == 0 and (nct - ncc) % 2 == 0
    g4 = gates_pre.reshape(bsz, nct, L, 4, heads)
    g_rows = jnp.transpose(g4, (0, 3, 4, 1, 2)).reshape(bsz, 4, heads, nct, 1, L)
    g_cols = jnp.transpose(g4, (0, 4, 1, 2, 3))
    koff = heads // hpb
    voff = 2 * heads * dqk // (dv * hpb)
    ooff = voff + heads // hpb
    return pl.pallas_call(
        functools.partial(_mlstm_kernel, scale=dqk ** -0.5, ncc=ncc, nct=nct, hpb=hpb),
        grid=(bsz, heads // hpb),
        in_specs=[
            pl.BlockSpec((n2, hpb * dqk), lambda b, h: (b, h)),
            pl.BlockSpec((n2, hpb * dqk), lambda b, h: (b, koff + h)),
            pl.BlockSpec((n2, hpb * dv), lambda b, h: (b, voff + h)),
            pl.BlockSpec((n2, hpb * dv), lambda b, h: (b, ooff + h)),
            pl.BlockSpec((1, 4, hpb, nct, 1, L), lambda b, h: (b, 0, h, 0, 0, 0)),
            pl.BlockSpec((1, hpb, nct, L, 4), lambda b, h: (b, h, 0, 0, 0)),
            pl.BlockSpec((1, hpb * dv), lambda b, h: (0, h)),
            pl.BlockSpec((8, d), lambda b, h: (0, 0)),
            pl.BlockSpec((None, d, ta), lambda b, h: (ada_layer, 0, b * (heads // hpb) + h)),
            pl.BlockSpec((None, 1, ta), lambda b, h: (ada_layer, 0, b * (heads // hpb) + h)),
        ],
        out_specs=[pl.BlockSpec((n2, hpb * dv), lambda b, h: (b, h)),
                   pl.BlockSpec((8, ta), lambda b, h: (0, b * (heads // hpb) + h))],
        out_shape=[jax.ShapeDtypeStruct((t, heads * dv), BF16), jax.ShapeDtypeStruct((8, n6), F32)],
        scratch_shapes=[pltpu.VMEM((n2, hpb * dv), F32), pltpu.VMEM((2 * hpb, dqk, dv), F32),
                        pltpu.VMEM((2 * hpb, 1, dqk), F32), pltpu.VMEM((2 * hpb, 1, 1), F32)],
        compiler_params=_cparams("parallel", "parallel"),
        name="mlstm",
    )(qkvo, qkvo, qkvo, qkvo, g_rows, g_cols, norm_w.reshape(1, heads * dv), cc, ada_w,
      ada_b.reshape(depth, 1, n6))


def _attn_kernel(sink_ref, q_ref, k_ref, v_ref, o_ref, bias_ref, *, lc, nb, group, hd):
    kv = pl.program_id(1)
    blk = AT_BLOCK
    rows = group * blk
    kc = k_ref[0, 0:lc, :]
    vc = v_ref[0, 0:lc, :]
    rid = lax.broadcasted_iota(jnp.int32, (rows, 1), 0)
    sink_col = jnp.zeros((rows, 1), F32)
    for g in range(group):
        sink_col = jnp.where(rid // blk == g, sink_ref[kv * group + g], sink_col)
    qrow = lax.broadcasted_iota(jnp.int32, (rows, 3 * blk), 0) % blk
    kcol = lax.broadcasted_iota(jnp.int32, (rows, 3 * blk), 1)
    for back in range(3):
        rel = kcol - back * blk - qrow
        bias_ref[back] = jnp.where(jnp.abs(rel) <= AT_WINDOW, 0.0, NEG_INF)

    def body(j, carry):
        ws = jnp.clip(j - 1, 0, nb - 3)
        qs = pl.multiple_of(lc + j * blk, blk)
        ks = pl.multiple_of(lc + ws * blk, blk)
        q4 = q_ref[0, pl.ds(qs, blk), :]
        qq = jnp.concatenate([q4[:, g * hd:(g + 1) * hd] for g in range(group)], axis=0)
        kw = k_ref[0, pl.ds(ks, 3 * blk), :]
        vw = v_ref[0, pl.ds(ks, 3 * blk), :]
        s_loc = lax.dot_general(qq, kw, NT_DIMS, preferred_element_type=F32)
        s_ctx = lax.dot_general(qq, kc, NT_DIMS, preferred_element_type=F32)
        s_loc = s_loc + bias_ref[j - ws]
        m = jnp.maximum(jnp.maximum(jnp.max(s_loc, axis=1, keepdims=True),
                                    jnp.max(s_ctx, axis=1, keepdims=True)), sink_col)
        p_loc = jnp.exp(s_loc - m)
        p_ctx = jnp.exp(s_ctx - m)
        den = (jnp.sum(p_loc, axis=1, keepdims=True) + jnp.sum(p_ctx, axis=1, keepdims=True)
               + jnp.exp(sink_col - m))
        o = (jnp.dot(p_loc.astype(BF16), vw, preferred_element_type=F32)
             + jnp.dot(p_ctx.astype(BF16), vc, preferred_element_type=F32)) / den
        os = pl.multiple_of(j * blk, blk)
        for g in range(group):
            o_ref[0, pl.ds(os, blk), g * hd:(g + 1) * hd] = o[g * blk:(g + 1) * blk].astype(o_ref.dtype)
        return carry

    lax.fori_loop(0, nb, body, 0, unroll=4 if nb % 4 == 0 else 2)


def window_attention(qkv3, sink, lc, n, heads, kv_heads, hd):
    bsz = qkv3.shape[0]
    group = heads // kv_heads
    nb = n // AT_BLOCK
    assert nb >= 3
    koff = heads
    voff = heads + kv_heads
    return pl.pallas_call(
        functools.partial(_attn_kernel, lc=lc, nb=nb, group=group, hd=hd),
        grid=(bsz, kv_heads),
        in_specs=[
            pl.BlockSpec(memory_space=pltpu.SMEM),
            pl.BlockSpec((1, lc + n, group * hd), lambda b, g: (b, 0, g)),
            pl.BlockSpec((1, lc + n, hd), lambda b, g: (b, 0, koff + g)),
            pl.BlockSpec((1, lc + n, hd), lambda b, g: (b, 0, voff + g)),
        ],
        out_specs=pl.BlockSpec((1, n, group * hd), lambda b, g: (b, 0, g)),
        out_shape=jax.ShapeDtypeStruct((bsz, n, heads * hd), BF16),
        scratch_shapes=[pltpu.VMEM((3, group * AT_BLOCK, 3 * AT_BLOCK), F32)],
        compiler_params=_cparams("parallel", "parallel"),
        name="window_attention",
    )(sink, qkv3, qkv3, qkv3)


def _ln_kernel(*refs, dual, gather, tile, has_h, has_router, alpha):
    refs = list(refs)
    if gather:
        pos_ref = refs.pop(0)
    x_ref = refs.pop(0)
    c_ref = refs.pop(0) if dual else None
    y_ref = refs.pop(0)
    gw_ref = refs.pop(0) if gather else None
    gate_ref, lng_ref, lnb_ref = refs.pop(0), refs.pop(0), refs.pop(0)
    if has_h:
        sh_ref, sc_ref = refs.pop(0), refs.pop(0)
    if has_router:
        rw_ref = refs.pop(0)
    xo_ref = refs.pop(0)
    if has_h:
        ho_ref = refs.pop(0)
    if has_router:
        lg_ref = refs.pop(0)

    if gather:
        buf_ref, sems = refs.pop(0), refs.pop(0)
        lin = pl.program_id(0) * pl.num_programs(1) + pl.program_id(1)
        ntiles = pl.num_programs(0) * pl.num_programs(1)
        ntok = pos_ref.shape[0] // 2

        def issue(t):
            slot = t % 2

            def start(r, carry):
                for k in range(2):
                    src = pos_ref[k * ntok + t * tile + r]
                    pltpu.make_async_copy(y_ref.at[src], buf_ref.at[slot, k, r], sems.at[slot]).start(priority=k)
                return carry

            lax.fori_loop(0, tile, start, 0, unroll=4)

        @pl.when(lin == 0)
        def _():
            issue(lin)

        @pl.when(lin + 1 < ntiles)
        def _():
            issue(lin + 1)

        slot = lin % 2
        for k in range(2):
            pltpu.make_async_copy(y_ref.at[pl.ds(0, tile)], buf_ref.at[slot, k], sems.at[slot]).wait()
        gw = gw_ref[...]
        d = x_ref.shape[-1]
        y = (gw[:, 0:1] * buf_ref[slot, 0].reshape(tile, d).astype(F32)
             + gw[:, 1:2] * buf_ref[slot, 1].reshape(tile, d).astype(F32))
    else:
        y = y_ref[...].astype(F32)

    if dual:
        x = jnp.where(pl.program_id(1) == 0, c_ref[0], x_ref[0])
    else:
        x = x_ref[...]
    z = alpha * x + gate_ref[0] * y
    mu = jnp.mean(z, axis=1, keepdims=True)
    zc = z - mu
    var = jnp.mean(zc * zc, axis=1, keepdims=True)
    xn = zc * lax.rsqrt(var + LN_EPS) * lng_ref[...] + lnb_ref[...]
    xo_ref[...] = xn
    if has_h:
        h = xn * (1.0 + sc_ref[0]) + sh_ref[0]
        ho_ref[...] = h.reshape(ho_ref.shape).astype(ho_ref.dtype)
        if has_router:
            lg_ref[...] = lax.dot_general(rw_ref[...], h, NT_DIMS, preferred_element_type=F32,
                                          precision=HIGHEST)


def ln_block(x, y, ada3, gate_col, lng, lnb, *, alpha, nb, per, tile, ctx_rows, ctx=None, x_per=None,
             x_off=0, nxt=None, h_dtype=None, h_rows3d=False, router_wt=None, pos=None, pos_w=None):
    d = x.shape[-1]
    dual = ctx is not None
    gather = pos is not None
    rows = nb * per * tile
    prow = (lambda b, j: jnp.where(j == 0, nb, b)) if ctx_rows else (lambda b, j: b)

    def im(f):
        return lambda b, j, *_: f(b, j)

    in_specs, args = [], []
    if dual:
        in_specs.append(pl.BlockSpec((1, tile, d), im(lambda b, j: (b, jnp.maximum(j - 1, 0), 0))))
        in_specs.append(pl.BlockSpec((1, tile, d), im(lambda b, j: (b, 0, 0))))
        args += [x, ctx]
    else:
        in_specs.append(pl.BlockSpec((tile, d), im(lambda b, j: (b * x_per + x_off + j, 0))))
        args.append(x)
    if gather:
        in_specs.append(pl.BlockSpec(memory_space=pl.ANY))
        in_specs.append(pl.BlockSpec((tile, 2), im(lambda b, j: (b * per + j, 0))))
        args += [y, pos_w]
    else:
        in_specs.append(pl.BlockSpec((tile, d), im(lambda b, j: (b * per + j, 0))))
        args.append(y)
    in_specs.append(pl.BlockSpec((1, 1, d), im(lambda b, j: (prow(b, j), 0, gate_col))))
    in_specs.append(pl.BlockSpec((1, d), im(lambda b, j: (0, 0))))
    in_specs.append(pl.BlockSpec((1, d), im(lambda b, j: (0, 0))))
    args += [ada3, lng.reshape(1, d), lnb.reshape(1, d)]
    if nxt is not None:
        ada_n, shc, scc = nxt
        in_specs.append(pl.BlockSpec((1, 1, d), im(lambda b, j: (prow(b, j), 0, shc))))
        in_specs.append(pl.BlockSpec((1, 1, d), im(lambda b, j: (prow(b, j), 0, scc))))
        args += [ada_n, ada_n]
    if router_wt is not None:
        ne = router_wt.shape[0]
        in_specs.append(pl.BlockSpec((ne, d), im(lambda b, j: (0, 0))))
        args.append(router_wt)

    out_specs = [pl.BlockSpec((tile, d), im(lambda b, j: (b * per + j, 0)))]
    out_shape = [jax.ShapeDtypeStruct((rows, d), F32)]
    if nxt is not None and h_rows3d:
        out_specs.append(pl.BlockSpec((tile, d // LANES, LANES), im(lambda b, j: (b * per + j, 0, 0))))
        out_shape.append(jax.ShapeDtypeStruct((rows, d // LANES, LANES), h_dtype))
    elif nxt is not None:
        out_specs.append(pl.BlockSpec((tile, d), im(lambda b, j: (b * per + j, 0))))
        out_shape.append(jax.ShapeDtypeStruct((rows, d), h_dtype))
    if router_wt is not None:
        out_specs.append(pl.BlockSpec((ne, tile), im(lambda b, j: (0, b * per + j))))
        out_shape.append(jax.ShapeDtypeStruct((ne, rows), F32))

    scratch = [pltpu.VMEM((2, 2, tile) + y.shape[1:], y.dtype), pltpu.SemaphoreType.DMA((2,))] if gather else []
    kern = functools.partial(_ln_kernel, dual=dual, gather=gather, tile=tile, has_h=nxt is not None,
                             has_router=router_wt is not None, alpha=alpha)
    call = pl.pallas_call(
        kern,
        grid_spec=pltpu.PrefetchScalarGridSpec(
            num_scalar_prefetch=1 if gather else 0,
            grid=(nb, per),
            in_specs=in_specs,
            out_specs=out_specs,
            scratch_shapes=scratch,
        ),
        out_shape=out_shape,
        compiler_params=_cparams("arbitrary", "arbitrary"),
        name="ln_gather" if gather else "ln_block",
    )
    return call(*(([pos] if gather else []) + args))


def _router_kernel(b_ref, l_ref, g_ref, m_ref, *, ne):
    epg = ne // N_GROUPS
    logits = l_ref[...]
    s = [_sigmoid(logits[e:e + 1, :]) for e in range(ne)]
    sel = [s[e] + b_ref[e] for e in range(ne)]
    gscore = []
    for g in range(N_GROUPS):
        v = sel[g * epg:(g + 1) * epg]
        best = None
        for i in range(epg):
            for j in range(i + 1, epg):
                pair = v[i] + v[j]
                best = pair if best is None else jnp.maximum(best, pair)
        gscore.append(best)
    chosen = []
    for g in range(N_GROUPS):
        ok = None
        for o in range(N_GROUPS):
            if o == g:
                continue
            t = (gscore[g] > gscore[o]) if o < g else (gscore[g] >= gscore[o])
            ok = t if ok is None else jnp.logical_and(ok, t)
        chosen.append(ok)
    mask = []
    for g in range(N_GROUPS):
        v = sel[g * epg:(g + 1) * epg]
        for i in range(epg):
            rank = jnp.zeros_like(v[i])
            for j in range(epg):
                if j == i:
                    continue
                ahead = (v[j] >= v[i]) if j < i else (v[j] > v[i])
                rank = rank + jnp.where(ahead, 1.0, 0.0)
            mask.append(jnp.logical_and(chosen[g], rank < TOP_K))
    wsum = jnp.zeros_like(s[0])
    for e in range(ne):
        wsum = wsum + jnp.where(mask[e], s[e], 0.0)
    for e in range(ne):
        g_ref[e:e + 1, :] = jnp.where(mask[e], s[e] / wsum, 0.0)
        m_ref[e:e + 1, :] = jnp.where(mask[e], 1, 0).astype(jnp.int32)


def router(logits_t, router_b):
    ne, t = logits_t.shape
    tl = _pick(t, (1024, 512, 256, 128))
    spec = pl.BlockSpec((ne, tl), lambda i: (0, i))
    return pl.pallas_call(
        functools.partial(_router_kernel, ne=ne),
        grid=(t // tl,),
        in_specs=[pl.BlockSpec(memory_space=pltpu.SMEM), spec],
        out_specs=[spec, spec],
        out_shape=[jax.ShapeDtypeStruct((ne, t), F32), jax.ShapeDtypeStruct((ne, t), jnp.int32)],
        compiler_params=_cparams("parallel"),
        name="router",
    )(router_b, logits_t)


def dispatch_plan(gates, mask, tm):
    ne, t = mask.shape
    n_tiles = (TOP_K * t) // tm + ne
    rows = n_tiles * tm
    sel = mask > 0
    counts = jnp.sum(mask, axis=1)
    ptiles = (counts + tm - 1) // tm
    tile_end = jnp.cumsum(ptiles)
    off = (tile_end - ptiles) * tm
    dest = off[:, None] + jnp.cumsum(mask, axis=1) - mask
    pos_lo = jnp.min(jnp.where(sel, dest, rows), axis=0)
    pos_hi = jnp.max(jnp.where(sel, dest, -1), axis=0)
    g_lo = jnp.sum(jnp.where(sel & (dest == pos_lo[None, :]), gates, 0.0), axis=0)
    g_hi = jnp.sum(jnp.where(sel & (dest == pos_hi[None, :]), gates, 0.0), axis=0)
    pos = jnp.concatenate([pos_lo, pos_hi]).astype(jnp.int32)
    n_valid = tile_end[-1]
    tiles = jnp.arange(n_tiles, dtype=jnp.int32)
    tid = jnp.minimum(tiles, n_valid - 1)
    tile_expert = jnp.minimum(jnp.sum(tile_end[None, :] <= tid[:, None], axis=1), ne - 1).astype(jnp.int32)
    prev = jnp.concatenate([jnp.full((1,), -1, jnp.int32), tile_expert[:-1]])
    tile_first = (tile_expert != prev).astype(jnp.int32)
    last = jnp.any((tiles[:, None] == tile_end[None, :] - 1) & (ptiles[None, :] > 0), axis=1)
    tile_fill = (last | (tiles >= n_valid)).astype(jnp.int32)
    nv = jnp.reshape(n_valid, (1,)).astype(jnp.int32)
    return tile_fill, jnp.stack([g_lo, g_hi], axis=1), pos, tid, tile_expert, tile_first, nv


def _scatter_kernel(pos_ref, fill_ref, x_ref, o_ref, stage_ref, zero_ref, sems, zsem, *, tile, tm):
    i = pl.program_id(0)
    n = pl.num_programs(0)
    ntok = pos_ref.shape[0] // 2
    slot = i % 2

    def slot_wait(s):
        for _ in range(2):
            pltpu.make_async_copy(stage_ref.at[s], o_ref.at[pl.ds(0, tile)], sems.at[s]).wait()

    @pl.when(i == 0)
    def _():
        zero_ref[...] = jnp.zeros_like(zero_ref)

        def fill(t, wait):
            @pl.when(fill_ref[t] == 1)
            def _():
                cp = pltpu.make_async_copy(zero_ref, o_ref.at[pl.ds(t * tm, tm)], zsem)
                cp.wait() if wait else cp.start()

        lax.fori_loop(0, fill_ref.shape[0], lambda t, c: (fill(t, False), c)[1], 0)
        lax.fori_loop(0, fill_ref.shape[0], lambda t, c: (fill(t, True), c)[1], 0)

    @pl.when(i >= 2)
    def _():
        slot_wait(slot)

    stage_ref[slot] = x_ref[...]

    def start(r, carry):
        for k in range(2):
            dst = pos_ref[k * ntok + i * tile + r]
            pltpu.make_async_copy(stage_ref.at[slot, r], o_ref.at[dst], sems.at[slot]).start(priority=k)
        return carry

    lax.fori_loop(0, tile, start, 0, unroll=4)

    @pl.when(i == n - 1)
    def _():
        slot_wait(slot)

        @pl.when(n >= 2)
        def _():
            slot_wait(1 - slot)


def scatter_rows(h, pos, tile_fill, tm, tile):
    t = h.shape[0]
    rows = tile_fill.shape[0] * tm
    return pl.pallas_call(
        functools.partial(_scatter_kernel, tile=tile, tm=tm),
        grid_spec=pltpu.PrefetchScalarGridSpec(
            num_scalar_prefetch=2,
            grid=(t // tile,),
            in_specs=[pl.BlockSpec((tile,) + h.shape[1:], lambda i, p, f: (i, 0, 0))],
            out_specs=pl.BlockSpec(memory_space=pl.ANY),
            scratch_shapes=[pltpu.VMEM((2, tile) + h.shape[1:], h.dtype), pltpu.VMEM((tm,) + h.shape[1:], h.dtype),
                            pltpu.SemaphoreType.DMA((2,)), pltpu.SemaphoreType.DMA(())],
        ),
        out_shape=jax.ShapeDtypeStruct((rows,) + h.shape[1:], h.dtype),
        compiler_params=_cparams("arbitrary"),
        name="moe_scatter",
    )(pos, tile_fill, h)


def _ffn_a_kernel(tid_ref, te_ref, tf_ref, nv_ref, x_ref, w1_ref, w3_ref, o_ref, w1b, w3b):
    i = pl.program_id(1)

    @pl.when(i < nv_ref[0])
    def _():
        @pl.when(tf_ref[i] == 1)
        def _():
            w1b[...] = w1_ref[0].astype(BF16)
            w3b[...] = w3_ref[0].astype(BF16)

        x = x_ref[...].reshape(x_ref.shape[0], -1)
        a = jnp.dot(x, w1b[...], preferred_element_type=F32)
        b = jnp.dot(x, w3b[...], preferred_element_type=F32)
        o_ref[...] = (a * _sigmoid(a) * b).astype(o_ref.dtype)

    @pl.when(i >= nv_ref[0])
    def _():
        o_ref[...] = jnp.zeros_like(o_ref)


def ffn_stage_a(xs, w1, w3, layer, plan, tm):
    _, _, _, tid, te, tf, nv = plan
    rows = xs.shape[0]
    d = xs.shape[1] * xs.shape[2]
    ff = w1.shape[3]
    tfc = _pick(ff, (512, 256, 128))
    wspec = pl.BlockSpec((None, 1, d, tfc), lambda f, i, tid, te, tf, nv: (layer, te[i], 0, f))
    return pl.pallas_call(
        _ffn_a_kernel,
        grid_spec=pltpu.PrefetchScalarGridSpec(
            num_scalar_prefetch=4,
            grid=(ff // tfc, rows // tm),
            in_specs=[pl.BlockSpec((tm,) + xs.shape[1:], lambda f, i, tid, te, tf, nv: (tid[i], 0, 0)), wspec, wspec],
            out_specs=pl.BlockSpec((tm, tfc), lambda f, i, tid, te, tf, nv: (i, f)),
            scratch_shapes=[pltpu.VMEM((d, tfc), BF16), pltpu.VMEM((d, tfc), BF16)],
        ),
        out_shape=jax.ShapeDtypeStruct((rows, ff), BF16),
        compiler_params=_cparams("arbitrary", "arbitrary"),
        name="moe_ffn_a",
    )(tid, te, tf, nv, xs, w1, w3)


def _ffn_b_kernel(tid_ref, te_ref, tf_ref, nv_ref, h_ref, w2_ref, o_ref, w2b):
    i = pl.program_id(1)

    @pl.when(i < nv_ref[0])
    def _():
        @pl.when(tf_ref[i] == 1)
        def _():
            w2b[...] = w2_ref[0].astype(BF16)

        y = jnp.dot(h_ref[...], w2b[...], preferred_element_type=F32)
        o_ref[...] = y.astype(o_ref.dtype).reshape(o_ref.shape)

    @pl.when(i >= nv_ref[0])
    def _():
        o_ref[...] = jnp.zeros_like(o_ref)


def ffn_stage_b(hs, w2, layer, plan, tm):
    _, _, _, tid, te, tf, nv = plan
    rows, ff = hs.shape
    d = w2.shape[3]
    tn = _pick(d, (2048, 1024, 512, 256, 128))
    return pl.pallas_call(
        _ffn_b_kernel,
        grid_spec=pltpu.PrefetchScalarGridSpec(
            num_scalar_prefetch=4,
            grid=(d // tn, rows // tm),
            in_specs=[
                pl.BlockSpec((tm, ff), lambda n, i, tid, te, tf, nv: (tid[i], 0)),
                pl.BlockSpec((None, 1, ff, tn), lambda n, i, tid, te, tf, nv: (layer, te[i], 0, n)),
            ],
            out_specs=pl.BlockSpec((tm, tn // LANES, LANES), lambda n, i, tid, te, tf, nv: (i, n, 0)),
            scratch_shapes=[pltpu.VMEM((ff, tn), BF16)],
        ),
        out_shape=jax.ShapeDtypeStruct((rows, d // LANES, LANES), BF16),
        compiler_params=_cparams("arbitrary", "arbitrary"),
        name="moe_ffn_b",
    )(tid, te, tf, nv, hs, w2)


def moe_sparse(h, logits_t, router_b, w1, w3, w2, layer, tile):
    gates, mask = router(logits_t, router_b)
    plan = dispatch_plan(gates, mask, MOE_TILE)
    xs = scatter_rows(h, plan[2], plan[0], MOE_TILE, tile)
    hs = ffn_stage_a(xs, w1, w3, layer, plan, MOE_TILE)
    ys = ffn_stage_b(hs, w2, layer, plan, MOE_TILE)
    return ys, plan[2], plan[1]


def rope_tables(n, lc, hd):
    half = hd // 2
    t = jnp.arange(n)
    inv = jnp.power(ROPE_BASE, -jnp.arange(0, half, 2, dtype=F32) / half)
    ang_r = (t // GRID_W).astype(F32)[:, None] * inv
    ang_c = (t % GRID_W).astype(F32)[:, None] * inv
    cos = jnp.concatenate([jnp.cos(ang_r)] * 2 + [jnp.cos(ang_c)] * 2, axis=1)
    sin = jnp.concatenate([-jnp.sin(ang_r), jnp.sin(ang_r), -jnp.sin(ang_c), jnp.sin(ang_c)], axis=1)
    cos = jnp.concatenate([jnp.ones((lc, hd), F32), cos], axis=0)
    sin = jnp.concatenate([jnp.zeros((lc, hd), F32), sin], axis=0)
    scale = hd ** -0.5
    ones, zeros = jnp.ones_like(cos), jnp.zeros_like(sin)
    return jnp.stack([cos * scale, cos, ones]), jnp.stack([sin * scale, sin, zeros])


def kernel(x, c, ctx, c_ctx, ada_w, ada_b, ln_g, ln_b, mlstm_w_in, mlstm_b_gates, mlstm_norm_w, mlstm_w_out,
           attn_w_in, attn_b_in, attn_sink, attn_w_out, router_w, router_b, moe_w1, moe_w3, moe_w2):
    bsz, n, d = x.shape
    lc = ctx.shape[1]
    depth = ada_w.shape[0]
    assert depth == 2 and bsz <= 4 and n % lc == 0 and lc % ML_CHUNK == 0
    alpha = (2 * depth) ** 0.25
    tile = lc
    per = (n + lc) // tile
    per_lat = n // tile

    cc = jnp.concatenate([c, c_ctx[None, :], jnp.zeros((8 - bsz - 1, d), F32)], axis=0)
    ada0 = adaln(cc, ada_w, ada_b, 0).reshape(8, 1, 6 * d)
    rwt = router_w.T

    heads = mlstm_b_gates.shape[1] // 4
    dv = d // heads
    dqk = dv // 2
    nmain = 2 * heads * (dqk + dv)
    w_in_t = jnp.swapaxes(mlstm_w_in, 1, 2)
    h0, gates_pre = mod0(x, ctx, ada0, w_in_t[0, nmain:, :], mlstm_b_gates[0][None, :])
    qkvo = matmul(h0, w_in_t, n_cols=nmain, w_t=True)
    ncc, nct = lc // ML_CHUNK, (n + lc) // ML_CHUNK
    hg, ada1 = mlstm_bidir(qkvo, gates_pre, mlstm_norm_w[0], bsz, heads, dqk, dv, ncc, nct, cc, ada_w, ada_b, 1)
    ada1 = ada1.reshape(8, 1, 6 * d)
    y = matmul(hg, mlstm_w_out)
    x1, hffn, lg = ln_block(x, y, ada0, 2, ln_g[0, 0], ln_b[0, 0], alpha=alpha, nb=bsz, per=per, tile=tile,
                            ctx_rows=True, ctx=ctx, nxt=(ada0, 3, 4), h_dtype=BF16, h_rows3d=True, router_wt=rwt)
    ys, pos, pos_w = moe_sparse(hffn, lg, router_b, moe_w1, moe_w3, moe_w2, 0, tile)
    x2, h1 = ln_block(x1, ys, ada0, 5, ln_g[0, 1], ln_b[0, 1], alpha=alpha, nb=bsz, per=per, tile=tile,
                      ctx_rows=True, x_per=per, nxt=(ada1, 0, 1), h_dtype=BF16, pos=pos, pos_w=pos_w)

    heads_a = attn_sink.shape[1]
    hd = d // heads_a
    kvh = (attn_w_in.shape[2] // hd - heads_a) // 2
    cos_t, sin_t = rope_tables(n, lc, hd)
    q_end, k_end = heads_a * hd, (heads_a + kvh) * hd
    type_of = lambda col: jnp.where(col < q_end, 0, jnp.where(col < k_end, 1, 2))
    tm1 = _pick(n + lc, (1152, 768, 384, 256, 128))
    qkv = matmul(h1, attn_w_in, bias=attn_b_in[0], rope=(cos_t, sin_t, n + lc, type_of), tm=tm1,
                 tn=_pick(kvh * hd, (512, 256, 128)))
    att = window_attention(qkv.reshape(bsz, n + lc, -1), attn_sink[0], lc, n, heads_a, kvh, hd)
    y = matmul(att.reshape(bsz * n, heads_a * hd), attn_w_out)
    x3, hffn, lg = ln_block(x2, y, ada1, 2, ln_g[1, 0], ln_b[1, 0], alpha=alpha, nb=bsz, per=per_lat, tile=tile,
                            ctx_rows=False, x_per=per, x_off=1, nxt=(ada1, 3, 4), h_dtype=BF16, h_rows3d=True,
                            router_wt=rwt)
    ys, pos, pos_w = moe_sparse(hffn, lg, router_b, moe_w1, moe_w3, moe_w2, 1, tile)
    (x4,) = ln_block(x3, ys, ada1, 5, ln_g[1, 1], ln_b[1, 1], alpha=alpha, nb=bsz, per=per_lat, tile=tile,
                     ctx_rows=False, x_per=per_lat, pos=pos, pos_w=pos_w)
    return x4.reshape(bsz, n, d)
```
